```python
import jax, jax.numpy as jnp
from jax import lax
import numpy as np

D_MODEL = 1024
BATCH = 16
SEQ = 2048
DEPTH = 4

HEAD_DIM = 64
N_HEADS = D_MODEL // HEAD_DIM
N_HEADS_SB = N_HEADS // 4
N_HEADS_DIL = (3 * N_HEADS) // 8
N_HEADS_FOX = N_HEADS - N_HEADS_SB - N_HEADS_DIL
D_MIX = N_HEADS * HEAD_DIM
N_IN = 3 * D_MIX + N_HEADS_FOX
DIL_PATTERNS = ((128, 1), (512, 4), (2048, 16))
Q_BLOCK = 128
ROPE_THETA = 10000.0
N_EXPERTS = 32
TOP_K = 4
D_FF = D_MODEL
SWIGLU_LIMIT = 7.0
SWIGLU_ALPHA = 1.702
MOE_BLOCK = 128
N_MOD = 6
EPS = 1e-6

kernel_name = "hybrid_sb_dilated_fox_moe_adaln"


def _rms(x):
    xf = x.astype(jnp.float32)
    return (xf * lax.rsqrt(jnp.mean(xf * xf, axis=-1, keepdims=True) + EPS)).astype(x.dtype)


def rmsnorm(x, g):
    return _rms(x) * g


def apply_rope(t, positions):
    dh = t.shape[-1]
    inv_freq = ROPE_THETA ** (-jnp.arange(0, dh, 2, dtype=jnp.float32) / dh)
    ang = positions.astype(jnp.float32)[:, None, :, None] * inv_freq
    cos, sin = jnp.cos(ang), jnp.sin(ang)
    tf = t.astype(jnp.float32)
    t1, t2 = tf[..., : dh // 2], tf[..., dh // 2:]
    return jnp.concatenate([t1 * cos - t2 * sin, t2 * cos + t1 * sin], axis=-1).astype(t.dtype)


def _to_query_blocks(t):
    b, h, s = t.shape[:3]
    nq = s // Q_BLOCK
    t = t.reshape((b, h, nq, Q_BLOCK) + t.shape[3:])
    return jnp.moveaxis(t, 2, 0)


def _from_query_blocks(t):
    nq, b, h, qb, dh = t.shape
    return jnp.moveaxis(t, 0, 2).reshape(b, h, nq * qb, dh)


def stick_breaking_attention(q, k, v):
    b, h, s, dh = q.shape
    scale = dh ** -0.5
    key_pos = jnp.arange(s)

    def block(args):
        q_blk, start = args
        z = jnp.einsum('bhqd,bhkd->bhqk', q_blk, k, preferred_element_type=jnp.float32) * scale
        qpos = start + jnp.arange(Q_BLOCK)
        mask = key_pos[None, :] < qpos[:, None]
        log_not = jnp.where(mask, jax.nn.log_sigmoid(-z), 0.0)
        between = lax.cumsum(log_not, axis=3, reverse=True) - log_not
        weights = jnp.where(mask, jnp.exp(jax.nn.log_sigmoid(z) + between), 0.0)
        return jnp.einsum('bhqk,bhkd->bhqd', weights.astype(v.dtype), v)

    starts = jnp.arange(s // Q_BLOCK, dtype=jnp.int32) * Q_BLOCK
    return _from_query_blocks(lax.map(block, (_to_query_blocks(q), starts)))


def forgetting_attention(q, k, v, log_f):
    b, h, s, dh = q.shape
    scale = dh ** -0.5
    key_pos = jnp.arange(s)
    cum = jnp.cumsum(log_f, axis=-1)

    def block(args):
        q_blk, cum_q, start = args
        logits = jnp.einsum('bhqd,bhkd->bhqk', q_blk, k, preferred_element_type=jnp.float32) * scale
        logits = logits + cum_q[..., :, None] - cum[..., None, :]
        qpos = start + jnp.arange(Q_BLOCK)
        mask = key_pos[None, :] <= qpos[:, None]
        p = jax.nn.softmax(jnp.where(mask, logits, -jnp.inf), axis=-1)
        return jnp.einsum('bhqk,bhkd->bhqd', p.astype(v.dtype), v)

    starts = jnp.arange(s // Q_BLOCK, dtype=jnp.int32) * Q_BLOCK
    return _from_query_blocks(lax.map(block, (_to_query_blocks(q), _to_query_blocks(cum), starts)))


def banded_window_attention(q, k, v, window):
    *lead, L, dh = q.shape
    nlead = len(lead)
    blk = window
    nb = -(-L // blk)
    pad = nb * blk - L
    scale = dh ** -0.5
    qp = jnp.pad(q, [(0, 0)] * nlead + [(0, pad), (0, 0)])
    kp = jnp.pad(k, [(0, 0)] * nlead + [(blk, pad), (0, 0)])
    vp = jnp.pad(v, [(0, 0)] * nlead + [(blk, pad), (0, 0)])
    qb = qp.reshape(*lead, nb, blk, dh)
    kb = kp.reshape(*lead, nb + 1, blk, dh)
    vb = vp.reshape(*lead, nb + 1, blk, dh)
    kcat = jnp.concatenate([kb[..., :-1, :, :], kb[..., 1:, :, :]], axis=-2)
    vcat = jnp.concatenate([vb[..., :-1, :, :], vb[..., 1:, :, :]], axis=-2)
    logits = jnp.einsum('...nqd,...nkd->...nqk', qb, kcat, preferred_element_type=jnp.float32) * scale
    dist = (jnp.arange(blk)[:, None] + blk) - jnp.arange(2 * blk)[None, :]
    in_band = (dist >= 0) & (dist <= window)
    not_front = (jnp.arange(nb)[:, None, None] > 0) | (jnp.arange(2 * blk)[None, None, :] >= blk)
    mask = in_band[None] & not_front
    logits = jnp.where(mask, logits, -jnp.inf)
    m = jnp.max(logits, axis=-1, keepdims=True)
    p = jnp.exp(logits - m)
    denom = jnp.sum(p, axis=-1)
    o = jnp.einsum('...nqk,...nkd->...nqd', p.astype(v.dtype), vcat).astype(jnp.float32) / denom[..., None]
    lse = m[..., 0] + jnp.log(denom)
    o = o.reshape(*lead, nb * blk, dh)[..., :L, :]
    lse = lse.reshape(*lead, nb * blk)[..., :L]
    return o, lse


def _stride_split(t, dil):
    b, h, s, dh = t.shape
    return t.reshape(b, h, s // dil, dil, dh).transpose(0, 1, 3, 2, 4)


def dilated_attention(q, k, v):
    b, h, s, dh = q.shape
    outs, lses = [], []
    for window, dil in DIL_PATTERNS:
        o, lse = banded_window_attention(_stride_split(q, dil), _stride_split(k, dil),
                                         _stride_split(v, dil), window // dil)
        outs.append(o.transpose(0, 1, 3, 2, 4).reshape(b, h, s, dh))
        lses.append(lse.transpose(0, 1, 3, 2).reshape(b, h, s))
    alpha = jax.nn.softmax(jnp.stack(lses, axis=0), axis=0)
    out = alpha[0][..., None] * outs[0]
    for i in range(1, len(DIL_PATTERNS)):
        out = out + alpha[i][..., None] * outs[i]
    return out.astype(q.dtype)


def hybrid_mixer(h, positions, w_in, b_forget, g_mix, w_out):
    b, s, _ = h.shape
    proj = h @ w_in
    q = proj[..., :D_MIX]
    k = proj[..., D_MIX:2 * D_MIX]
    v = proj[..., 2 * D_MIX:3 * D_MIX]
    f_logit = proj[..., 3 * D_MIX:]

    def heads(t):
        return t.reshape(b, s, N_HEADS, HEAD_DIM).transpose(0, 2, 1, 3)

    q, k, v = heads(q), heads(k), heads(v)
    i1 = N_HEADS_SB
    i2 = N_HEADS_SB + N_HEADS_DIL
    o_sb = stick_breaking_attention(q[:, :i1], k[:, :i1], v[:, :i1])
    o_dil = dilated_attention(apply_rope(q[:, i1:i2], positions), apply_rope(k[:, i1:i2], positions), v[:, i1:i2])
    log_f = jax.nn.log_sigmoid(f_logit.astype(jnp.float32) + b_forget.astype(jnp.float32)).transpose(0, 2, 1)
    o_fox = forgetting_attention(q[:, i2:], k[:, i2:], v[:, i2:], log_f)

    def merge(o):
        return _rms(o.transpose(0, 2, 1, 3).reshape(b, s, -1))

    o = jnp.concatenate([merge(o_sb), merge(o_dil), merge(o_fox)], axis=-1) * g_mix
    return o @ w_out


def moe_ffn(h, w_router, b_router, w_gu, b_gu, w_dn, b_dn):
    b, s, d = h.shape
    n = b * s
    xt = h.reshape(n, d)
    logits = (xt @ w_router).astype(jnp.float32) + b_router.astype(jnp.float32)
    top_val, top_idx = lax.top_k(logits, TOP_K)
    gates = jax.nn.softmax(top_val, axis=-1)
    n_assign = n * TOP_K
    flat_e = top_idx.reshape(-1).astype(jnp.int32)
    flat_tok = jnp.repeat(jnp.arange(n, dtype=jnp.int32), TOP_K)
    flat_g = gates.reshape(-1)
    order = jnp.argsort(flat_e)
    se, st, sg = flat_e[order], flat_tok[order], flat_g[order]
    counts = jnp.bincount(flat_e, length=N_EXPERTS).astype(jnp.int32)
    starts = jnp.cumsum(counts) - counts
    pcounts = (counts + MOE_BLOCK - 1) // MOE_BLOCK * MOE_BLOCK
    pends = jnp.cumsum(pcounts)
    pstarts = pends - pcounts
    ppos = pstarts[se] + (jnp.arange(n_assign, dtype=jnp.int32) - starts[se])
    nblk = -(-(n_assign + N_EXPERTS * (MOE_BLOCK - 1)) // MOE_BLOCK)
    cap = nblk * MOE_BLOCK
    row_tok = jnp.zeros((cap,), jnp.int32).at[ppos].set(st)
    row_gate = jnp.zeros((cap,), jnp.float32).at[ppos].set(sg)
    blk_e = jnp.minimum(jnp.searchsorted(pends, jnp.arange(nblk, dtype=jnp.int32) * MOE_BLOCK, side='right'),
                        N_EXPERTS - 1).astype(jnp.int32)

    def step(y, args):
        tok, g, e = args
        xb = xt[tok]
        gu = xb @ w_gu[e] + b_gu[e]
        gate, up = gu[:, :D_FF], gu[:, D_FF:]
        gate = jnp.minimum(gate, SWIGLU_LIMIT)
        up = jnp.clip(up, -SWIGLU_LIMIT, SWIGLU_LIMIT)
        act = (up + 1) * (gate * jax.nn.sigmoid(SWIGLU_ALPHA * gate))
        out = act @ w_dn[e] + b_dn[e]
        return y.at[tok].add((g[:, None] * out).astype(y.dtype)), None

    y, _ = lax.scan(step, jnp.zeros_like(xt),
                    (row_tok.reshape(nblk, MOE_BLOCK), row_gate.reshape(nblk, MOE_BLOCK), blk_e))
    return y.reshape(b, s, d)


def setup_inputs(seed: int = 0) -> dict:
    key = jax.random.key(seed)
    ks = jax.random.split(key, 20)

    def nrm(k, shape, scale):
        return jax.random.normal(k, shape, jnp.float32) * scale

    D = D_MODEL
    x = nrm(ks[0], (BATCH, SEQ, D), 1.0)
    c = nrm(ks[1], (BATCH, D), 1.0)
    positions = (jnp.arange(SEQ, dtype=jnp.int32)[None, :]
                 + jax.random.randint(ks[2], (BATCH, 1), 0, 4096, dtype=jnp.int32))
    w_mod = nrm(ks[3], (DEPTH, D, N_MOD * D), 0.5 * D ** -0.5)
    b_mod = nrm(ks[4], (DEPTH, N_MOD * D), 0.02)
    g_attn = 1.0 + nrm(ks[5], (DEPTH, D), 0.02)
    w_in = nrm(ks[6], (DEPTH, D, N_IN), D ** -0.5)
    b_forget = jax.random.uniform(ks[7], (DEPTH, N_HEADS_FOX), jnp.float32, 1.0, 5.0)
    g_mix = 1.0 + nrm(ks[8], (DEPTH, D_MIX), 0.02)
    w_out = nrm(ks[9], (DEPTH, D_MIX, D), D_MIX ** -0.5)
    g_ffn = 1.0 + nrm(ks[10], (DEPTH, D), 0.02)
    w_router = nrm(ks[11], (DEPTH, D, N_EXPERTS), D ** -0.5)
    b_router = nrm(ks[12], (DEPTH, N_EXPERTS), 0.01)
    w_gate_up = nrm(ks[13], (DEPTH, N_EXPERTS, D, 2 * D_FF), D ** -0.5)
    b_gate_up = nrm(ks[14], (DEPTH, N_EXPERTS, 2 * D_FF), 0.02)
    w_down = nrm(ks[15], (DEPTH, N_EXPERTS, D_FF, D), D_FF ** -0.5)
    b_down = nrm(ks[16], (DEPTH, N_EXPERTS, D), 0.02)
    g_final = 1.0 + nrm(ks[17], (D,), 0.02)
    return {"x": x, "c": c, "positions": positions, "w_mod": w_mod, "b_mod": b_mod,
            "g_attn": g_attn, "w_in": w_in, "b_forget": b_forget, "g_mix": g_mix, "w_out": w_out,
            "g_ffn": g_ffn, "w_router": w_router, "b_router": b_router, "w_gate_up": w_gate_up,
            "b_gate_up": b_gate_up, "w_down": w_down, "b_down": b_down, "g_final": g_final}


def reference(x, c, positions, w_mod, b_mod, g_attn, w_in, b_forget, g_mix, w_out, g_ffn,
              w_router, b_router, w_gate_up, b_gate_up, w_down, b_down, g_final):
    c_act = jax.nn.silu(c)
    for layer in range(DEPTH):
        mod = (c_act @ w_mod[layer] + b_mod[layer])[:, None, :]
        sh1, sc1, ga1, sh2, sc2, ga2 = jnp.split(mod, N_MOD, axis=-1)
        h = rmsnorm(x, g_attn[layer]) * (1 + sc1) + sh1
        x = x + ga1 * hybrid_mixer(h, positions, w_in[layer], b_forget[layer], g_mix[layer], w_out[layer])
        h = rmsnorm(x, g_ffn[layer]) * (1 + sc2) + sh2
        x = x + ga2 * moe_ffn(h, w_router[layer], b_router[layer], w_gate_up[layer], b_gate_up[layer],
                              w_down[layer], b_down[layer])
    return rmsnorm(x, g_final)
```

```python
import functools

import numpy as np
import jax
import jax.numpy as jnp
from jax import lax
from jax.experimental import pallas as pl
from jax.experimental.pallas import tpu as pltpu

F32 = jnp.float32
BF16 = jnp.bfloat16
HIGHEST = lax.Precision.HIGHEST

HEAD_DIM = 64
LANES = 128
N_HEADS_SB = 4
N_HEADS_DIL = 6
N_HEADS_FOX = 6
DIL_PATTERNS = ((128, 1), (512, 4), (2048, 16))
ROPE_THETA = 10000.0
TOP_K = 4
SWIGLU_LIMIT = 7.0
SWIGLU_ALPHA = 1.702
N_MOD = 6
EPS = 1e-6
NEG_BIG = -1e30
ATTN_TILE = 256
ROW_TILE = 512
EXPERT_TILE = 512
DMA_TILE = 256
VMEM_LIMIT = 48 * 1024 * 1024


def _params(sem, vmem=VMEM_LIMIT):
    return pltpu.CompilerParams(dimension_semantics=sem, vmem_limit_bytes=vmem)


def _rms_rows(x):
    return x * lax.rsqrt(jnp.mean(x * x, axis=-1, keepdims=True) + EPS)


def _softplus(z):
    return jnp.maximum(z, 0.0) + jnp.log(1.0 + jnp.exp(-jnp.abs(z)))


def _mod_kernel(c_ref, w_ref, b_ref, o_ref):
    c = c_ref[...]
    ca = c * jax.nn.sigmoid(c)
    o_ref[...] = jnp.dot(ca, w_ref[...], preferred_element_type=F32, precision=HIGHEST) + b_ref[...]


def modulation(c, w_mod, b_mod):
    depth, d, n6 = w_mod.shape
    b = c.shape[0]
    tn = min(n6, 1536)
    return pl.pallas_call(
        _mod_kernel,
        grid=(depth, n6 // tn),
        in_specs=[pl.BlockSpec((b, d), lambda l, j: (0, 0)),
                  pl.BlockSpec((None, d, tn), lambda l, j: (l, 0, j)),
                  pl.BlockSpec((None, 1, tn), lambda l, j: (l, 0, j))],
        out_specs=pl.BlockSpec((None, b, tn), lambda l, j: (l, 0, j)),
        out_shape=jax.ShapeDtypeStruct((depth, b, n6), F32),
        compiler_params=_params(("arbitrary", "arbitrary")),
        name="modulation",
    )(c, w_mod, b_mod.reshape(depth, 1, n6))


def _rope_table_kernel(pos_ref, invf_ref, cos_ref, sin_ref):
    ang = pos_ref[...].astype(F32) * invf_ref[...]
    lane = lax.broadcasted_iota(jnp.int32, ang.shape, 1)
    first_half = (lane % HEAD_DIM) < (HEAD_DIM // 2)
    s = jnp.sin(ang)
    cos_ref[...] = jnp.cos(ang)
    sin_ref[...] = jnp.where(first_half, -s, s)


def rope_tables(positions):
    b, s = positions.shape
    half = HEAD_DIM // 2
    inv_freq = ROPE_THETA ** (-np.arange(0, HEAD_DIM, 2, dtype=np.float64) / HEAD_DIM)
    invf = jnp.asarray(np.tile(inv_freq, LANES // half)[None, :], F32)
    return pl.pallas_call(
        _rope_table_kernel,
        grid=(b,),
        in_specs=[pl.BlockSpec((None, s, 1), lambda i: (i, 0, 0)),
                  pl.BlockSpec((1, LANES), lambda i: (0, 0))],
        out_specs=[pl.BlockSpec((None, s, LANES), lambda i: (i, 0, 0))] * 2,
        out_shape=[jax.ShapeDtypeStruct((b, s, LANES), F32)] * 2,
        compiler_params=_params(("arbitrary",)),
        name="rope_tables",
    )(positions.reshape(b, s, 1), invf)


def _rope(x, cos, sin_signed):
    lane = lax.broadcasted_iota(jnp.int32, x.shape, 1)
    first_half = (lane % HEAD_DIM) < (HEAD_DIM // 2)
    half = HEAD_DIM // 2
    partner = jnp.where(first_half, pltpu.roll(x, LANES - half, 1), pltpu.roll(x, half, 1))
    return x * cos + partner * sin_signed


def _inproj_kernel(x_ref, sh_ref, sc_ref, g_ref, wqkv_ref, wf_ref, bf_ref, qkv_ref, logf_ref, *, d_mix):
    x = x_ref[...]
    h = _rms_rows(x) * g_ref[...] * (1.0 + sc_ref[...]) + sh_ref[...]
    hb = h.astype(BF16)
    for j in range(3):
        y = jnp.dot(hb, wqkv_ref[:, j * d_mix:(j + 1) * d_mix], preferred_element_type=F32)
        if j == 0:
            y = y * (HEAD_DIM ** -0.5)
        qkv_ref[:, j * d_mix:(j + 1) * d_mix] = y.astype(BF16)
    f = jnp.dot(hb, wf_ref[...], preferred_element_type=F32) + bf_ref[...]
    logf_ref[...] = -_softplus(-f)


def in_projection(x, mod, g, w_qkv, w_f, b_f):
    b, s, d = x.shape
    d_mix = w_qkv.shape[1] // 3
    tm = min(ROW_TILE, s)
    return pl.pallas_call(
        functools.partial(_inproj_kernel, d_mix=d_mix),
        grid=(b, s // tm),
        in_specs=[pl.BlockSpec((None, tm, d), lambda i, j: (i, j, 0)),
                  pl.BlockSpec((None, 1, d), lambda i, j: (i, 0, 0)),
                  pl.BlockSpec((None, 1, d), lambda i, j: (i, 0, 1)),
                  pl.BlockSpec((1, d), lambda i, j: (0, 0)),
                  pl.BlockSpec((d, 3 * d_mix), lambda i, j: (0, 0)),
                  pl.BlockSpec((d, LANES), lambda i, j: (0, 0)),
                  pl.BlockSpec((1, LANES), lambda i, j: (0, 0))],
        out_specs=[pl.BlockSpec((None, tm, 3 * d_mix), lambda i, j: (i, j, 0)),
                   pl.BlockSpec((None, tm, LANES), lambda i, j: (i, j, 0))],
        out_shape=[jax.ShapeDtypeStruct((b, s, 3 * d_mix), BF16),
                   jax.ShapeDtypeStruct((b, s, LANES), F32)],
        compiler_params=_params(("arbitrary", "arbitrary")),
        name="in_projection",
    )(x, mod, mod, g, w_qkv, w_f, b_f)


def _cumsum_kernel(logf_ref, col_ref, row_ref):
    s = logf_ref.shape[0]
    r = lax.broadcasted_iota(jnp.int32, (LANES, LANES), 0)
    c = lax.broadcasted_iota(jnp.int32, (LANES, LANES), 1)
    tri = (c <= r).astype(F32)
    carry = jnp.zeros((1, LANES), F32)
    for i in range(s // LANES):
        blk = logf_ref[i * LANES:(i + 1) * LANES, :]
        cs = jnp.dot(tri, blk, preferred_element_type=F32, precision=HIGHEST) + carry
        carry = cs[LANES - 1:LANES, :]
        col_ref[i * LANES:(i + 1) * LANES, :] = cs
        row_ref[i] = cs.T[0:8, :]


def forget_cumsum(logf):
    b, s, _ = logf.shape
    return pl.pallas_call(
        _cumsum_kernel,
        grid=(b,),
        in_specs=[pl.BlockSpec((None, s, LANES), lambda i: (i, 0, 0))],
        out_specs=[pl.BlockSpec((None, s, LANES), lambda i: (i, 0, 0)),
                   pl.BlockSpec((None, s // LANES, 8, LANES), lambda i: (i, 0, 0, 0))],
        out_shape=[jax.ShapeDtypeStruct((b, s, LANES), F32),
                   jax.ShapeDtypeStruct((b, s // LANES, 8, LANES), F32)],
        compiler_params=_params(("arbitrary",)),
        name="forget_cumsum",
    )(logf)


def _tile_iotas(t):
    return lax.broadcasted_iota(jnp.int32, (t, t), 0), lax.broadcasted_iota(jnp.int32, (t, t), 1)


def _head(ref, start, size, hh):
    return ref[pl.ds(start, size), hh * HEAD_DIM:(hh + 1) * HEAD_DIM]


def _qk(q, k):
    return lax.dot_general(q, k, (((1,), (1,)), ((), ())), preferred_element_type=F32)


def _sb_kernel(q_ref, k_ref, v_ref, o_ref, *, t):
    qi = pl.program_id(2)
    row, col = _tile_iotas(t)
    strict = col < row
    upper = (row > col).astype(BF16)
    outs = []
    for hh in range(2):
        q = q_ref[:, hh * HEAD_DIM:(hh + 1) * HEAD_DIM]

        def tile(ki, carry, diag, q=q, hh=hh):
            suffix, acc = carry
            ks = pl.multiple_of(ki * t, t)
            z = _qk(q, _head(k_ref, ks, t, hh))
            sp = _softplus(z)
            log_not = -sp
            if diag:
                log_not = jnp.where(strict, log_not, 0.0)
            hi = log_not.astype(BF16)
            lo = (log_not - hi.astype(F32)).astype(BF16)
            between = (jnp.dot(hi, upper, preferred_element_type=F32)
                       + jnp.dot(lo, upper, preferred_element_type=F32))
            w = jnp.exp(z - sp + between + suffix)
            if diag:
                w = jnp.where(strict, w, 0.0)
            acc = acc + jnp.dot(w.astype(BF16), _head(v_ref, ks, t, hh), preferred_element_type=F32)
            suffix = suffix + between[:, 0:1] + log_not[:, 0:1]
            return suffix, acc

        carry = tile(qi, (jnp.zeros((t, 1), F32), jnp.zeros((t, HEAD_DIM), F32)), True)
        carry = lax.fori_loop(0, qi, lambda j, cr: tile(qi - 1 - j, cr, False), carry)
        outs.append(carry[1])
    o_ref[...] = jnp.concatenate(outs, axis=-1).astype(o_ref.dtype)


def _online_softmax_step(s, v, carry, weight=None):
    m, l, acc = carry
    m_new = jnp.maximum(m, jnp.max(s, axis=-1, keepdims=True))
    alpha = jnp.exp(m - m_new)
    p = jnp.exp(s - m_new)
    if weight is not None:
        p = p * weight
    l = alpha * l + jnp.sum(p, axis=-1, keepdims=True)
    acc = alpha * acc + jnp.dot(p.astype(BF16), v, preferred_element_type=F32)
    return m_new, l, acc


def _softmax_init(t):
    return (jnp.full((t, 1), NEG_BIG, F32), jnp.zeros((t, 1), F32), jnp.zeros((t, HEAD_DIM), F32))


def _fox_kernel(q_ref, k_ref, v_ref, ccol_ref, crow_ref, o_ref, *, t):
    p = pl.program_id(1)
    qi = pl.program_id(2)
    row, col = _tile_iotas(t)
    lane = lax.broadcasted_iota(jnp.int32, (t, LANES), 1)
    nch = t // LANES
    outs = []
    for hh in range(2):
        hidx = 2 * p + hh
        q = q_ref[:, hh * HEAD_DIM:(hh + 1) * HEAD_DIM]
        cq = jnp.sum(jnp.where(lane == hidx, ccol_ref[...], 0.0), axis=-1, keepdims=True)

        def tile(ki, carry, diag, q=q, hh=hh, hidx=hidx, cq=cq):
            ks = pl.multiple_of(ki * t, t)
            ck = jnp.concatenate([crow_ref[ki * nch + c, pl.ds(hidx, 1), :] for c in range(nch)], axis=-1)
            s = _qk(q, _head(k_ref, ks, t, hh)) + (cq - ck)
            if diag:
                s = jnp.where(col <= row, s, NEG_BIG)
            return _online_softmax_step(s, _head(v_ref, ks, t, hh), carry)

        carry = lax.fori_loop(0, qi, lambda ki, cr: tile(ki, cr, False), _softmax_init(t))
        _, l, acc = tile(qi, carry, True)
        outs.append(acc / l)
    o_ref[...] = jnp.concatenate(outs, axis=-1).astype(o_ref.dtype)


def _dilation_count(d):
    cnt = None
    for window, dil in DIL_PATTERNS:
        hit = (d >= 0) & (d <= window) & ((d & (dil - 1)) == 0)
        term = jnp.where(hit, 1.0, 0.0)
        cnt = term if cnt is None else cnt + term
    return cnt


def _dil_kernel(q_ref, k_ref, v_ref, cos_ref, sin_ref, o_ref, kr_ref, *, t):
    qi = pl.program_id(2)

    @pl.when(qi == 0)
    def _():
        kr_ref[...] = _rope(k_ref[...].astype(F32), cos_ref[...], sin_ref[...]).astype(kr_ref.dtype)

    qs = pl.multiple_of(qi * t, t)
    qr = _rope(q_ref[...].astype(F32), cos_ref[pl.ds(qs, t), :], sin_ref[pl.ds(qs, t), :]).astype(BF16)
    row, col = _tile_iotas(t)
    outs = []
    for hh in range(2):
        q = qr[:, hh * HEAD_DIM:(hh + 1) * HEAD_DIM]

        def tile(ki, carry, q=q, hh=hh):
            ks = pl.multiple_of(ki * t, t)
            cnt = _dilation_count((qi - ki) * t + row - col)
            s = jnp.where(cnt > 0.0, _qk(q, _head(kr_ref, ks, t, hh)), NEG_BIG)
            return _online_softmax_step(s, _head(v_ref, ks, t, hh), carry, weight=cnt)

        _, l, acc = lax.fori_loop(0, qi + 1, tile, _softmax_init(t))
        outs.append(acc / l)
    o_ref[...] = jnp.concatenate(outs, axis=-1).astype(o_ref.dtype)


def _attention_call(kernel, qkv, head0, n_heads, extra_inputs, extra_specs, scratch, name):
    b, s, w3 = qkv.shape
    nblk = w3 // 3 // LANES
    pairs = n_heads // 2
    p0 = head0 // 2
    t = min(ATTN_TILE, s)
    qspec = pl.BlockSpec((None, t, LANES), lambda i, p, j: (i, j, p0 + p))
    kspec = pl.BlockSpec((None, s, LANES), lambda i, p, j: (i, 0, nblk + p0 + p))
    vspec = pl.BlockSpec((None, s, LANES), lambda i, p, j: (i, 0, 2 * nblk + p0 + p))
    return pl.pallas_call(
        functools.partial(kernel, t=t),
        grid=(b, pairs, s // t),
        in_specs=[qspec, kspec, vspec] + extra_specs,
        out_specs=pl.BlockSpec((None, t, LANES), lambda i, p, j: (i, j, p)),
        out_shape=jax.ShapeDtypeStruct((b, s, pairs * LANES), BF16),
        scratch_shapes=scratch,
        compiler_params=_params(("arbitrary", "arbitrary", "arbitrary")),
        name=name,
    )(qkv, qkv, qkv, *extra_inputs)


def sb_attention(qkv):
    return _attention_call(_sb_kernel, qkv, 0, N_HEADS_SB, [], [], [], "sb_attention")


def dil_attention(qkv, cos, sin):
    s = qkv.shape[1]
    tab = pl.BlockSpec((None, s, LANES), lambda i, p, j: (i, 0, 0))
    return _attention_call(_dil_kernel, qkv, N_HEADS_SB, N_HEADS_DIL, [cos, sin], [tab, tab],
                           [pltpu.VMEM((s, LANES), BF16)], "dil_attention")


def fox_attention(qkv, ccol, crow):
    s = qkv.shape[1]
    t = min(ATTN_TILE, s)
    specs = [pl.BlockSpec((None, t, LANES), lambda i, p, j: (i, j, 0)),
             pl.BlockSpec((None, s // LANES, 8, LANES), lambda i, p, j: (i, 0, 0, 0))]
    return _attention_call(_fox_kernel, qkv, N_HEADS_SB + N_HEADS_DIL, N_HEADS_FOX, [ccol, crow], specs,
                           [], "fox_attention")


def _outproj_router_kernel(osb_ref, odil_ref, ofox_ref, gmix_ref, wout_ref, x_ref, ga_ref, sc_ref, sh_ref,
                           gffn_ref, wr_ref, br_ref,
                           xo_ref, h2_ref, idx_ref, gate_ref, rank_ref, cnt_ref, carry_ref):
    first = (pl.program_id(0) == 0) & (pl.program_id(1) == 0)

    @pl.when(first)
    def _():
        carry_ref[...] = jnp.zeros_like(carry_ref)

    a = None
    lo = 0
    for o_ref in (osb_ref, odil_ref, ofox_ref):
        w = o_ref.shape[-1]
        on = (_rms_rows(o_ref[...].astype(F32)) * gmix_ref[:, lo:lo + w]).astype(BF16)
        part = jnp.dot(on, wout_ref[lo:lo + w, :], preferred_element_type=F32)
        a = part if a is None else a + part
        lo += w
    xn = x_ref[...] + ga_ref[...] * a
    xo_ref[...] = xn
    h2 = _rms_rows(xn) * gffn_ref[...] * (1.0 + sc_ref[...]) + sh_ref[...]
    h2_ref[...] = h2

    logits = jnp.dot(h2, wr_ref[...], preferred_element_type=F32, precision=HIGHEST) + br_ref[...]
    tm = logits.shape[0]
    lane = lax.broadcasted_iota(jnp.int32, (tm, LANES), 1)
    vals, idxs = [], []
    rest = logits
    for _ in range(TOP_K):
        m = jnp.max(rest, axis=-1, keepdims=True)
        ik = jnp.min(jnp.where(rest == m, lane, LANES), axis=-1, keepdims=True)
        vals.append(m)
        idxs.append(ik)
        rest = jnp.where(lane == ik, -jnp.inf, rest)
    es = [jnp.exp(v - vals[0]) for v in vals]
    den = es[0] + es[1] + es[2] + es[3]

    hot = [(lane == ik) for ik in idxs]
    multi = jnp.where(hot[0] | hot[1] | hot[2] | hot[3], 1.0, 0.0)
    r = lax.broadcasted_iota(jnp.int32, (tm, tm), 0)
    c = lax.broadcasted_iota(jnp.int32, (tm, tm), 1)
    before = (c < r).astype(BF16)
    prior = jnp.dot(before, multi.astype(BF16), preferred_element_type=F32) + carry_ref[...]
    idx_out = jnp.zeros((tm, LANES), jnp.int32)
    gate_out = jnp.zeros((tm, LANES), F32)
    rank_out = jnp.zeros((tm, LANES), F32)
    for k in range(TOP_K):
        rk = jnp.sum(jnp.where(hot[k], prior, 0.0), axis=-1, keepdims=True)
        idx_out = jnp.where(lane == k, idxs[k], idx_out)
        gate_out = jnp.where(lane == k, es[k] / den, gate_out)
        rank_out = jnp.where(lane == k, rk, rank_out)
    idx_ref[...] = idx_out
    gate_ref[...] = gate_out
    rank_ref[...] = rank_out.astype(jnp.int32)
    total = prior[tm - 1:tm, :] + multi[tm - 1:tm, :]
    carry_ref[...] = total
    cnt_ref[...] = total.astype(jnp.int32)


def outproj_router(o_sb, o_dil, o_fox, g_mix, w_out, x, mod, g_ffn, w_r, b_r):
    b, s, d = x.shape
    tm = min(ROW_TILE, s)
    n = b * s

    def act(w):
        return pl.BlockSpec((None, tm, w), lambda i, j: (i, j, 0))

    def modspec(col):
        return pl.BlockSpec((None, 1, d), lambda i, j: (i, 0, col))

    def const(shape):
        return pl.BlockSpec(shape, lambda i, j: (0, 0))

    tok = pl.BlockSpec((tm, LANES), lambda i, j: (i * (s // tm) + j, 0))
    return pl.pallas_call(
        _outproj_router_kernel,
        grid=(b, s // tm),
        in_specs=[act(o_sb.shape[-1]), act(o_dil.shape[-1]), act(o_fox.shape[-1]),
                  const((1, d)), const(w_out.shape), act(d),
                  modspec(2), modspec(4), modspec(3),
                  const((1, d)), const((d, LANES)), const((1, LANES))],
        out_specs=[act(d), pl.BlockSpec((tm, d), lambda i, j: (i * (s // tm) + j, 0)), tok, tok, tok,
                   const((1, LANES))],
        out_shape=[jax.ShapeDtypeStruct((b, s, d), F32), jax.ShapeDtypeStruct((n, d), F32),
                   jax.ShapeDtypeStruct((n, LANES), jnp.int32), jax.ShapeDtypeStruct((n, LANES), F32),
                   jax.ShapeDtypeStruct((n, LANES), jnp.int32), jax.ShapeDtypeStruct((1, LANES), jnp.int32)],
        scratch_shapes=[pltpu.VMEM((1, LANES), F32)],
        compiler_params=_params(("arbitrary", "arbitrary")),
        name="outproj_router",
    )(o_sb, o_dil, o_fox, g_mix, w_out, x, mod, mod, mod, g_ffn, w_r, b_r)


def _row_copy(src_hbm, dst_hbm, sem, src_row, dst_row):
    return pltpu.make_async_copy(src_hbm.at[pl.ds(src_row, 1)], dst_hbm.at[pl.ds(dst_row, 1)], sem)


def _dispatch_kernel(ppos_ref, h_hbm, zeros_hbm, xs_hbm, sem, *, tg):
    del zeros_hbm
    base = pl.program_id(0) * tg

    def issue(r, _):
        tkn = base + r
        for k in range(TOP_K):
            _row_copy(h_hbm, xs_hbm, sem, tkn, ppos_ref[tkn * TOP_K + k]).start()
        return 0

    def drain(r, _):
        for k in range(TOP_K):
            _row_copy(h_hbm, xs_hbm, sem, 0, 0).wait()
        return 0

    lax.fori_loop(0, tg, issue, 0)
    lax.fori_loop(0, tg, drain, 0)


def dispatch_rows(ppos, h, cap):
    n, d = h.shape
    tg = min(DMA_TILE, n)
    zeros = jnp.zeros((cap, d), h.dtype)
    return pl.pallas_call(
        functools.partial(_dispatch_kernel, tg=tg),
        grid_spec=pltpu.PrefetchScalarGridSpec(
            num_scalar_prefetch=1,
            grid=(n // tg,),
            in_specs=[pl.BlockSpec(memory_space=pl.ANY), pl.BlockSpec(memory_space=pl.ANY)],
            out_specs=pl.BlockSpec(memory_space=pl.ANY),
            scratch_shapes=[pltpu.SemaphoreType.DMA(())]),
        out_shape=jax.ShapeDtypeStruct((cap, d), h.dtype),
        input_output_aliases={2: 0},
        compiler_params=pltpu.CompilerParams(dimension_semantics=("arbitrary",), has_side_effects=True),
        name="dispatch_rows",
    )(ppos, h, zeros)


def _expert_kernel(blk_e_ref, nused_ref, xs_ref, wgu_ref, bgu_ref, wdn_ref, bdn_ref, o_ref):
    del blk_e_ref

    @pl.when(pl.program_id(0) < nused_ref[0])
    def _():
        f = wdn_ref.shape[0]
        gu = jnp.dot(xs_ref[...].astype(BF16), wgu_ref[...], preferred_element_type=F32) + bgu_ref[...]
        gate = jnp.minimum(gu[:, :f], SWIGLU_LIMIT)
        up = jnp.clip(gu[:, f:], -SWIGLU_LIMIT, SWIGLU_LIMIT)
        act = (up + 1.0) * (gate * jax.nn.sigmoid(SWIGLU_ALPHA * gate))
        o_ref[...] = jnp.dot(act.astype(BF16), wdn_ref[...], preferred_element_type=F32) + bdn_ref[...]


def expert_ffn(blk_e, nused, xs, w_gu, b_gu, w_dn, b_dn, te):
    cap, d = xs.shape
    ne, _, f2 = w_gu.shape
    f = f2 // 2
    nblk = cap // te

    def rows(i, be, nu):
        return (jnp.minimum(i, nu[0] - 1), 0)

    def per_expert(i, be, nu):
        return (be[i], 0, 0)

    return pl.pallas_call(
        _expert_kernel,
        grid_spec=pltpu.PrefetchScalarGridSpec(
            num_scalar_prefetch=2,
            grid=(nblk,),
            in_specs=[pl.BlockSpec((te, d), rows),
                      pl.BlockSpec((None, d, f2), per_expert),
                      pl.BlockSpec((None, 1, f2), per_expert),
                      pl.BlockSpec((None, f, d), per_expert),
                      pl.BlockSpec((None, 1, d), per_expert)],
            out_specs=pl.BlockSpec((te, d), rows)),
        out_shape=jax.ShapeDtypeStruct((cap, d), F32),
        compiler_params=_params(("arbitrary",)),
        name="expert_ffn",
    )(blk_e, nused, xs, w_gu, b_gu.reshape(ne, 1, f2), w_dn, b_dn.reshape(ne, 1, d))


def _combine_kernel(ppos_ref, ys_hbm, gate_ref, x_ref, ga_ref, xo_ref, buf_ref, sem, *, tc):
    base = pl.program_id(0) * tc

    def issue(r, _):
        for k in range(TOP_K):
            src = ppos_ref[(base + r) * TOP_K + k]
            pltpu.make_async_copy(ys_hbm.at[pl.ds(src, 1)], buf_ref.at[k, pl.ds(r, 1)], sem).start()
        return 0

    def drain(r, _):
        for k in range(TOP_K):
            pltpu.make_async_copy(ys_hbm.at[pl.ds(0, 1)], buf_ref.at[0, pl.ds(0, 1)], sem).wait()
        return 0

    lax.fori_loop(0, tc, issue, 0)
    lax.fori_loop(0, tc, drain, 0)
    g = gate_ref[...]
    y = g[:, 0:1] * buf_ref[0]
    for k in range(1, TOP_K):
        y = y + g[:, k:k + 1] * buf_ref[k]
    xo_ref[...] = x_ref[...] + ga_ref[...] * y


def combine_rows(ppos, ys, gates, x, mod):
    b, s, d = x.shape
    tc = min(DMA_TILE, s)
    per_b = s // tc
    return pl.pallas_call(
        functools.partial(_combine_kernel, tc=tc),
        grid_spec=pltpu.PrefetchScalarGridSpec(
            num_scalar_prefetch=1,
            grid=(b * per_b,),
            in_specs=[pl.BlockSpec(memory_space=pl.ANY),
                      pl.BlockSpec((tc, LANES), lambda i, pp: (i, 0)),
                      pl.BlockSpec((None, tc, d), lambda i, pp: (i // per_b, i % per_b, 0)),
                      pl.BlockSpec((None, 1, d), lambda i, pp: (i // per_b, 0, 5))],
            out_specs=pl.BlockSpec((None, tc, d), lambda i, pp: (i // per_b, i % per_b, 0)),
            scratch_shapes=[pltpu.VMEM((TOP_K, tc, d), F32), pltpu.SemaphoreType.DMA(())]),
        out_shape=jax.ShapeDtypeStruct((b, s, d), F32),
        compiler_params=_params(("arbitrary",)),
        name="combine_rows",
    )(ppos, ys, gates, x, mod)


def _final_norm_kernel(x_ref, g_ref, o_ref):
    o_ref[...] = _rms_rows(x_ref[...]) * g_ref[...]


def final_norm(x, g):
    b, s, d = x.shape
    tm = min(ROW_TILE, s)
    spec = pl.BlockSpec((None, tm, d), lambda i, j: (i, j, 0))
    return pl.pallas_call(
        _final_norm_kernel,
        grid=(b, s // tm),
        in_specs=[spec, pl.BlockSpec((1, d), lambda i, j: (0, 0))],
        out_specs=spec,
        out_shape=jax.ShapeDtypeStruct((b, s, d), F32),
        compiler_params=_params(("arbitrary", "arbitrary")),
        name="final_norm",
    )(x, g)


def _routing_tables(idx, rank, counts, n_experts, te, nblk):
    counts = counts[0, :n_experts]
    padded = (counts + te - 1) // te * te
    ends = jnp.cumsum(padded)
    starts = ends - padded
    ppos = (starts[idx[:, :TOP_K]] + rank[:, :TOP_K]).reshape(-1).astype(jnp.int32)
    nused = (ends[-1] // te).astype(jnp.int32)
    blk = jnp.arange(nblk, dtype=jnp.int32)
    blk_e = jnp.minimum(jnp.searchsorted(ends, blk * te, side="right"), n_experts - 1).astype(jnp.int32)
    blk_e = jnp.where(blk < nused, blk_e, blk_e[nused - 1])
    return ppos, blk_e, nused.reshape(1)


def _pad_lanes(a, fill=0.0):
    return jnp.pad(a, [(0, 0)] * (a.ndim - 1) + [(0, LANES - a.shape[-1])], constant_values=fill)


def kernel(x, c, positions, w_mod, b_mod, g_attn, w_in, b_forget, g_mix, w_out, g_ffn, w_router, b_router,
           w_gate_up, b_gate_up, w_down, b_down, g_final):
    b, s, d = x.shape
    depth = w_mod.shape[0]
    n_experts = w_router.shape[-1]
    d_mix = w_out.shape[1]
    n = b * s
    te = min(EXPERT_TILE, n)
    nblk = -(-(n * TOP_K + n_experts * (te - 1)) // te)
    cap = nblk * te

    mod_all = modulation(c, w_mod, b_mod)
    cos, sin = rope_tables(positions)
    for layer in range(depth):
        mod = mod_all[layer].reshape(b, 1, N_MOD * d)
        w_qkv = w_in[layer, :, :3 * d_mix].astype(BF16)
        w_f = _pad_lanes(w_in[layer, :, 3 * d_mix:]).astype(BF16)
        b_f = _pad_lanes(b_forget[layer][None, :])
        qkv, logf = in_projection(x, mod, g_attn[layer][None, :], w_qkv, w_f, b_f)
        ccol, crow = forget_cumsum(logf)
        o_sb = sb_attention(qkv)
        o_dil = dil_attention(qkv, cos, sin)
        o_fox = fox_attention(qkv, ccol, crow)
        x, h2, idx, gates, rank, counts = outproj_router(
            o_sb, o_dil, o_fox, g_mix[layer][None, :], w_out[layer].astype(BF16), x, mod, g_ffn[layer][None, :],
            _pad_lanes(w_router[layer]), _pad_lanes(b_router[layer][None, :], NEG_BIG))
        ppos, blk_e, nused = _routing_tables(idx, rank, counts, n_experts, te, nblk)
        xs = dispatch_rows(ppos, h2, cap)
        ys = expert_ffn(blk_e, nused, xs, w_gate_up[layer].astype(BF16), b_gate_up[layer],
                        w_down[layer].astype(BF16), b_down[layer], te)
        x = combine_rows(ppos, ys, gates, x, mod)
    return final_norm(x, g_final[None, :])
```

```python
import functools

import numpy as np
import jax
import jax.numpy as jnp
from jax import lax
from jax.experimental import pallas as pl
from jax.experimental.pallas import tpu as pltpu

F32 = jnp.float32
BF16 = jnp.bfloat16
HIGHEST = lax.Precision.HIGHEST

HEAD_DIM = 64
LANES = 128
N_HEADS_SB = 4
N_HEADS_DIL = 6
N_HEADS_FOX = 6
DIL_PATTERNS = ((128, 1), (512, 4), (2048, 16))
ROPE_THETA = 10000.0
TOP_K = 4
SWIGLU_LIMIT = 7.0
SWIGLU_ALPHA = 1.702
N_MOD = 6
EPS = 1e-6
NEG_BIG = -1e30
ATTN_TILE = 256
ROW_TILE = 512
EXPERT_TILE = 512
DMA_TILE = 256
VMEM_LIMIT = 48 * 1024 * 1024


def _params(sem, vmem=VMEM_LIMIT):
    return pltpu.CompilerParams(dimension_semantics=sem, vmem_limit_bytes=vmem)


def _rms_rows(x):
    return x * lax.rsqrt(jnp.mean(x * x, axis=-1, keepdims=True) + EPS)


def _softplus(z):
    return jnp.maximum(z, 0.0) + jnp.log(1.0 + jnp.exp(-jnp.abs(z)))


def _mod_kernel(c_ref, w_ref, b_ref, o_ref):
    c = c_ref[...]
    ca = c * jax.nn.sigmoid(c)
    o_ref[...] = jnp.dot(ca, w_ref[...], preferred_element_type=F32, precision=HIGHEST) + b_ref[...]


def modulation(c, w_mod, b_mod):
    depth, d, n6 = w_mod.shape
    b = c.shape[0]
    tn = min(n6, 1536)
    return pl.pallas_call(
        _mod_kernel,
        grid=(depth, n6 // tn),
        in_specs=[pl.BlockSpec((b, d), lambda l, j: (0, 0)),
                  pl.BlockSpec((None, d, tn), lambda l, j: (l, 0, j)),
                  pl.BlockSpec((None, 1, tn), lambda l, j: (l, 0, j))],
        out_specs=pl.BlockSpec((None, b, tn), lambda l, j: (l, 0, j)),
        out_shape=jax.ShapeDtypeStruct((depth, b, n6), F32),
        compiler_params=_params(("arbitrary", "arbitrary")),
        name="modulation",
    )(c, w_mod, b_mod.reshape(depth, 1, n6))


def _rope_table_kernel(pos_ref, invf_ref, cos_ref, sin_ref):
    ang = pos_ref[...].astype(F32) * invf_ref[...]
    lane = lax.broadcasted_iota(jnp.int32, ang.shape, 1)
    first_half = (lane % HEAD_DIM) < (HEAD_DIM // 2)
    s = jnp.sin(ang)
    cos_ref[...] = jnp.cos(ang)
    sin_ref[...] = jnp.where(first_half, -s, s)


def rope_tables(positions):
    b, s = positions.shape
    half = HEAD_DIM // 2
    inv_freq = ROPE_THETA ** (-np.arange(0, HEAD_DIM, 2, dtype=np.float64) / HEAD_DIM)
    invf = jnp.asarray(np.tile(inv_freq, LANES // half)[None, :], F32)
    return pl.pallas_call(
        _rope_table_kernel,
        grid=(b,),
        in_specs=[pl.BlockSpec((None, s, 1), lambda i: (i, 0, 0)),
                  pl.BlockSpec((1, LANES), lambda i: (0, 0))],
        out_specs=[pl.BlockSpec((None, s, LANES), lambda i: (i, 0, 0))] * 2,
        out_shape=[jax.ShapeDtypeStruct((b, s, LANES), F32)] * 2,
        compiler_params=_params(("arbitrary",)),
        name="rope_tables",
    )(positions.reshape(b, s, 1), invf)


def _rope(x, cos, sin_signed):
    lane = lax.broadcasted_iota(jnp.int32, x.shape, 1)
    first_half = (lane % HEAD_DIM) < (HEAD_DIM // 2)
    half = HEAD_DIM // 2
    partner = jnp.where(first_half, pltpu.roll(x, LANES - half, 1), pltpu.roll(x, half, 1))
    return x * cos + partner * sin_signed


def _inproj_kernel(x_ref, sh_ref, sc_ref, g_ref, wqkv_ref, wf_ref, bf_ref, qkv_ref, logf_ref, *, d_mix):
    x = x_ref[...]
    h = _rms_rows(x) * g_ref[...] * (1.0 + sc_ref[...]) + sh_ref[...]
    hb = h.astype(BF16)
    for j in range(3):
        y = jnp.dot(hb, wqkv_ref[:, j * d_mix:(j + 1) * d_mix], preferred_element_type=F32)
        if j == 0:
            y = y * (HEAD_DIM ** -0.5)
        qkv_ref[:, j * d_mix:(j + 1) * d_mix] = y.astype(BF16)
    f = jnp.dot(hb, wf_ref[...], preferred_element_type=F32) + bf_ref[...]
    logf_ref[...] = -_softplus(-f)


def in_projection(x, mod, g, w_qkv, w_f, b_f):
    b, s, d = x.shape
    d_mix = w_qkv.shape[1] // 3
    tm = min(ROW_TILE, s)
    return pl.pallas_call(
        functools.partial(_inproj_kernel, d_mix=d_mix),
        grid=(b, s // tm),
        in_specs=[pl.BlockSpec((None, tm, d), lambda i, j: (i, j, 0)),
                  pl.BlockSpec((None, 1, d), lambda i, j: (i, 0, 0)),
                  pl.BlockSpec((None, 1, d), lambda i, j: (i, 0, 1)),
                  pl.BlockSpec((1, d), lambda i, j: (0, 0)),
                  pl.BlockSpec((d, 3 * d_mix), lambda i, j: (0, 0)),
                  pl.BlockSpec((d, LANES), lambda i, j: (0, 0)),
                  pl.BlockSpec((1, LANES), lambda i, j: (0, 0))],
        out_specs=[pl.BlockSpec((None, tm, 3 * d_mix), lambda i, j: (i, j, 0)),
                   pl.BlockSpec((None, tm, LANES), lambda i, j: (i, j, 0))],
        out_shape=[jax.ShapeDtypeStruct((b, s, 3 * d_mix), BF16),
                   jax.ShapeDtypeStruct((b, s, LANES), F32)],
        compiler_params=_params(("arbitrary", "arbitrary")),
        name="in_projection",
    )(x, mod, mod, g, w_qkv, w_f, b_f)


def _cumsum_kernel(logf_ref, col_ref, row_ref):
    s = logf_ref.shape[0]
    r = lax.broadcasted_iota(jnp.int32, (LANES, LANES), 0)
    c = lax.broadcasted_iota(jnp.int32, (LANES, LANES), 1)
    tri = (c <= r).astype(F32)
    carry = jnp.zeros((1, LANES), F32)
    for i in range(s // LANES):
        blk = logf_ref[i * LANES:(i + 1) * LANES, :]
        cs = jnp.dot(tri, blk, preferred_element_type=F32, precision=HIGHEST) + carry
        carry = cs[LANES - 1:LANES, :]
        col_ref[i * LANES:(i + 1) * LANES, :] = cs
        row_ref[i] = cs.T[0:8, :]


def forget_cumsum(logf):
    b, s, _ = logf.shape
    return pl.pallas_call(
        _cumsum_kernel,
        grid=(b,),
        in_specs=[pl.BlockSpec((None, s, LANES), lambda i: (i, 0, 0))],
        out_specs=[pl.BlockSpec((None, s, LANES), lambda i: (i, 0, 0)),
                   pl.BlockSpec((None, s // LANES, 8, LANES), lambda i: (i, 0, 0, 0))],
        out_shape=[jax.ShapeDtypeStruct((b, s, LANES), F32),
                   jax.ShapeDtypeStruct((b, s // LANES, 8, LANES), F32)],
        compiler_params=_params(("arbitrary",)),
        name="forget_cumsum",
    )(logf)


def _tile_iotas(t):
    return lax.broadcasted_iota(jnp.int32, (t, t), 0), lax.broadcasted_iota(jnp.int32, (t, t), 1)


def _head(ref, start, size, hh):
    return ref[pl.ds(start, size), hh * HEAD_DIM:(hh + 1) * HEAD_DIM]


def _qk(q, k):
    return lax.dot_general(q, k, (((1,), (1,)), ((), ())), preferred_element_type=F32)


def _sb_kernel(q_ref, k_ref, v_ref, o_ref, *, t):
    qi = pl.program_id(2)
    row, col = _tile_iotas(t)
    strict = col < row
    upper = (row > col).astype(BF16)
    outs = []
    for hh in range(2):
        q = q_ref[:, hh * HEAD_DIM:(hh + 1) * HEAD_DIM]

        def tile(ki, carry, diag, q=q, hh=hh):
            suffix, acc = carry
            ks = pl.multiple_of(ki * t, t)
            z = _qk(q, _head(k_ref, ks, t, hh))
            sp = _softplus(z)
            log_not = -sp
            if diag:
                log_not = jnp.where(strict, log_not, 0.0)
            hi = log_not.astype(BF16)
            lo = (log_not - hi.astype(F32)).astype(BF16)
            between = (jnp.dot(hi, upper, preferred_element_type=F32)
                       + jnp.dot(lo, upper, preferred_element_type=F32))
            w = jnp.exp(z - sp + between + suffix)
            if diag:
                w = jnp.where(strict, w, 0.0)
            acc = acc + jnp.dot(w.astype(BF16), _head(v_ref, ks, t, hh), preferred_element_type=F32)
            suffix = suffix + between[:, 0:1] + log_not[:, 0:1]
            return suffix, acc

        carry = tile(qi, (jnp.zeros((t, 1), F32), jnp.zeros((t, HEAD_DIM), F32)), True)
        carry = lax.fori_loop(0, qi, lambda j, cr: tile(qi - 1 - j, cr, False), carry)
        outs.append(carry[1])
    o_ref[...] = jnp.concatenate(outs, axis=-1).astype(o_ref.dtype)


def _online_softmax_step(s, v, carry, weight=None):
    m, l, acc = carry
    m_new = jnp.maximum(m, jnp.max(s, axis=-1, keepdims=True))
    alpha = jnp.exp(m - m_new)
    p = jnp.exp(s - m_new)
    if weight is not None:
        p = p * weight
    l = alpha * l + jnp.sum(p, axis=-1, keepdims=True)
    acc = alpha * acc + jnp.dot(p.astype(BF16), v, preferred_element_type=F32)
    return m_new, l, acc


def _softmax_init(t):
    return (jnp.full((t, 1), NEG_BIG, F32), jnp.zeros((t, 1), F32), jnp.zeros((t, HEAD_DIM), F32))


def _fox_kernel(q_ref, k_ref, v_ref, ccol_ref, crow_ref, o_ref, *, t):
    p = pl.program_id(1)
    qi = pl.program_id(2)
    row, col = _tile_iotas(t)
    lane = lax.broadcasted_iota(jnp.int32, (t, LANES), 1)
    nch = t // LANES
    outs = []
    for hh in range(2):
        hidx = 2 * p + hh
        q = q_ref[:, hh * HEAD_DIM:(hh + 1) * HEAD_DIM]
        cq = jnp.sum(jnp.where(lane == hidx, ccol_ref[...], 0.0), axis=-1, keepdims=True)

        def tile(ki, carry, diag, q=q, hh=hh, hidx=hidx, cq=cq):
            ks = pl.multiple_of(ki * t, t)
            ck = jnp.concatenate([crow_ref[ki * nch + c, pl.ds(hidx, 1), :] for c in range(nch)], axis=-1)
            s = _qk(q, _head(k_ref, ks, t, hh)) + (cq - ck)
            if diag:
                s = jnp.where(col <= row, s, NEG_BIG)
            return _online_softmax_step(s, _head(v_ref, ks, t, hh), carry)

        carry = lax.fori_loop(0, qi, lambda ki, cr: tile(ki, cr, False), _softmax_init(t))
        _, l, acc = tile(qi, carry, True)
        outs.append(acc / l)
    o_ref[...] = jnp.concatenate(outs, axis=-1).astype(o_ref.dtype)


def _dilation_count(d):
    cnt = None
    for window, dil in DIL_PATTERNS:
        hit = (d >= 0) & (d <= window) & ((d & (dil - 1)) == 0)
        term = jnp.where(hit, 1.0, 0.0)
        cnt = term if cnt is None else cnt + term
    return cnt


def _dil_kernel(q_ref, k_ref, v_ref, cos_ref, sin_ref, o_ref, kr_ref, *, t):
    qi = pl.program_id(2)

    @pl.when(qi == 0)
    def _():
        kr_ref[...] = _rope(k_ref[...].astype(F32), cos_ref[...], sin_ref[...]).astype(kr_ref.dtype)

    qs = pl.multiple_of(qi * t, t)
    qr = _rope(q_ref[...].astype(F32), cos_ref[pl.ds(qs, t), :], sin_ref[pl.ds(qs, t), :]).astype(BF16)
    row, col = _tile_iotas(t)
    outs = []
    for hh in range(2):
        q = qr[:, hh * HEAD_DIM:(hh + 1) * HEAD_DIM]

        def tile(ki, carry, q=q, hh=hh):
            ks = pl.multiple_of(ki * t, t)
            cnt = _dilation_count((qi - ki) * t + row - col)
            s = jnp.where(cnt > 0.0, _qk(q, _head(kr_ref, ks, t, hh)), NEG_BIG)
            return _online_softmax_step(s, _head(v_ref, ks, t, hh), carry, weight=cnt)

        _, l, acc = lax.fori_loop(0, qi + 1, tile, _softmax_init(t))
        outs.append(acc / l)
    o_ref[...] = jnp.concatenate(outs, axis=-1).astype(o_ref.dtype)


def _attention_call(kernel, qkv, head0, n_heads, extra_inputs, extra_specs, scratch, name):
    b, s, w3 = qkv.shape
    nblk = w3 // 3 // LANES
    pairs = n_heads // 2
    p0 = head0 // 2
    t = min(ATTN_TILE, s)
    qspec = pl.BlockSpec((None, t, LANES), lambda i, p, j: (i, j, p0 + p))
    kspec = pl.BlockSpec((None, s, LANES), lambda i, p, j: (i, 0, nblk + p0 + p))
    vspec = pl.BlockSpec((None, s, LANES), lambda i, p, j: (i, 0, 2 * nblk + p0 + p))
    return pl.pallas_call(
        functools.partial(kernel, t=t),
        grid=(b, pairs, s // t),
        in_specs=[qspec, kspec, vspec] + extra_specs,
        out_specs=pl.BlockSpec((None, t, LANES), lambda i, p, j: (i, j, p)),
        out_shape=jax.ShapeDtypeStruct((b, s, pairs * LANES), BF16),
        scratch_shapes=scratch,
        compiler_params=_params(("arbitrary", "arbitrary", "arbitrary")),
        name=name,
    )(qkv, qkv, qkv, *extra_inputs)


def sb_attention(qkv):
    return _attention_call(_sb_kernel, qkv, 0, N_HEADS_SB, [], [], [], "sb_attention")


def dil_attention(qkv, cos, sin):
    s = qkv.shape[1]
    tab = pl.BlockSpec((None, s, LANES), lambda i, p, j: (i, 0, 0))
    return _attention_call(_dil_kernel, qkv, N_HEADS_SB, N_HEADS_DIL, [cos, sin], [tab, tab],
                           [pltpu.VMEM((s, LANES), BF16)], "dil_attention")


def fox_attention(qkv, ccol, crow):
    s = qkv.shape[1]
    t = min(ATTN_TILE, s)
    specs = [pl.BlockSpec((None, t, LANES), lambda i, p, j: (i, j, 0)),
             pl.BlockSpec((None, s // LANES, 8, LANES), lambda i, p, j: (i, 0, 0, 0))]
    return _attention_call(_fox_kernel, qkv, N_HEADS_SB + N_HEADS_DIL, N_HEADS_FOX, [ccol, crow], specs,
                           [], "fox_attention")


def _outproj_router_kernel(osb_ref, odil_ref, ofox_ref, gmix_ref, wout_ref, x_ref, ga_ref, sc_ref, sh_ref,
                           gffn_ref, wr_ref, br_ref,
                           xo_ref, h2_ref, idx_ref, gate_ref, rank_ref, cnt_ref, carry_ref):
    first = (pl.program_id(0) == 0) & (pl.program_id(1) == 0)

    @pl.when(first)
    def _():
        carry_ref[...] = jnp.zeros_like(carry_ref)

    a = None
    lo = 0
    for o_ref in (osb_ref, odil_ref, ofox_ref):
        w = o_ref.shape[-1]
        on = (_rms_rows(o_ref[...].astype(F32)) * gmix_ref[:, lo:lo + w]).astype(BF16)
        part = jnp.dot(on, wout_ref[lo:lo + w, :], preferred_element_type=F32)
        a = part if a is None else a + part
        lo += w
    xn = x_ref[...] + ga_ref[...] * a
    xo_ref[...] = xn
    h2 = _rms_rows(xn) * gffn_ref[...] * (1.0 + sc_ref[...]) + sh_ref[...]
    h2_ref[...] = h2

    logits = jnp.dot(h2, wr_ref[...], preferred_element_type=F32, precision=HIGHEST) + br_ref[...]
    tm = logits.shape[0]
    lane = lax.broadcasted_iota(jnp.int32, (tm, LANES), 1)
    vals, idxs = [], []
    rest = logits
    for _ in range(TOP_K):
        m = jnp.max(rest, axis=-1, keepdims=True)
        ik = jnp.min(jnp.where(rest == m, lane, LANES), axis=-1, keepdims=True)
        vals.append(m)
        idxs.append(ik)
        rest = jnp.where(lane == ik, -jnp.inf, rest)
    es = [jnp.exp(v - vals[0]) for v in vals]
    den = es[0] + es[1] + es[2] + es[3]

    hot = [(lane == ik) for ik in idxs]
    multi = jnp.where(hot[0] | hot[1] | hot[2] | hot[3], 1.0, 0.0)
    r = lax.broadcasted_iota(jnp.int32, (tm, tm), 0)
    c = lax.broadcasted_iota(jnp.int32, (tm, tm), 1)
    before = (c < r).astype(BF16)
    prior = jnp.dot(before, multi.astype(BF16), preferred_element_type=F32) + carry_ref[...]
    idx_out = jnp.zeros((tm, LANES), jnp.int32)
    gate_out = jnp.zeros((tm, LANES), F32)
    rank_out = jnp.zeros((tm, LANES), F32)
    for k in range(TOP_K):
        rk = jnp.sum(jnp.where(hot[k], prior, 0.0), axis=-1, keepdims=True)
        idx_out = jnp.where(lane == k, idxs[k], idx_out)
        gate_out = jnp.where(lane == k, es[k] / den, gate_out)
        rank_out = jnp.where(lane == k, rk, rank_out)
    idx_ref[...] = idx_out
    gate_ref[...] = gate_out
    rank_ref[...] = rank_out.astype(jnp.int32)
    total = prior[tm - 1:tm, :] + multi[tm - 1:tm, :]
    carry_ref[...] = total
    cnt_ref[...] = total.astype(jnp.int32)


def outproj_router(o_sb, o_dil, o_fox, g_mix, w_out, x, mod, g_ffn, w_r, b_r):
    b, s, d = x.shape
    tm = min(ROW_TILE, s)
    n = b * s

    def act(w):
        return pl.BlockSpec((None, tm, w), lambda i, j: (i, j, 0))

    def modspec(col):
        return pl.BlockSpec((None, 1, d), lambda i, j: (i, 0, col))

    def const(shape):
        return pl.BlockSpec(shape, lambda i, j: (0, 0))

    tok = pl.BlockSpec((tm, LANES), lambda i, j: (i * (s // tm) + j, 0))
    return pl.pallas_call(
        _outproj_router_kernel,
        grid=(b, s // tm),
        in_specs=[act(o_sb.shape[-1]), act(o_dil.shape[-1]), act(o_fox.shape[-1]),
                  const((1, d)), const(w_out.shape), act(d),
                  modspec(2), modspec(4), modspec(3),
                  const((1, d)), const((d, LANES)), const((1, LANES))],
        out_specs=[act(d), pl.BlockSpec((tm, d), lambda i, j: (i * (s // tm) + j, 0)), tok, tok, tok,
                   const((1, LANES))],
        out_shape=[jax.ShapeDtypeStruct((b, s, d), F32), jax.ShapeDtypeStruct((n, d), F32),
                   jax.ShapeDtypeStruct((n, LANES), jnp.int32), jax.ShapeDtypeStruct((n, LANES), F32),
                   jax.ShapeDtypeStruct((n, LANES), jnp.int32), jax.ShapeDtypeStruct((1, LANES), jnp.int32)],
        scratch_shapes=[pltpu.VMEM((1, LANES), F32)],
        compiler_params=_params(("arbitrary", "arbitrary")),
        name="outproj_router",
    )(o_sb, o_dil, o_fox, g_mix, w_out, x, mod, mod, mod, g_ffn, w_r, b_r)


def _row_copy(src_hbm, dst_hbm, sem, src_row, dst_row):
    return pltpu.make_async_copy(src_hbm.at[pl.ds(src_row, 1)], dst_hbm.at[pl.ds(dst_row, 1)], sem)


def _dispatch_kernel(ppos_ref, h_ref, zeros_hbm, xs_hbm, sem, *, tg):
    del zeros_hbm
    base = pl.program_id(0) * tg

    def issue(r, _):
        for k in range(TOP_K):
            _row_copy(h_ref, xs_hbm, sem, r, ppos_ref[(base + r) * TOP_K + k]).start()
        return 0

    def drain(r, _):
        for k in range(TOP_K):
            _row_copy(h_ref, xs_hbm, sem, 0, 0).wait()
        return 0

    lax.fori_loop(0, tg, issue, 0)
    lax.fori_loop(0, tg, drain, 0)


def dispatch_rows(ppos, h, cap):
    n, d = h.shape
    tg = min(DMA_TILE, n)
    zeros = jnp.zeros((cap, d), h.dtype)
    return pl.pallas_call(
        functools.partial(_dispatch_kernel, tg=tg),
        grid_spec=pltpu.PrefetchScalarGridSpec(
            num_scalar_prefetch=1,
            grid=(n // tg,),
            in_specs=[pl.BlockSpec((tg, d), lambda i, pp: (i, 0)), pl.BlockSpec(memory_space=pl.ANY)],
            out_specs=pl.BlockSpec(memory_space=pl.ANY),
            scratch_shapes=[pltpu.SemaphoreType.DMA(())]),
        out_shape=jax.ShapeDtypeStruct((cap, d), h.dtype),
        input_output_aliases={2: 0},
        compiler_params=pltpu.CompilerParams(dimension_semantics=("arbitrary",), has_side_effects=True),
        name="dispatch_rows",
    )(ppos, h, zeros)


def _expert_kernel(blk_e_ref, nused_ref, xs_ref, wgu_ref, bgu_ref, wdn_ref, bdn_ref, o_ref):
    del blk_e_ref

    @pl.when(pl.program_id(0) < nused_ref[0])
    def _():
        f = wdn_ref.shape[0]
        gu = jnp.dot(xs_ref[...].astype(BF16), wgu_ref[...], preferred_element_type=F32) + bgu_ref[...]
        gate = jnp.minimum(gu[:, :f], SWIGLU_LIMIT)
        up = jnp.clip(gu[:, f:], -SWIGLU_LIMIT, SWIGLU_LIMIT)
        act = (up + 1.0) * (gate * jax.nn.sigmoid(SWIGLU_ALPHA * gate))
        o_ref[...] = jnp.dot(act.astype(BF16), wdn_ref[...], preferred_element_type=F32) + bdn_ref[...]


def expert_ffn(blk_e, nused, xs, w_gu, b_gu, w_dn, b_dn, te):
    cap, d = xs.shape
    ne, _, f2 = w_gu.shape
    f = f2 // 2
    nblk = cap // te

    def rows(i, be, nu):
        return (jnp.minimum(i, nu[0] - 1), 0)

    def per_expert(i, be, nu):
        return (be[i], 0, 0)

    return pl.pallas_call(
        _expert_kernel,
        grid_spec=pltpu.PrefetchScalarGridSpec(
            num_scalar_prefetch=2,
            grid=(nblk,),
            in_specs=[pl.BlockSpec((te, d), rows),
                      pl.BlockSpec((None, d, f2), per_expert),
                      pl.BlockSpec((None, 1, f2), per_expert),
                      pl.BlockSpec((None, f, d), per_expert),
                      pl.BlockSpec((None, 1, d), per_expert)],
            out_specs=pl.BlockSpec((te, d), rows)),
        out_shape=jax.ShapeDtypeStruct((cap, d), F32),
        compiler_params=_params(("arbitrary",)),
        name="expert_ffn",
    )(blk_e, nused, xs, w_gu, b_gu.reshape(ne, 1, f2), w_dn, b_dn.reshape(ne, 1, d))


def _combine_kernel(ppos_ref, ys_hbm, gate_ref, x_ref, ga_ref, xo_ref, buf_ref, sem, *, tc):
    base = pl.program_id(0) * tc

    def issue(r, _):
        for k in range(TOP_K):
            src = ppos_ref[(base + r) * TOP_K + k]
            pltpu.make_async_copy(ys_hbm.at[pl.ds(src, 1)], buf_ref.at[k, pl.ds(r, 1)], sem).start()
        return 0

    def drain(r, _):
        for k in range(TOP_K):
            pltpu.make_async_copy(ys_hbm.at[pl.ds(0, 1)], buf_ref.at[0, pl.ds(0, 1)], sem).wait()
        return 0

    lax.fori_loop(0, tc, issue, 0)
    lax.fori_loop(0, tc, drain, 0)
    g = gate_ref[...]
    y = g[:, 0:1] * buf_ref[0]
    for k in range(1, TOP_K):
        y = y + g[:, k:k + 1] * buf_ref[k]
    xo_ref[...] = x_ref[...] + ga_ref[...] * y


def combine_rows(ppos, ys, gates, x, mod):
    b, s, d = x.shape
    tc = min(DMA_TILE, s)
    per_b = s // tc
    return pl.pallas_call(
        functools.partial(_combine_kernel, tc=tc),
        grid_spec=pltpu.PrefetchScalarGridSpec(
            num_scalar_prefetch=1,
            grid=(b * per_b,),
            in_specs=[pl.BlockSpec(memory_space=pl.ANY),
                      pl.BlockSpec((tc, LANES), lambda i, pp: (i, 0)),
                      pl.BlockSpec((None, tc, d), lambda i, pp: (i // per_b, i % per_b, 0)),
                      pl.BlockSpec((None, 1, d), lambda i, pp: (i // per_b, 0, 5))],
            out_specs=pl.BlockSpec((None, tc, d), lambda i, pp: (i // per_b, i % per_b, 0)),
            scratch_shapes=[pltpu.VMEM((TOP_K, tc, d), F32), pltpu.SemaphoreType.DMA(())]),
        out_shape=jax.ShapeDtypeStruct((b, s, d), F32),
        compiler_params=_params(("arbitrary",)),
        name="combine_rows",
    )(ppos, ys, gates, x, mod)


def _final_norm_kernel(x_ref, g_ref, o_ref):
    o_ref[...] = _rms_rows(x_ref[...]) * g_ref[...]


def final_norm(x, g):
    b, s, d = x.shape
    tm = min(ROW_TILE, s)
    spec = pl.BlockSpec((None, tm, d), lambda i, j: (i, j, 0))
    return pl.pallas_call(
        _final_norm_kernel,
        grid=(b, s // tm),
        in_specs=[spec, pl.BlockSpec((1, d), lambda i, j: (0, 0))],
        out_specs=spec,
        out_shape=jax.ShapeDtypeStruct((b, s, d), F32),
        compiler_params=_params(("arbitrary", "arbitrary")),
        name="final_norm",
    )(x, g)


def _routing_tables(idx, rank, counts, n_experts, te, nblk):
    counts = counts[0, :n_experts]
    padded = (counts + te - 1) // te * te
    ends = jnp.cumsum(padded)
    starts = ends - padded
    ppos = (starts[idx[:, :TOP_K]] + rank[:, :TOP_K]).reshape(-1).astype(jnp.int32)
    nused = (ends[-1] // te).astype(jnp.int32)
    blk = jnp.arange(nblk, dtype=jnp.int32)
    blk_e = jnp.minimum(jnp.searchsorted(ends, blk * te, side="right"), n_experts - 1).astype(jnp.int32)
    blk_e = jnp.where(blk < nused, blk_e, blk_e[nused - 1])
    return ppos, blk_e, nused.reshape(1)


def _pad_lanes(a, fill=0.0):
    return jnp.pad(a, [(0, 0)] * (a.ndim - 1) + [(0, LANES - a.shape[-1])], constant_values=fill)


def kernel(x, c, positions, w_mod, b_mod, g_attn, w_in, b_forget, g_mix, w_out, g_ffn, w_router, b_router,
           w_gate_up, b_gate_up, w_down, b_down, g_final):
    b, s, d = x.shape
    depth = w_mod.shape[0]
    n_experts = w_router.shape[-1]
    d_mix = w_out.shape[1]
    n = b * s
    te = min(EXPERT_TILE, n)
    nblk = -(-(n * TOP_K + n_experts * (te - 1)) // te)
    cap = nblk * te

    mod_all = modulation(c, w_mod, b_mod)
    cos, sin = rope_tables(positions)
    for layer in range(depth):
        mod = mod_all[layer].reshape(b, 1, N_MOD * d)
        w_qkv = w_in[layer, :, :3 * d_mix].astype(BF16)
        w_f = _pad_lanes(w_in[layer, :, 3 * d_mix:]).astype(BF16)
        b_f = _pad_lanes(b_forget[layer][None, :])
        qkv, logf = in_projection(x, mod, g_attn[layer][None, :], w_qkv, w_f, b_f)
        ccol, crow = forget_cumsum(logf)
        o_sb = sb_attention(qkv)
        o_dil = dil_attention(qkv, cos, sin)
        o_fox = fox_attention(qkv, ccol, crow)
        x, h2, idx, gates, rank, counts = outproj_router(
            o_sb, o_dil, o_fox, g_mix[layer][None, :], w_out[layer].astype(BF16), x, mod, g_ffn[layer][None, :],
            _pad_lanes(w_router[layer]), _pad_lanes(b_router[layer][None, :], NEG_BIG))
        ppos, blk_e, nused = _routing_tables(idx, rank, counts, n_experts, te, nblk)
        xs = dispatch_rows(ppos, h2, cap)
        ys = expert_ffn(blk_e, nused, xs, w_gate_up[layer].astype(BF16), b_gate_up[layer],
                        w_down[layer].astype(BF16), b_down[layer], te)
        x = combine_rows(ppos, ys, gates, x, mod)
    return final_norm(x, g_final[None, :])
```

```python
import functools

import numpy as np
import jax
import jax.numpy as jnp
from jax import lax
from jax.experimental import pallas as pl
from jax.experimental.pallas import tpu as pltpu

F32 = jnp.float32
BF16 = jnp.bfloat16
HIGHEST = lax.Precision.HIGHEST

HEAD_DIM = 64
LANES = 128
N_HEADS_SB = 4
N_HEADS_DIL = 6
N_HEADS_FOX = 6
DIL_PATTERNS = ((128, 1), (512, 4), (2048, 16))
ROPE_THETA = 10000.0
TOP_K = 4
SWIGLU_LIMIT = 7.0
SWIGLU_ALPHA = 1.702
N_MOD = 6
EPS = 1e-6
NEG_BIG = -1e30
ATTN_TILE = 256
ROW_TILE = 512
EXPERT_TILE = 512
DMA_TILE = 256
VMEM_LIMIT = 48 * 1024 * 1024


def _params(sem, vmem=VMEM_LIMIT):
    return pltpu.CompilerParams(dimension_semantics=sem, vmem_limit_bytes=vmem)


def _rms_rows(x):
    return x * lax.rsqrt(jnp.mean(x * x, axis=-1, keepdims=True) + EPS)


def _softplus(z):
    return jnp.maximum(z, 0.0) + jnp.log(1.0 + jnp.exp(-jnp.abs(z)))


def _mod_kernel(c_ref, w_ref, b_ref, o_ref):
    c = c_ref[...]
    ca = c * jax.nn.sigmoid(c)
    o_ref[...] = jnp.dot(ca, w_ref[...], preferred_element_type=F32, precision=HIGHEST) + b_ref[...]


def modulation(c, w_mod, b_mod):
    depth, d, n6 = w_mod.shape
    b = c.shape[0]
    tn = min(n6, 1536)
    return pl.pallas_call(
        _mod_kernel,
        grid=(depth, n6 // tn),
        in_specs=[pl.BlockSpec((b, d), lambda l, j: (0, 0)),
                  pl.BlockSpec((None, d, tn), lambda l, j: (l, 0, j)),
                  pl.BlockSpec((None, 1, tn), lambda l, j: (l, 0, j))],
        out_specs=pl.BlockSpec((None, b, tn), lambda l, j: (l, 0, j)),
        out_shape=jax.ShapeDtypeStruct((depth, b, n6), F32),
        compiler_params=_params(("arbitrary", "arbitrary")),
        name="modulation",
    )(c, w_mod, b_mod.reshape(depth, 1, n6))


def _rope_table_kernel(pos_ref, invf_ref, cos_ref, sin_ref):
    ang = pos_ref[...].astype(F32) * invf_ref[...]
    lane = lax.broadcasted_iota(jnp.int32, ang.shape, 1)
    first_half = (lane % HEAD_DIM) < (HEAD_DIM // 2)
    s = jnp.sin(ang)
    cos_ref[...] = jnp.cos(ang)
    sin_ref[...] = jnp.where(first_half, -s, s)


def rope_tables(positions):
    b, s = positions.shape
    half = HEAD_DIM // 2
    inv_freq = ROPE_THETA ** (-np.arange(0, HEAD_DIM, 2, dtype=np.float64) / HEAD_DIM)
    invf = jnp.asarray(np.tile(inv_freq, LANES // half)[None, :], F32)
    return pl.pallas_call(
        _rope_table_kernel,
        grid=(b,),
        in_specs=[pl.BlockSpec((None, s, 1), lambda i: (i, 0, 0)),
                  pl.BlockSpec((1, LANES), lambda i: (0, 0))],
        out_specs=[pl.BlockSpec((None, s, LANES), lambda i: (i, 0, 0))] * 2,
        out_shape=[jax.ShapeDtypeStruct((b, s, LANES), F32)] * 2,
        compiler_params=_params(("arbitrary",)),
        name="rope_tables",
    )(positions.reshape(b, s, 1), invf)


def _rope(x, cos, sin_signed):
    lane = lax.broadcasted_iota(jnp.int32, x.shape, 1)
    first_half = (lane % HEAD_DIM) < (HEAD_DIM // 2)
    half = HEAD_DIM // 2
    partner = jnp.where(first_half, pltpu.roll(x, LANES - half, 1), pltpu.roll(x, half, 1))
    return x * cos + partner * sin_signed


def _rope_wide(x_ref, rows, cos, sin_signed):
    tiles = [_rope(x_ref[rows, c * LANES:(c + 1) * LANES].astype(F32), cos, sin_signed).astype(BF16)
             for c in range(x_ref.shape[-1] // LANES)]
    return jnp.concatenate(tiles, axis=-1)


FAMILY_HEADS = (N_HEADS_SB, N_HEADS_DIL, N_HEADS_FOX)


def family_major_qkv_weight(w_in_layer, d_mix):
    wq = w_in_layer[:, :d_mix] * (HEAD_DIM ** -0.5)
    wk = w_in_layer[:, d_mix:2 * d_mix]
    wv = w_in_layer[:, 2 * d_mix:3 * d_mix]
    cols, lo = [], 0
    for nh in FAMILY_HEADS:
        hi = lo + nh * HEAD_DIM
        cols += [wq[:, lo:hi], wk[:, lo:hi], wv[:, lo:hi]]
        lo = hi
    return jnp.concatenate(cols, axis=1).astype(BF16)


def _inproj_kernel(x_ref, sh_ref, sc_ref, g_ref, wqkv_ref, wf_ref, bf_ref, sb_ref, dil_ref, fox_ref, logf_ref):
    x = x_ref[...]
    h = _rms_rows(x) * g_ref[...] * (1.0 + sc_ref[...]) + sh_ref[...]
    hb = h.astype(BF16)
    lo = 0
    for o_ref in (sb_ref, dil_ref, fox_ref):
        w = o_ref.shape[-1]
        o_ref[...] = jnp.dot(hb, wqkv_ref[:, lo:lo + w], preferred_element_type=F32).astype(BF16)
        lo += w
    f = jnp.dot(hb, wf_ref[...], preferred_element_type=F32) + bf_ref[...]
    logf_ref[...] = -_softplus(-f)


def in_projection(x, mod, g, w_qkv, w_f, b_f):
    b, s, d = x.shape
    d_mix = w_qkv.shape[1] // 3
    tm = min(ROW_TILE, s)
    widths = [3 * nh * HEAD_DIM for nh in FAMILY_HEADS]
    return pl.pallas_call(
        _inproj_kernel,
        grid=(b, s // tm),
        in_specs=[pl.BlockSpec((None, tm, d), lambda i, j: (i, j, 0)),
                  pl.BlockSpec((None, 1, d), lambda i, j: (i, 0, 0)),
                  pl.BlockSpec((None, 1, d), lambda i, j: (i, 0, 1)),
                  pl.BlockSpec((1, d), lambda i, j: (0, 0)),
                  pl.BlockSpec((d, 3 * d_mix), lambda i, j: (0, 0)),
                  pl.BlockSpec((d, LANES), lambda i, j: (0, 0)),
                  pl.BlockSpec((1, LANES), lambda i, j: (0, 0))],
        out_specs=[pl.BlockSpec((None, tm, w), lambda i, j: (i, j, 0)) for w in widths + [LANES]],
        out_shape=[jax.ShapeDtypeStruct((b, s, w), BF16) for w in widths]
        + [jax.ShapeDtypeStruct((b, s, LANES), F32)],
        compiler_params=_params(("arbitrary", "arbitrary")),
        name="in_projection",
    )(x, mod, mod, g, w_qkv, w_f, b_f)


def _cumsum_kernel(logf_ref, col_ref, row_ref):
    s = logf_ref.shape[0]
    r = lax.broadcasted_iota(jnp.int32, (LANES, LANES), 0)
    c = lax.broadcasted_iota(jnp.int32, (LANES, LANES), 1)
    tri = (c <= r).astype(F32)
    carry = jnp.zeros((1, LANES), F32)
    for i in range(s // LANES):
        blk = logf_ref[i * LANES:(i + 1) * LANES, :]
        cs = jnp.dot(tri, blk, preferred_element_type=F32, precision=HIGHEST) + carry
        carry = cs[LANES - 1:LANES, :]
        col_ref[i * LANES:(i + 1) * LANES, :] = cs
        row_ref[i] = cs.T[0:8, :]


def forget_cumsum(logf):
    b, s, _ = logf.shape
    return pl.pallas_call(
        _cumsum_kernel,
        grid=(b,),
        in_specs=[pl.BlockSpec((None, s, LANES), lambda i: (i, 0, 0))],
        out_specs=[pl.BlockSpec((None, s, LANES), lambda i: (i, 0, 0)),
                   pl.BlockSpec((None, s // LANES, 8, LANES), lambda i: (i, 0, 0, 0))],
        out_shape=[jax.ShapeDtypeStruct((b, s, LANES), F32),
                   jax.ShapeDtypeStruct((b, s // LANES, 8, LANES), F32)],
        compiler_params=_params(("arbitrary",)),
        name="forget_cumsum",
    )(logf)


def _tile_iotas(t):
    return lax.broadcasted_iota(jnp.int32, (t, t), 0), lax.broadcasted_iota(jnp.int32, (t, t), 1)


def _head(ref, rows, h):
    return ref[rows, h * HEAD_DIM:(h + 1) * HEAD_DIM]


def _qk(q, k):
    return lax.dot_general(q, k, (((1,), (1,)), ((), ())), preferred_element_type=F32)


def _sb_kernel(q_ref, k_ref, v_ref, o_ref, *, t, nh):
    qi = pl.program_id(1)
    row, col = _tile_iotas(t)
    strict = col < row
    upper = (row > col).astype(BF16)

    def tile(ki, carries, diag):
        rows = pl.ds(pl.multiple_of(ki * t, t), t)
        out = []
        for h in range(nh):
            suffix, acc = carries[h]
            z = _qk(_head(q_ref, slice(None), h), _head(k_ref, rows, h))
            sp = _softplus(z)
            log_not = -sp
            if diag:
                log_not = jnp.where(strict, log_not, 0.0)
            hi = log_not.astype(BF16)
            lo = (log_not - hi.astype(F32)).astype(BF16)
            between = (jnp.dot(hi, upper, preferred_element_type=F32)
                       + jnp.dot(lo, upper, preferred_element_type=F32))
            w = jnp.exp(z - sp + between + suffix)
            if diag:
                w = jnp.where(strict, w, 0.0)
            acc = acc + jnp.dot(w.astype(BF16), _head(v_ref, rows, h), preferred_element_type=F32)
            suffix = suffix + between[:, 0:1] + log_not[:, 0:1]
            out.append((suffix, acc))
        return tuple(out)

    init = tuple((jnp.zeros((t, 1), F32), jnp.zeros((t, HEAD_DIM), F32)) for _ in range(nh))
    carries = tile(qi, init, True)
    carries = lax.fori_loop(0, qi, lambda j, cr: tile(qi - 1 - j, cr, False), carries)
    o_ref[...] = jnp.concatenate([cr[1] for cr in carries], axis=-1).astype(o_ref.dtype)


def _online_softmax_step(s, v, carry):
    m, l, acc = carry
    m_new = jnp.maximum(m, jnp.max(s, axis=-1, keepdims=True))
    alpha = jnp.exp(m - m_new)
    p = jnp.exp(s - m_new)
    l = alpha * l + jnp.sum(p, axis=-1, keepdims=True)
    acc = alpha * acc + jnp.dot(p.astype(BF16), v, preferred_element_type=F32)
    return m_new, l, acc


def _softmax_init(t, nh):
    return tuple((jnp.full((t, 1), NEG_BIG, F32), jnp.zeros((t, 1), F32), jnp.zeros((t, HEAD_DIM), F32))
                 for _ in range(nh))


def _softmax_finish(carries, o_ref):
    o_ref[...] = jnp.concatenate([acc / l for _, l, acc in carries], axis=-1).astype(o_ref.dtype)


def _fox_kernel(q_ref, k_ref, v_ref, ccol_ref, crow_ref, o_ref, *, t, nh):
    qi = pl.program_id(1)
    row, col = _tile_iotas(t)
    nch = t // LANES

    def tile(ki, carries, diag):
        rows = pl.ds(pl.multiple_of(ki * t, t), t)
        out = []
        for h in range(nh):
            ck = jnp.concatenate([crow_ref[ki * nch + c, h:h + 1, :] for c in range(nch)], axis=-1)
            s = _qk(_head(q_ref, slice(None), h), _head(k_ref, rows, h)) + (ccol_ref[:, h:h + 1] - ck)
            if diag:
                s = jnp.where(col <= row, s, NEG_BIG)
            out.append(_online_softmax_step(s, _head(v_ref, rows, h), carries[h]))
        return tuple(out)

    carries = lax.fori_loop(0, qi, lambda ki, cr: tile(ki, cr, False), _softmax_init(t, nh))
    _softmax_finish(tile(qi, carries, True), o_ref)


DIL_NEAR_TILES = 3


def _dilation_log_count(d):
    cnt = None
    for window, dil in DIL_PATTERNS:
        hit = (d >= 0) & (d <= window) & ((d & (dil - 1)) == 0)
        term = jnp.where(hit, 1.0, 0.0)
        cnt = term if cnt is None else cnt + term
    return jnp.where(cnt > 0.0, jnp.log(jnp.maximum(cnt, 1.0)), NEG_BIG)


def _dil_kernel(q_ref, k_ref, v_ref, cos_ref, sin_ref, o_ref, kr_ref, bias_ref, *, t, nh):
    qi = pl.program_id(1)

    @pl.when(qi == 0)
    def _():
        kr_ref[...] = _rope_wide(k_ref, slice(None), cos_ref[...], sin_ref[...])
        row, col = _tile_iotas(t)
        for delta in range(DIL_NEAR_TILES + 1):
            bias_ref[delta] = _dilation_log_count(delta * t + row - col)

    qrows = pl.ds(pl.multiple_of(qi * t, t), t)
    qr = _rope_wide(q_ref, slice(None), cos_ref[qrows, :], sin_ref[qrows, :])

    def tile(ki, carries):
        rows = pl.ds(pl.multiple_of(ki * t, t), t)
        bias = bias_ref[jnp.minimum(qi - ki, DIL_NEAR_TILES)]
        out = []
        for h in range(nh):
            s = _qk(qr[:, h * HEAD_DIM:(h + 1) * HEAD_DIM], _head(kr_ref, rows, h)) + bias
            out.append(_online_softmax_step(s, _head(v_ref, rows, h), carries[h]))
        return tuple(out)

    _softmax_finish(lax.fori_loop(0, qi + 1, tile, _softmax_init(t, nh)), o_ref)


def _attention_call(kernel, qkv, n_heads, extra_inputs, extra_specs, scratch, name):
    b, s, w3 = qkv.shape
    w = w3 // 3
    t = min(ATTN_TILE, s)
    return pl.pallas_call(
        functools.partial(kernel, t=t, nh=n_heads),
        grid=(b, s // t),
        in_specs=[pl.BlockSpec((None, t, w), lambda i, j: (i, j, 0)),
                  pl.BlockSpec((None, s, w), lambda i, j: (i, 0, 1)),
                  pl.BlockSpec((None, s, w), lambda i, j: (i, 0, 2))] + extra_specs,
        out_specs=pl.BlockSpec((None, t, w), lambda i, j: (i, j, 0)),
        out_shape=jax.ShapeDtypeStruct((b, s, w), BF16),
        scratch_shapes=scratch,
        compiler_params=_params(("arbitrary", "arbitrary")),
        name=name,
    )(qkv, qkv, qkv, *extra_inputs)


def sb_attention(qkv):
    return _attention_call(_sb_kernel, qkv, N_HEADS_SB, [], [], [], "sb_attention")


def dil_attention(qkv, cos, sin):
    s = qkv.shape[1]
    t = min(ATTN_TILE, s)
    widest_window, widest_dil = DIL_PATTERNS[-1]
    assert s - 1 <= widest_window and t % widest_dil == 0
    assert all(window < (DIL_NEAR_TILES - 1) * t + 1 for window, _ in DIL_PATTERNS[:-1])
    tab = pl.BlockSpec((None, s, LANES), lambda i, j: (i, 0, 0))
    scratch = [pltpu.VMEM((s, qkv.shape[-1] // 3), BF16), pltpu.VMEM((DIL_NEAR_TILES + 1, t, t), F32)]
    return _attention_call(_dil_kernel, qkv, N_HEADS_DIL, [cos, sin], [tab, tab], scratch, "dil_attention")


def fox_attention(qkv, ccol, crow):
    s = qkv.shape[1]
    t = min(ATTN_TILE, s)
    specs = [pl.BlockSpec((None, t, LANES), lambda i, j: (i, j, 0)),
             pl.BlockSpec((None, s // LANES, 8, LANES), lambda i, j: (i, 0, 0, 0))]
    return _attention_call(_fox_kernel, qkv, N_HEADS_FOX, [ccol, crow], specs, [], "fox_attention")


def _outproj_router_kernel(osb_ref, odil_ref, ofox_ref, gmix_ref, wout_ref, x_ref, ga_ref, sc_ref, sh_ref,
                           gffn_ref, wr_ref, br_ref,
                           xo_ref, h2_ref, idx_ref, gate_ref, rank_ref, cnt_ref, carry_ref):
    first = (pl.program_id(0) == 0) & (pl.program_id(1) == 0)

    @pl.when(first)
    def _():
        carry_ref[...] = jnp.zeros_like(carry_ref)

    a = None
    lo = 0
    for o_ref in (osb_ref, odil_ref, ofox_ref):
        w = o_ref.shape[-1]
        on = (_rms_rows(o_ref[...].astype(F32)) * gmix_ref[:, lo:lo + w]).astype(BF16)
        part = jnp.dot(on, wout_ref[lo:lo + w, :], preferred_element_type=F32)
        a = part if a is None else a + part
        lo += w
    xn = x_ref[...] + ga_ref[...] * a
    xo_ref[...] = xn
    h2 = _rms_rows(xn) * gffn_ref[...] * (1.0 + sc_ref[...]) + sh_ref[...]
    h2_ref[...] = h2

    logits = jnp.dot(h2, wr_ref[...], preferred_element_type=F32, precision=HIGHEST) + br_ref[...]
    tm = logits.shape[0]
    lane = lax.broadcasted_iota(jnp.int32, (tm, LANES), 1)
    vals, idxs = [], []
    rest = logits
    for _ in range(TOP_K):
        m = jnp.max(rest, axis=-1, keepdims=True)
        ik = jnp.min(jnp.where(rest == m, lane, LANES), axis=-1, keepdims=True)
        vals.append(m)
        idxs.append(ik)
        rest = jnp.where(lane == ik, -jnp.inf, rest)
    es = [jnp.exp(v - vals[0]) for v in vals]
    den = es[0] + es[1] + es[2] + es[3]

    hot = [(lane == ik) for ik in idxs]
    multi = jnp.where(hot[0] | hot[1] | hot[2] | hot[3], 1.0, 0.0)
    r = lax.broadcasted_iota(jnp.int32, (tm, tm), 0)
    c = lax.broadcasted_iota(jnp.int32, (tm, tm), 1)
    before = (c < r).astype(BF16)
    prior = jnp.dot(before, multi.astype(BF16), preferred_element_type=F32) + carry_ref[...]
    idx_out = jnp.zeros((tm, LANES), jnp.int32)
    gate_out = jnp.zeros((tm, LANES), F32)
    rank_out = jnp.zeros((tm, LANES), F32)
    for k in range(TOP_K):
        rk = jnp.sum(jnp.where(hot[k], prior, 0.0), axis=-1, keepdims=True)
        idx_out = jnp.where(lane == k, idxs[k], idx_out)
        gate_out = jnp.where(lane == k, es[k] / den, gate_out)
        rank_out = jnp.where(lane == k, rk, rank_out)
    idx_ref[...] = idx_out
    gate_ref[...] = gate_out
    rank_ref[...] = rank_out.astype(jnp.int32)
    total = prior[tm - 1:tm, :] + multi[tm - 1:tm, :]
    carry_ref[...] = total
    cnt_ref[...] = total.astype(jnp.int32)


def outproj_router(o_sb, o_dil, o_fox, g_mix, w_out, x, mod, g_ffn, w_r, b_r):
    b, s, d = x.shape
    tm = min(ROW_TILE, s)
    n = b * s

    def act(w):
        return pl.BlockSpec((None, tm, w), lambda i, j: (i, j, 0))

    def modspec(col):
        return pl.BlockSpec((None, 1, d), lambda i, j: (i, 0, col))

    def const(shape):
        return pl.BlockSpec(shape, lambda i, j: (0, 0))

    tok = pl.BlockSpec((tm, LANES), lambda i, j: (i * (s // tm) + j, 0))
    return pl.pallas_call(
        _outproj_router_kernel,
        grid=(b, s // tm),
        in_specs=[act(o_sb.shape[-1]), act(o_dil.shape[-1]), act(o_fox.shape[-1]),
                  const((1, d)), const(w_out.shape), act(d),
                  modspec(2), modspec(4), modspec(3),
                  const((1, d)), const((d, LANES)), const((1, LANES))],
        out_specs=[act(d), pl.BlockSpec((tm, d), lambda i, j: (i * (s // tm) + j, 0)), tok, tok, tok,
                   const((1, LANES))],
        out_shape=[jax.ShapeDtypeStruct((b, s, d), F32), jax.ShapeDtypeStruct((n, d), F32),
                   jax.ShapeDtypeStruct((n, LANES), jnp.int32), jax.ShapeDtypeStruct((n, LANES), F32),
                   jax.ShapeDtypeStruct((n, LANES), jnp.int32), jax.ShapeDtypeStruct((1, LANES), jnp.int32)],
        scratch_shapes=[pltpu.VMEM((1, LANES), F32)],
        compiler_params=_params(("arbitrary", "arbitrary")),
        name="outproj_router",
    )(o_sb, o_dil, o_fox, g_mix, w_out, x, mod, mod, mod, g_ffn, w_r, b_r)


def _row_copy(src_hbm, dst_hbm, sem, src_row, dst_row):
    return pltpu.make_async_copy(src_hbm.at[pl.ds(src_row, 1)], dst_hbm.at[pl.ds(dst_row, 1)], sem)


def _dispatch_kernel(ppos_ref, h_ref, zeros_hbm, xs_hbm, sem, *, tg):
    del zeros_hbm
    base = pl.program_id(0) * tg

    def issue(r, _):
        for k in range(TOP_K):
            _row_copy(h_ref, xs_hbm, sem, r, ppos_ref[(base + r) * TOP_K + k]).start()
        return 0

    def drain(r, _):
        for k in range(TOP_K):
            _row_copy(h_ref, xs_hbm, sem, 0, 0).wait()
        return 0

    lax.fori_loop(0, tg, issue, 0)
    lax.fori_loop(0, tg, drain, 0)


def dispatch_rows(ppos, h, cap):
    n, d = h.shape
    tg = min(DMA_TILE, n)
    zeros = jnp.zeros((cap, d), h.dtype)
    return pl.pallas_call(
        functools.partial(_dispatch_kernel, tg=tg),
        grid_spec=pltpu.PrefetchScalarGridSpec(
            num_scalar_prefetch=1,
            grid=(n // tg,),
            in_specs=[pl.BlockSpec((tg, d), lambda i, pp: (i, 0)), pl.BlockSpec(memory_space=pl.ANY)],
            out_specs=pl.BlockSpec(memory_space=pl.ANY),
            scratch_shapes=[pltpu.SemaphoreType.DMA(())]),
        out_shape=jax.ShapeDtypeStruct((cap, d), h.dtype),
        input_output_aliases={2: 0},
        compiler_params=pltpu.CompilerParams(dimension_semantics=("arbitrary",), has_side_effects=True),
        name="dispatch_rows",
    )(ppos, h, zeros)


def _expert_kernel(blk_e_ref, nused_ref, xs_ref, wgu_ref, bgu_ref, wdn_ref, bdn_ref, o_ref):
    del blk_e_ref

    @pl.when(pl.program_id(0) < nused_ref[0])
    def _():
        f = wdn_ref.shape[0]
        gu = jnp.dot(xs_ref[...].astype(BF16), wgu_ref[...], preferred_element_type=F32) + bgu_ref[...]
        gate = jnp.minimum(gu[:, :f], SWIGLU_LIMIT)
        up = jnp.clip(gu[:, f:], -SWIGLU_LIMIT, SWIGLU_LIMIT)
        act = (up + 1.0) * (gate * jax.nn.sigmoid(SWIGLU_ALPHA * gate))
        o_ref[...] = jnp.dot(act.astype(BF16), wdn_ref[...], preferred_element_type=F32) + bdn_ref[...]


def expert_ffn(blk_e, nused, xs, w_gu, b_gu, w_dn, b_dn, te):
    cap, d = xs.shape
    ne, _, f2 = w_gu.shape
    f = f2 // 2
    nblk = cap // te

    def rows(i, be, nu):
        return (jnp.minimum(i, nu[0] - 1), 0)

    def per_expert(i, be, nu):
        return (be[i], 0, 0)

    return pl.pallas_call(
        _expert_kernel,
        grid_spec=pltpu.PrefetchScalarGridSpec(
            num_scalar_prefetch=2,
            grid=(nblk,),
            in_specs=[pl.BlockSpec((te, d), rows),
                      pl.BlockSpec((None, d, f2), per_expert),
                      pl.BlockSpec((None, 1, f2), per_expert),
                      pl.BlockSpec((None, f, d), per_expert),
                      pl.BlockSpec((None, 1, d), per_expert)],
            out_specs=pl.BlockSpec((te, d), rows)),
        out_shape=jax.ShapeDtypeStruct((cap, d), F32),
        compiler_params=_params(("arbitrary",)),
        name="expert_ffn",
    )(blk_e, nused, xs, w_gu, b_gu.reshape(ne, 1, f2), w_dn, b_dn.reshape(ne, 1, d))


def _combine_kernel(ppos_ref, ys_hbm, gate_ref, x_ref, ga_ref, xo_ref, buf_ref, sem, *, tc):
    base = pl.program_id(0) * tc

    def issue(r, _):
        for k in range(TOP_K):
            src = ppos_ref[(base + r) * TOP_K + k]
            pltpu.make_async_copy(ys_hbm.at[pl.ds(src, 1)], buf_ref.at[k, pl.ds(r, 1)], sem).start()
        return 0

    def drain(r, _):
        for k in range(TOP_K):
            pltpu.make_async_copy(ys_hbm.at[pl.ds(0, 1)], buf_ref.at[0, pl.ds(0, 1)], sem).wait()
        return 0

    lax.fori_loop(0, tc, issue, 0)
    lax.fori_loop(0, tc, drain, 0)
    g = gate_ref[...]
    y = g[:, 0:1] * buf_ref[0]
    for k in range(1, TOP_K):
        y = y + g[:, k:k + 1] * buf_ref[k]
    xo_ref[...] = x_ref[...] + ga_ref[...] * y


def combine_rows(ppos, ys, gates, x, mod):
    b, s, d = x.shape
    tc = min(DMA_TILE, s)
    per_b = s // tc
    return pl.pallas_call(
        functools.partial(_combine_kernel, tc=tc),
        grid_spec=pltpu.PrefetchScalarGridSpec(
            num_scalar_prefetch=1,
            grid=(b * per_b,),
            in_specs=[pl.BlockSpec(memory_space=pl.ANY),
                      pl.BlockSpec((tc, LANES), lambda i, pp: (i, 0)),
                      pl.BlockSpec((None, tc, d), lambda i, pp: (i // per_b, i % per_b, 0)),
                      pl.BlockSpec((None, 1, d), lambda i, pp: (i // per_b, 0, 5))],
            out_specs=pl.BlockSpec((None, tc, d), lambda i, pp: (i // per_b, i % per_b, 0)),
            scratch_shapes=[pltpu.VMEM((TOP_K, tc, d), F32), pltpu.SemaphoreType.DMA(())]),
        out_shape=jax.ShapeDtypeStruct((b, s, d), F32),
        compiler_params=_params(("arbitrary",)),
        name="combine_rows",
    )(ppos, ys, gates, x, mod)


def _final_norm_kernel(x_ref, g_ref, o_ref):
    o_ref[...] = _rms_rows(x_ref[...]) * g_ref[...]


def final_norm(x, g):
    b, s, d = x.shape
    tm = min(ROW_TILE, s)
    spec = pl.BlockSpec((None, tm, d), lambda i, j: (i, j, 0))
    return pl.pallas_call(
        _final_norm_kernel,
        grid=(b, s // tm),
        in_specs=[spec, pl.BlockSpec((1, d), lambda i, j: (0, 0))],
        out_specs=spec,
        out_shape=jax.ShapeDtypeStruct((b, s, d), F32),
        compiler_params=_params(("arbitrary", "arbitrary")),
        name="final_norm",
    )(x, g)


def _routing_tables(idx, rank, counts, n_experts, te, nblk):
    counts = counts[0, :n_experts]
    padded = (counts + te - 1) // te * te
    ends = jnp.cumsum(padded)
    starts = ends - padded
    ppos = (starts[idx[:, :TOP_K]] + rank[:, :TOP_K]).reshape(-1).astype(jnp.int32)
    nused = (ends[-1] // te).astype(jnp.int32)
    blk = jnp.arange(nblk, dtype=jnp.int32)
    blk_e = jnp.minimum(jnp.searchsorted(ends, blk * te, side="right"), n_experts - 1).astype(jnp.int32)
    blk_e = jnp.where(blk < nused, blk_e, blk_e[nused - 1])
    return ppos, blk_e, nused.reshape(1)


def _pad_lanes(a, fill=0.0):
    return jnp.pad(a, [(0, 0)] * (a.ndim - 1) + [(0, LANES - a.shape[-1])], constant_values=fill)


def kernel(x, c, positions, w_mod, b_mod, g_attn, w_in, b_forget, g_mix, w_out, g_ffn, w_router, b_router,
           w_gate_up, b_gate_up, w_down, b_down, g_final):
    b, s, d = x.shape
    depth = w_mod.shape[0]
    n_experts = w_router.shape[-1]
    d_mix = w_out.shape[1]
    n = b * s
    te = min(EXPERT_TILE, n)
    nblk = -(-(n * TOP_K + n_experts * (te - 1)) // te)
    cap = nblk * te

    mod_all = modulation(c, w_mod, b_mod)
    cos, sin = rope_tables(positions)
    for layer in range(depth):
        mod = mod_all[layer].reshape(b, 1, N_MOD * d)
        w_qkv = family_major_qkv_weight(w_in[layer], d_mix)
        w_f = _pad_lanes(w_in[layer, :, 3 * d_mix:]).astype(BF16)
        b_f = _pad_lanes(b_forget[layer][None, :])
        qkv_sb, qkv_dil, qkv_fox, logf = in_projection(x, mod, g_attn[layer][None, :], w_qkv, w_f, b_f)
        ccol, crow = forget_cumsum(logf)
        o_sb = sb_attention(qkv_sb)
        o_dil = dil_attention(qkv_dil, cos, sin)
        o_fox = fox_attention(qkv_fox, ccol, crow)
        x, h2, idx, gates, rank, counts = outproj_router(
            o_sb, o_dil, o_fox, g_mix[layer][None, :], w_out[layer].astype(BF16), x, mod, g_ffn[layer][None, :],
            _pad_lanes(w_router[layer]), _pad_lanes(b_router[layer][None, :], NEG_BIG))
        ppos, blk_e, nused = _routing_tables(idx, rank, counts, n_experts, te, nblk)
        xs = dispatch_rows(ppos, h2, cap)
        ys = expert_ffn(blk_e, nused, xs, w_gate_up[layer].astype(BF16), b_gate_up[layer],
                        w_down[layer].astype(BF16), b_down[layer], te)
        x = combine_rows(ppos, ys, gates, x, mod)
    return final_norm(x, g_final[None, :])
```

```python
import functools

import numpy as np
import jax
import jax.numpy as jnp
from jax import lax
from jax.experimental import pallas as pl
from jax.experimental.pallas import tpu as pltpu

F32 = jnp.float32
BF16 = jnp.bfloat16
HIGHEST = lax.Precision.HIGHEST

HEAD_DIM = 64
LANES = 128
N_HEADS_SB = 4
N_HEADS_DIL = 6
N_HEADS_FOX = 6
DIL_PATTERNS = ((128, 1), (512, 4), (2048, 16))
ROPE_THETA = 10000.0
TOP_K = 4
SWIGLU_LIMIT = 7.0
SWIGLU_ALPHA = 1.702
N_MOD = 6
EPS = 1e-6
NEG_BIG = -1e30
ATTN_TILE = 256
ROW_TILE = 512
EXPERT_TILE = 512
DMA_TILE = 256
VMEM_LIMIT = 48 * 1024 * 1024


def _params(sem, vmem=VMEM_LIMIT):
    return pltpu.CompilerParams(dimension_semantics=sem, vmem_limit_bytes=vmem)


def _rms_rows(x):
    return x * lax.rsqrt(jnp.mean(x * x, axis=-1, keepdims=True) + EPS)


def _softplus(z):
    return jnp.maximum(z, 0.0) + jnp.log(1.0 + jnp.exp(-jnp.abs(z)))


def _mod_kernel(c_ref, w_ref, b_ref, o_ref):
    c = c_ref[...]
    ca = c * jax.nn.sigmoid(c)
    o_ref[...] = jnp.dot(ca, w_ref[...], preferred_element_type=F32, precision=HIGHEST) + b_ref[...]


def modulation(c, w_mod, b_mod):
    depth, d, n6 = w_mod.shape
    b = c.shape[0]
    tn = min(n6, 1536)
    return pl.pallas_call(
        _mod_kernel,
        grid=(depth, n6 // tn),
        in_specs=[pl.BlockSpec((b, d), lambda l, j: (0, 0)),
                  pl.BlockSpec((None, d, tn), lambda l, j: (l, 0, j)),
                  pl.BlockSpec((None, 1, tn), lambda l, j: (l, 0, j))],
        out_specs=pl.BlockSpec((None, b, tn), lambda l, j: (l, 0, j)),
        out_shape=jax.ShapeDtypeStruct((depth, b, n6), F32),
        compiler_params=_params(("arbitrary", "arbitrary")),
        name="modulation",
    )(c, w_mod, b_mod.reshape(depth, 1, n6))


def _rope_table_kernel(pos_ref, invf_ref, cos_ref, sin_ref):
    ang = pos_ref[...].astype(F32) * invf_ref[...]
    lane = lax.broadcasted_iota(jnp.int32, ang.shape, 1)
    first_half = (lane % HEAD_DIM) < (HEAD_DIM // 2)
    s = jnp.sin(ang)
    cos_ref[...] = jnp.cos(ang)
    sin_ref[...] = jnp.where(first_half, -s, s)


def rope_tables(positions):
    b, s = positions.shape
    half = HEAD_DIM // 2
    inv_freq = ROPE_THETA ** (-np.arange(0, HEAD_DIM, 2, dtype=np.float64) / HEAD_DIM)
    invf = jnp.asarray(np.tile(inv_freq, LANES // half)[None, :], F32)
    return pl.pallas_call(
        _rope_table_kernel,
        grid=(b,),
        in_specs=[pl.BlockSpec((None, s, 1), lambda i: (i, 0, 0)),
                  pl.BlockSpec((1, LANES), lambda i: (0, 0))],
        out_specs=[pl.BlockSpec((None, s, LANES), lambda i: (i, 0, 0))] * 2,
        out_shape=[jax.ShapeDtypeStruct((b, s, LANES), F32)] * 2,
        compiler_params=_params(("arbitrary",)),
        name="rope_tables",
    )(positions.reshape(b, s, 1), invf)


def _rope(x, cos, sin_signed):
    lane = lax.broadcasted_iota(jnp.int32, x.shape, 1)
    first_half = (lane % HEAD_DIM) < (HEAD_DIM // 2)
    half = HEAD_DIM // 2
    partner = jnp.where(first_half, pltpu.roll(x, LANES - half, 1), pltpu.roll(x, half, 1))
    return x * cos + partner * sin_signed


def _rope_wide(x_ref, rows, cos, sin_signed):
    tiles = [_rope(x_ref[rows, c * LANES:(c + 1) * LANES].astype(F32), cos, sin_signed).astype(BF16)
             for c in range(x_ref.shape[-1] // LANES)]
    return jnp.concatenate(tiles, axis=-1)


FAMILY_HEADS = (N_HEADS_SB, N_HEADS_DIL, N_HEADS_FOX)


def family_major_qkv_weight(w_in_layer, d_mix):
    wq = w_in_layer[:, :d_mix] * (HEAD_DIM ** -0.5)
    wk = w_in_layer[:, d_mix:2 * d_mix]
    wv = w_in_layer[:, 2 * d_mix:3 * d_mix]
    cols, lo = [], 0
    for nh in FAMILY_HEADS:
        hi = lo + nh * HEAD_DIM
        cols += [wq[:, lo:hi], wk[:, lo:hi], wv[:, lo:hi]]
        lo = hi
    return jnp.concatenate(cols, axis=1).astype(BF16)


def _inproj_kernel(x_ref, sh_ref, sc_ref, g_ref, wqkv_ref, wf_ref, bf_ref, sb_ref, dil_ref, fox_ref, logf_ref):
    x = x_ref[...]
    h = _rms_rows(x) * g_ref[...] * (1.0 + sc_ref[...]) + sh_ref[...]
    hb = h.astype(BF16)
    lo = 0
    for o_ref in (sb_ref, dil_ref, fox_ref):
        w = o_ref.shape[-1]
        o_ref[...] = jnp.dot(hb, wqkv_ref[:, lo:lo + w], preferred_element_type=F32).astype(BF16)
        lo += w
    f = jnp.dot(hb, wf_ref[...], preferred_element_type=F32) + bf_ref[...]
    logf_ref[...] = -_softplus(-f)


def in_projection(x, mod, g, w_qkv, w_f, b_f):
    b, s, d = x.shape
    d_mix = w_qkv.shape[1] // 3
    tm = min(ROW_TILE, s)
    widths = [3 * nh * HEAD_DIM for nh in FAMILY_HEADS]
    return pl.pallas_call(
        _inproj_kernel,
        grid=(b, s // tm),
        in_specs=[pl.BlockSpec((None, tm, d), lambda i, j: (i, j, 0)),
                  pl.BlockSpec((None, 1, d), lambda i, j: (i, 0, 0)),
                  pl.BlockSpec((None, 1, d), lambda i, j: (i, 0, 1)),
                  pl.BlockSpec((1, d), lambda i, j: (0, 0)),
                  pl.BlockSpec((d, 3 * d_mix), lambda i, j: (0, 0)),
                  pl.BlockSpec((d, LANES), lambda i, j: (0, 0)),
                  pl.BlockSpec((1, LANES), lambda i, j: (0, 0))],
        out_specs=[pl.BlockSpec((None, tm, w), lambda i, j: (i, j, 0)) for w in widths + [LANES]],
        out_shape=[jax.ShapeDtypeStruct((b, s, w), BF16) for w in widths]
        + [jax.ShapeDtypeStruct((b, s, LANES), F32)],
        compiler_params=_params(("arbitrary", "arbitrary")),
        name="in_projection",
    )(x, mod, mod, g, w_qkv, w_f, b_f)


def _cumsum_kernel(logf_ref, col_ref, row_ref):
    s = logf_ref.shape[0]
    r = lax.broadcasted_iota(jnp.int32, (LANES, LANES), 0)
    c = lax.broadcasted_iota(jnp.int32, (LANES, LANES), 1)
    tri = (c <= r).astype(F32)
    carry = jnp.zeros((1, LANES), F32)
    for i in range(s // LANES):
        blk = logf_ref[i * LANES:(i + 1) * LANES, :]
        cs = jnp.dot(tri, blk, preferred_element_type=F32, precision=HIGHEST) + carry
        carry = cs[LANES - 1:LANES, :]
        col_ref[i * LANES:(i + 1) * LANES, :] = cs
        row_ref[i] = cs.T[0:8, :]


def forget_cumsum(logf):
    b, s, _ = logf.shape
    return pl.pallas_call(
        _cumsum_kernel,
        grid=(b,),
        in_specs=[pl.BlockSpec((None, s, LANES), lambda i: (i, 0, 0))],
        out_specs=[pl.BlockSpec((None, s, LANES), lambda i: (i, 0, 0)),
                   pl.BlockSpec((None, s // LANES, 8, LANES), lambda i: (i, 0, 0, 0))],
        out_shape=[jax.ShapeDtypeStruct((b, s, LANES), F32),
                   jax.ShapeDtypeStruct((b, s // LANES, 8, LANES), F32)],
        compiler_params=_params(("arbitrary",)),
        name="forget_cumsum",
    )(logf)


def _tile_iotas(t):
    return lax.broadcasted_iota(jnp.int32, (t, t), 0), lax.broadcasted_iota(jnp.int32, (t, t), 1)


def _head(ref, rows, h):
    return ref[rows, h * HEAD_DIM:(h + 1) * HEAD_DIM]


def _qk(q, k):
    return lax.dot_general(q, k, (((1,), (1,)), ((), ())), preferred_element_type=F32)


def _sb_kernel(q_ref, k_ref, v_ref, o_ref, *, t, nh):
    qi = pl.program_id(1)
    row, col = _tile_iotas(t)
    strict = col < row
    upper = (row > col).astype(BF16)

    def tile(ki, carries, diag):
        rows = pl.ds(pl.multiple_of(ki * t, t), t)
        out = []
        for h in range(nh):
            suffix, acc = carries[h]
            z = _qk(_head(q_ref, slice(None), h), _head(k_ref, rows, h))
            sp = _softplus(z)
            log_not = -sp
            if diag:
                log_not = jnp.where(strict, log_not, 0.0)
            hi = log_not.astype(BF16)
            lo = (log_not - hi.astype(F32)).astype(BF16)
            between = (jnp.dot(hi, upper, preferred_element_type=F32)
                       + jnp.dot(lo, upper, preferred_element_type=F32))
            w = jnp.exp(z - sp + between + suffix)
            if diag:
                w = jnp.where(strict, w, 0.0)
            acc = acc + jnp.dot(w.astype(BF16), _head(v_ref, rows, h), preferred_element_type=F32)
            suffix = suffix + between[:, 0:1] + log_not[:, 0:1]
            out.append((suffix, acc))
        return tuple(out)

    init = tuple((jnp.zeros((t, 1), F32), jnp.zeros((t, HEAD_DIM), F32)) for _ in range(nh))
    carries = tile(qi, init, True)
    carries = lax.fori_loop(0, qi, lambda j, cr: tile(qi - 1 - j, cr, False), carries)
    o_ref[...] = jnp.concatenate([cr[1] for cr in carries], axis=-1).astype(o_ref.dtype)


def _online_softmax_step(s, v, carry):
    m, l, acc = carry
    m_new = jnp.maximum(m, jnp.max(s, axis=-1, keepdims=True))
    alpha = jnp.exp(m - m_new)
    p = jnp.exp(s - m_new)
    l = alpha * l + jnp.sum(p, axis=-1, keepdims=True)
    acc = alpha * acc + jnp.dot(p.astype(BF16), v, preferred_element_type=F32)
    return m_new, l, acc


def _softmax_init(t, nh):
    return tuple((jnp.full((t, 1), NEG_BIG, F32), jnp.zeros((t, 1), F32), jnp.zeros((t, HEAD_DIM), F32))
                 for _ in range(nh))


def _softmax_finish(carries, o_ref):
    o_ref[...] = jnp.concatenate([acc / l for _, l, acc in carries], axis=-1).astype(o_ref.dtype)


def _fox_kernel(q_ref, k_ref, v_ref, ccol_ref, crow_ref, o_ref, *, t, nh):
    qi = pl.program_id(1)
    row, col = _tile_iotas(t)
    nch = t // LANES

    def tile(ki, carries, diag):
        rows = pl.ds(pl.multiple_of(ki * t, t), t)
        out = []
        for h in range(nh):
            ck = jnp.concatenate([crow_ref[ki * nch + c, h:h + 1, :] for c in range(nch)], axis=-1)
            s = _qk(_head(q_ref, slice(None), h), _head(k_ref, rows, h)) + (ccol_ref[:, h:h + 1] - ck)
            if diag:
                s = jnp.where(col <= row, s, NEG_BIG)
            out.append(_online_softmax_step(s, _head(v_ref, rows, h), carries[h]))
        return tuple(out)

    carries = lax.fori_loop(0, qi, lambda ki, cr: tile(ki, cr, False), _softmax_init(t, nh))
    _softmax_finish(tile(qi, carries, True), o_ref)


DIL_NEAR_TILES = 3


def _dilation_log_count(d):
    cnt = None
    for window, dil in DIL_PATTERNS:
        hit = (d >= 0) & (d <= window) & ((d & (dil - 1)) == 0)
        term = jnp.where(hit, 1.0, 0.0)
        cnt = term if cnt is None else cnt + term
    return jnp.where(cnt > 0.0, jnp.log(jnp.maximum(cnt, 1.0)), NEG_BIG)


def _dil_kernel(q_ref, k_ref, v_ref, cos_ref, sin_ref, o_ref, kr_ref, bias_ref, *, t, nh):
    qi = pl.program_id(1)

    @pl.when(qi == 0)
    def _():
        kr_ref[...] = _rope_wide(k_ref, slice(None), cos_ref[...], sin_ref[...])
        row, col = _tile_iotas(t)
        for delta in range(DIL_NEAR_TILES + 1):
            bias_ref[delta] = _dilation_log_count(delta * t + row - col)

    qrows = pl.ds(pl.multiple_of(qi * t, t), t)
    qr = _rope_wide(q_ref, slice(None), cos_ref[qrows, :], sin_ref[qrows, :])

    def tile(ki, carries):
        rows = pl.ds(pl.multiple_of(ki * t, t), t)
        bias = bias_ref[jnp.minimum(qi - ki, DIL_NEAR_TILES)]
        out = []
        for h in range(nh):
            s = _qk(qr[:, h * HEAD_DIM:(h + 1) * HEAD_DIM], _head(kr_ref, rows, h)) + bias
            out.append(_online_softmax_step(s, _head(v_ref, rows, h), carries[h]))
        return tuple(out)

    _softmax_finish(lax.fori_loop(0, qi + 1, tile, _softmax_init(t, nh)), o_ref)


def _attention_call(kernel, qkv, n_heads, extra_inputs, extra_specs, scratch, name):
    b, s, w3 = qkv.shape
    w = w3 // 3
    t = min(ATTN_TILE, s)
    return pl.pallas_call(
        functools.partial(kernel, t=t, nh=n_heads),
        grid=(b, s // t),
        in_specs=[pl.BlockSpec((None, t, w), lambda i, j: (i, j, 0)),
                  pl.BlockSpec((None, s, w), lambda i, j: (i, 0, 1)),
                  pl.BlockSpec((None, s, w), lambda i, j: (i, 0, 2))] + extra_specs,
        out_specs=pl.BlockSpec((None, t, w), lambda i, j: (i, j, 0)),
        out_shape=jax.ShapeDtypeStruct((b, s, w), BF16),
        scratch_shapes=scratch,
        compiler_params=_params(("arbitrary", "arbitrary")),
        name=name,
    )(qkv, qkv, qkv, *extra_inputs)


def sb_attention(qkv):
    return _attention_call(_sb_kernel, qkv, N_HEADS_SB, [], [], [], "sb_attention")


def dil_attention(qkv, cos, sin):
    s = qkv.shape[1]
    t = min(ATTN_TILE, s)
    widest_window, widest_dil = DIL_PATTERNS[-1]
    assert s - 1 <= widest_window and t % widest_dil == 0
    assert all(window < (DIL_NEAR_TILES - 1) * t + 1 for window, _ in DIL_PATTERNS[:-1])
    tab = pl.BlockSpec((None, s, LANES), lambda i, j: (i, 0, 0))
    scratch = [pltpu.VMEM((s, qkv.shape[-1] // 3), BF16), pltpu.VMEM((DIL_NEAR_TILES + 1, t, t), F32)]
    return _attention_call(_dil_kernel, qkv, N_HEADS_DIL, [cos, sin], [tab, tab], scratch, "dil_attention")


def fox_attention(qkv, ccol, crow):
    s = qkv.shape[1]
    t = min(ATTN_TILE, s)
    specs = [pl.BlockSpec((None, t, LANES), lambda i, j: (i, j, 0)),
             pl.BlockSpec((None, s // LANES, 8, LANES), lambda i, j: (i, 0, 0, 0))]
    return _attention_call(_fox_kernel, qkv, N_HEADS_FOX, [ccol, crow], specs, [], "fox_attention")


def _outproj_router_kernel(osb_ref, odil_ref, ofox_ref, gmix_ref, wout_ref, x_ref, ga_ref, sc_ref, sh_ref,
                           gffn_ref, wr_ref, br_ref,
                           xo_ref, h2_ref, idx_ref, gate_ref, rank_ref, cnt_ref, carry_ref):
    first = (pl.program_id(0) == 0) & (pl.program_id(1) == 0)

    @pl.when(first)
    def _():
        carry_ref[...] = jnp.zeros_like(carry_ref)

    a = None
    lo = 0
    for o_ref in (osb_ref, odil_ref, ofox_ref):
        w = o_ref.shape[-1]
        on = (_rms_rows(o_ref[...].astype(F32)) * gmix_ref[:, lo:lo + w]).astype(BF16)
        part = jnp.dot(on, wout_ref[lo:lo + w, :], preferred_element_type=F32)
        a = part if a is None else a + part
        lo += w
    xn = x_ref[...] + ga_ref[...] * a
    xo_ref[...] = xn
    h2 = _rms_rows(xn) * gffn_ref[...] * (1.0 + sc_ref[...]) + sh_ref[...]
    h2_ref[...] = h2

    logits = jnp.dot(h2, wr_ref[...], preferred_element_type=F32, precision=HIGHEST) + br_ref[...]
    tm = logits.shape[0]
    lane = lax.broadcasted_iota(jnp.int32, (tm, LANES), 1)
    vals, idxs = [], []
    rest = logits
    for _ in range(TOP_K):
        m = jnp.max(rest, axis=-1, keepdims=True)
        ik = jnp.min(jnp.where(rest == m, lane, LANES), axis=-1, keepdims=True)
        vals.append(m)
        idxs.append(ik)
        rest = jnp.where(lane == ik, -jnp.inf, rest)
    es = [jnp.exp(v - vals[0]) for v in vals]
    den = es[0] + es[1] + es[2] + es[3]

    hot = [(lane == ik) for ik in idxs]
    multi = jnp.where(hot[0] | hot[1] | hot[2] | hot[3], 1.0, 0.0)
    r = lax.broadcasted_iota(jnp.int32, (tm, tm), 0)
    c = lax.broadcasted_iota(jnp.int32, (tm, tm), 1)
    before = (c < r).astype(BF16)
    prior = jnp.dot(before, multi.astype(BF16), preferred_element_type=F32) + carry_ref[...]
    idx_out = jnp.zeros((tm, LANES), jnp.int32)
    gate_out = jnp.zeros((tm, LANES), F32)
    rank_out = jnp.zeros((tm, LANES), F32)
    for k in range(TOP_K):
        rk = jnp.sum(jnp.where(hot[k], prior, 0.0), axis=-1, keepdims=True)
        idx_out = jnp.where(lane == k, idxs[k], idx_out)
        gate_out = jnp.where(lane == k, es[k] / den, gate_out)
        rank_out = jnp.where(lane == k, rk, rank_out)
    idx_ref[...] = idx_out
    gate_ref[...] = gate_out
    rank_ref[...] = rank_out.astype(jnp.int32)
    total = prior[tm - 1:tm, :] + multi[tm - 1:tm, :]
    carry_ref[...] = total
    cnt_ref[...] = total.astype(jnp.int32)


def outproj_router(o_sb, o_dil, o_fox, g_mix, w_out, x, mod, g_ffn, w_r, b_r):
    b, s, d = x.shape
    tm = min(ROW_TILE, s)
    n = b * s

    def act(w):
        return pl.BlockSpec((None, tm, w), lambda i, j: (i, j, 0))

    def modspec(col):
        return pl.BlockSpec((None, 1, d), lambda i, j: (i, 0, col))

    def const(shape):
        return pl.BlockSpec(shape, lambda i, j: (0, 0))

    tok = pl.BlockSpec((tm, LANES), lambda i, j: (i * (s // tm) + j, 0))
    return pl.pallas_call(
        _outproj_router_kernel,
        grid=(b, s // tm),
        in_specs=[act(o_sb.shape[-1]), act(o_dil.shape[-1]), act(o_fox.shape[-1]),
                  const((1, d)), const(w_out.shape), act(d),
                  modspec(2), modspec(4), modspec(3),
                  const((1, d)), const((d, LANES)), const((1, LANES))],
        out_specs=[act(d), pl.BlockSpec((tm, d), lambda i, j: (i * (s // tm) + j, 0)), tok, tok, tok,
                   const((1, LANES))],
        out_shape=[jax.ShapeDtypeStruct((b, s, d), F32), jax.ShapeDtypeStruct((n, d), F32),
                   jax.ShapeDtypeStruct((n, LANES), jnp.int32), jax.ShapeDtypeStruct((n, LANES), F32),
                   jax.ShapeDtypeStruct((n, LANES), jnp.int32), jax.ShapeDtypeStruct((1, LANES), jnp.int32)],
        scratch_shapes=[pltpu.VMEM((1, LANES), F32)],
        compiler_params=_params(("arbitrary", "arbitrary")),
        name="outproj_router",
    )(o_sb, o_dil, o_fox, g_mix, w_out, x, mod, mod, mod, g_ffn, w_r, b_r)


def _row_copy(src_hbm, dst_hbm, sem, src_row, dst_row):
    return pltpu.make_async_copy(src_hbm.at[pl.ds(src_row, 1)], dst_hbm.at[pl.ds(dst_row, 1)], sem)


def _dispatch_kernel(ppos_ref, h_ref, zeros_hbm, xs_hbm, sem, *, tg):
    del zeros_hbm
    base = pl.program_id(0) * tg

    def issue(r, _):
        for k in range(TOP_K):
            _row_copy(h_ref, xs_hbm, sem, r, ppos_ref[(base + r) * TOP_K + k]).start(priority=k % 2)
        return 0

    def drain(r, _):
        for k in range(TOP_K):
            _row_copy(h_ref, xs_hbm, sem, 0, 0).wait()
        return 0

    lax.fori_loop(0, tg, issue, 0)
    lax.fori_loop(0, tg, drain, 0)


def dispatch_rows(ppos, h, cap):
    n, d = h.shape
    tg = min(DMA_TILE, n)
    zeros = jnp.zeros((cap, d), h.dtype)
    return pl.pallas_call(
        functools.partial(_dispatch_kernel, tg=tg),
        grid_spec=pltpu.PrefetchScalarGridSpec(
            num_scalar_prefetch=1,
            grid=(n // tg,),
            in_specs=[pl.BlockSpec((tg, d), lambda i, pp: (i, 0)), pl.BlockSpec(memory_space=pl.ANY)],
            out_specs=pl.BlockSpec(memory_space=pl.ANY),
            scratch_shapes=[pltpu.SemaphoreType.DMA(())]),
        out_shape=jax.ShapeDtypeStruct((cap, d), h.dtype),
        input_output_aliases={2: 0},
        compiler_params=pltpu.CompilerParams(dimension_semantics=("arbitrary",), has_side_effects=True),
        name="dispatch_rows",
    )(ppos, h, zeros)


def _expert_kernel(blk_e_ref, nused_ref, xs_ref, wgu_ref, bgu_ref, wdn_ref, bdn_ref, o_ref):
    del blk_e_ref

    @pl.when(pl.program_id(0) < nused_ref[0])
    def _():
        f = wdn_ref.shape[0]
        gu = jnp.dot(xs_ref[...].astype(BF16), wgu_ref[...], preferred_element_type=F32) + bgu_ref[...]
        gate = jnp.minimum(gu[:, :f], SWIGLU_LIMIT)
        up = jnp.clip(gu[:, f:], -SWIGLU_LIMIT, SWIGLU_LIMIT)
        act = (up + 1.0) * (gate * jax.nn.sigmoid(SWIGLU_ALPHA * gate))
        o_ref[...] = jnp.dot(act.astype(BF16), wdn_ref[...], preferred_element_type=F32) + bdn_ref[...]


def expert_ffn(blk_e, nused, xs, w_gu, b_gu, w_dn, b_dn, te):
    cap, d = xs.shape
    ne, _, f2 = w_gu.shape
    f = f2 // 2
    nblk = cap // te

    def rows(i, be, nu):
        return (jnp.minimum(i, nu[0] - 1), 0)

    def per_expert(i, be, nu):
        return (be[i], 0, 0)

    return pl.pallas_call(
        _expert_kernel,
        grid_spec=pltpu.PrefetchScalarGridSpec(
            num_scalar_prefetch=2,
            grid=(nblk,),
            in_specs=[pl.BlockSpec((te, d), rows),
                      pl.BlockSpec((None, d, f2), per_expert),
                      pl.BlockSpec((None, 1, f2), per_expert),
                      pl.BlockSpec((None, f, d), per_expert),
                      pl.BlockSpec((None, 1, d), per_expert)],
            out_specs=pl.BlockSpec((te, d), rows)),
        out_shape=jax.ShapeDtypeStruct((cap, d), F32),
        compiler_params=_params(("arbitrary",)),
        name="expert_ffn",
    )(blk_e, nused, xs, w_gu, b_gu.reshape(ne, 1, f2), w_dn, b_dn.reshape(ne, 1, d))


def _combine_kernel(ppos_ref, ys_hbm, gate_ref, x_ref, ga_ref, xo_ref, buf_ref, sem, *, tc):
    base = pl.program_id(0) * tc

    def issue(r, _):
        for k in range(TOP_K):
            src = ppos_ref[(base + r) * TOP_K + k]
            pltpu.make_async_copy(ys_hbm.at[pl.ds(src, 1)], buf_ref.at[k, pl.ds(r, 1)], sem).start(priority=k % 2)
        return 0

    def drain(r, _):
        for k in range(TOP_K):
            pltpu.make_async_copy(ys_hbm.at[pl.ds(0, 1)], buf_ref.at[0, pl.ds(0, 1)], sem).wait()
        return 0

    lax.fori_loop(0, tc, issue, 0)
    lax.fori_loop(0, tc, drain, 0)
    g = gate_ref[...]
    y = g[:, 0:1] * buf_ref[0]
    for k in range(1, TOP_K):
        y = y + g[:, k:k + 1] * buf_ref[k]
    xo_ref[...] = x_ref[...] + ga_ref[...] * y


def combine_rows(ppos, ys, gates, x, mod):
    b, s, d = x.shape
    tc = min(DMA_TILE, s)
    per_b = s // tc
    return pl.pallas_call(
        functools.partial(_combine_kernel, tc=tc),
        grid_spec=pltpu.PrefetchScalarGridSpec(
            num_scalar_prefetch=1,
            grid=(b * per_b,),
            in_specs=[pl.BlockSpec(memory_space=pl.ANY),
                      pl.BlockSpec((tc, LANES), lambda i, pp: (i, 0)),
                      pl.BlockSpec((None, tc, d), lambda i, pp: (i // per_b, i % per_b, 0)),
                      pl.BlockSpec((None, 1, d), lambda i, pp: (i // per_b, 0, 5))],
            out_specs=pl.BlockSpec((None, tc, d), lambda i, pp: (i // per_b, i % per_b, 0)),
            scratch_shapes=[pltpu.VMEM((TOP_K, tc, d), F32), pltpu.SemaphoreType.DMA(())]),
        out_shape=jax.ShapeDtypeStruct((b, s, d), F32),
        compiler_params=_params(("arbitrary",)),
        name="combine_rows",
    )(ppos, ys, gates, x, mod)


def _final_norm_kernel(x_ref, g_ref, o_ref):
    o_ref[...] = _rms_rows(x_ref[...]) * g_ref[...]


def final_norm(x, g):
    b, s, d = x.shape
    tm = min(ROW_TILE, s)
    spec = pl.BlockSpec((None, tm, d), lambda i, j: (i, j, 0))
    return pl.pallas_call(
        _final_norm_kernel,
        grid=(b, s // tm),
        in_specs=[spec, pl.BlockSpec((1, d), lambda i, j: (0, 0))],
        out_specs=spec,
        out_shape=jax.ShapeDtypeStruct((b, s, d), F32),
        compiler_params=_params(("arbitrary", "arbitrary")),
        name="final_norm",
    )(x, g)


def _routing_tables(idx, rank, counts, n_experts, te, nblk):
    counts = counts[0, :n_experts]
    padded = (counts + te - 1) // te * te
    ends = jnp.cumsum(padded)
    starts = ends - padded
    ppos = (starts[idx[:, :TOP_K]] + rank[:, :TOP_K]).reshape(-1).astype(jnp.int32)
    nused = (ends[-1] // te).astype(jnp.int32)
    blk = jnp.arange(nblk, dtype=jnp.int32)
    blk_e = jnp.sum((ends[None, :] <= (blk * te)[:, None]).astype(jnp.int32), axis=1)
    blk_e = jnp.minimum(blk_e, n_experts - 1)
    blk_e = jnp.where(blk < nused, blk_e, blk_e[nused - 1])
    return ppos, blk_e, nused.reshape(1)


def _pad_lanes(a, fill=0.0):
    return jnp.pad(a, [(0, 0)] * (a.ndim - 1) + [(0, LANES - a.shape[-1])], constant_values=fill)


def kernel(x, c, positions, w_mod, b_mod, g_attn, w_in, b_forget, g_mix, w_out, g_ffn, w_router, b_router,
           w_gate_up, b_gate_up, w_down, b_down, g_final):
    b, s, d = x.shape
    depth = w_mod.shape[0]
    n_experts = w_router.shape[-1]
    d_mix = w_out.shape[1]
    n = b * s
    te = min(EXPERT_TILE, n)
    nblk = -(-(n * TOP_K + n_experts * (te - 1)) // te)
    cap = nblk * te

    mod_all = modulation(c, w_mod, b_mod)
    cos, sin = rope_tables(positions)
    for layer in range(depth):
        mod = mod_all[layer].reshape(b, 1, N_MOD * d)
        w_qkv = family_major_qkv_weight(w_in[layer], d_mix)
        w_f = _pad_lanes(w_in[layer, :, 3 * d_mix:]).astype(BF16)
        b_f = _pad_lanes(b_forget[layer][None, :])
        qkv_sb, qkv_dil, qkv_fox, logf = in_projection(x, mod, g_attn[layer][None, :], w_qkv, w_f, b_f)
        ccol, crow = forget_cumsum(logf)
        o_sb = sb_attention(qkv_sb)
        o_dil = dil_attention(qkv_dil, cos, sin)
        o_fox = fox_attention(qkv_fox, ccol, crow)
        x, h2, idx, gates, rank, counts = outproj_router(
            o_sb, o_dil, o_fox, g_mix[layer][None, :], w_out[layer].astype(BF16), x, mod, g_ffn[layer][None, :],
            _pad_lanes(w_router[layer]), _pad_lanes(b_router[layer][None, :], NEG_BIG))
        ppos, blk_e, nused = _routing_tables(idx, rank, counts, n_experts, te, nblk)
        xs = dispatch_rows(ppos, h2, cap)
        ys = expert_ffn(blk_e, nused, xs, w_gate_up[layer].astype(BF16), b_gate_up[layer],
                        w_down[layer].astype(BF16), b_down[layer], te)
        x = combine_rows(ppos, ys, gates, x, mod)
    return final_norm(x, g_final[None, :])
```

```python
import functools

import numpy as np
import jax
import jax.numpy as jnp
from jax import lax
from jax.experimental import pallas as pl
from jax.experimental.pallas import tpu as pltpu

F32 = jnp.float32
BF16 = jnp.bfloat16
HIGHEST = lax.Precision.HIGHEST

HEAD_DIM = 64
LANES = 128
N_HEADS_SB = 4
N_HEADS_DIL = 6
N_HEADS_FOX = 6
DIL_PATTERNS = ((128, 1), (512, 4), (2048, 16))
ROPE_THETA = 10000.0
TOP_K = 4
SWIGLU_LIMIT = 7.0
SWIGLU_ALPHA = 1.702
N_MOD = 6
EPS = 1e-6
NEG_BIG = -1e30
ATTN_TILE = 256
ROW_TILE = 512
EXPERT_TILE = 512
DMA_TILE = 256
VMEM_LIMIT = 48 * 1024 * 1024


def _params(sem, vmem=VMEM_LIMIT):
    return pltpu.CompilerParams(dimension_semantics=sem, vmem_limit_bytes=vmem)


def _rms_rows(x):
    return x * lax.rsqrt(jnp.mean(x * x, axis=-1, keepdims=True) + EPS)


def _softplus(z):
    return jnp.maximum(z, 0.0) + jnp.log(1.0 + jnp.exp(-jnp.abs(z)))


def _mod_kernel(c_ref, w_ref, b_ref, o_ref):
    c = c_ref[...]
    ca = c * jax.nn.sigmoid(c)
    o_ref[...] = jnp.dot(ca, w_ref[...], preferred_element_type=F32, precision=HIGHEST) + b_ref[...]


def modulation(c, w_mod, b_mod):
    depth, d, n6 = w_mod.shape
    b = c.shape[0]
    tn = min(n6, 1536)
    return pl.pallas_call(
        _mod_kernel,
        grid=(depth, n6 // tn),
        in_specs=[pl.BlockSpec((b, d), lambda l, j: (0, 0)),
                  pl.BlockSpec((None, d, tn), lambda l, j: (l, 0, j)),
                  pl.BlockSpec((None, 1, tn), lambda l, j: (l, 0, j))],
        out_specs=pl.BlockSpec((None, b, tn), lambda l, j: (l, 0, j)),
        out_shape=jax.ShapeDtypeStruct((depth, b, n6), F32),
        compiler_params=_params(("arbitrary", "arbitrary")),
        name="modulation",
    )(c, w_mod, b_mod.reshape(depth, 1, n6))


def _rope_table_kernel(pos_ref, invf_ref, cos_ref, sin_ref):
    ang = pos_ref[...].astype(F32) * invf_ref[...]
    lane = lax.broadcasted_iota(jnp.int32, ang.shape, 1)
    first_half = (lane % HEAD_DIM) < (HEAD_DIM // 2)
    s = jnp.sin(ang)
    cos_ref[...] = jnp.cos(ang)
    sin_ref[...] = jnp.where(first_half, -s, s)


def rope_tables(positions):
    b, s = positions.shape
    half = HEAD_DIM // 2
    inv_freq = ROPE_THETA ** (-np.arange(0, HEAD_DIM, 2, dtype=np.float64) / HEAD_DIM)
    invf = jnp.asarray(np.tile(inv_freq, LANES // half)[None, :], F32)
    return pl.pallas_call(
        _rope_table_kernel,
        grid=(b,),
        in_specs=[pl.BlockSpec((None, s, 1), lambda i: (i, 0, 0)),
                  pl.BlockSpec((1, LANES), lambda i: (0, 0))],
        out_specs=[pl.BlockSpec((None, s, LANES), lambda i: (i, 0, 0))] * 2,
        out_shape=[jax.ShapeDtypeStruct((b, s, LANES), F32)] * 2,
        compiler_params=_params(("arbitrary",)),
        name="rope_tables",
    )(positions.reshape(b, s, 1), invf)


def _rope(x, cos, sin_signed):
    lane = lax.broadcasted_iota(jnp.int32, x.shape, 1)
    first_half = (lane % HEAD_DIM) < (HEAD_DIM // 2)
    half = HEAD_DIM // 2
    partner = jnp.where(first_half, pltpu.roll(x, LANES - half, 1), pltpu.roll(x, half, 1))
    return x * cos + partner * sin_signed


FAMILY_HEADS = (N_HEADS_SB, N_HEADS_DIL, N_HEADS_FOX)


def family_major_qkv_weight(w_in_layer, d_mix):
    wq = w_in_layer[:, :d_mix] * (HEAD_DIM ** -0.5)
    wk = w_in_layer[:, d_mix:2 * d_mix]
    wv = w_in_layer[:, 2 * d_mix:3 * d_mix]
    cols, lo = [], 0
    for nh in FAMILY_HEADS:
        hi = lo + nh * HEAD_DIM
        cols += [wq[:, lo:hi], wk[:, lo:hi], wv[:, lo:hi]]
        lo = hi
    return jnp.concatenate(cols, axis=1).astype(BF16)


def _inproj_kernel(x_ref, sh_ref, sc_ref, g_ref, wqkv_ref, wf_ref, bf_ref, sb_ref, dil_ref, fox_ref, logf_ref):
    x = x_ref[...]
    h = _rms_rows(x) * g_ref[...] * (1.0 + sc_ref[...]) + sh_ref[...]
    hb = h.astype(BF16)
    lo = 0
    for o_ref in (sb_ref, dil_ref, fox_ref):
        w = o_ref.shape[-1]
        o_ref[...] = jnp.dot(hb, wqkv_ref[:, lo:lo + w], preferred_element_type=F32).astype(BF16)
        lo += w
    f = jnp.dot(hb, wf_ref[...], preferred_element_type=F32) + bf_ref[...]
    logf_ref[...] = -_softplus(-f)


def in_projection(x, mod, g, w_qkv, w_f, b_f):
    b, s, d = x.shape
    d_mix = w_qkv.shape[1] // 3
    tm = min(ROW_TILE, s)
    widths = [3 * nh * HEAD_DIM for nh in FAMILY_HEADS]
    return pl.pallas_call(
        _inproj_kernel,
        grid=(b, s // tm),
        in_specs=[pl.BlockSpec((None, tm, d), lambda i, j: (i, j, 0)),
                  pl.BlockSpec((None, 1, d), lambda i, j: (i, 0, 0)),
                  pl.BlockSpec((None, 1, d), lambda i, j: (i, 0, 1)),
                  pl.BlockSpec((1, d), lambda i, j: (0, 0)),
                  pl.BlockSpec((d, 3 * d_mix), lambda i, j: (0, 0)),
                  pl.BlockSpec((d, LANES), lambda i, j: (0, 0)),
                  pl.BlockSpec((1, LANES), lambda i, j: (0, 0))],
        out_specs=[pl.BlockSpec((None, tm, w), lambda i, j: (i, j, 0)) for w in widths + [LANES]],
        out_shape=[jax.ShapeDtypeStruct((b, s, w), BF16) for w in widths]
        + [jax.ShapeDtypeStruct((b, s, LANES), F32)],
        compiler_params=_params(("arbitrary", "arbitrary")),
        name="in_projection",
    )(x, mod, mod, g, w_qkv, w_f, b_f)


AUG_TERMS = 3


def _own_half(shape, h):
    lane = lax.broadcasted_iota(jnp.int32, shape, len(shape) - 1)
    return (lane < HEAD_DIM) if h % 2 == 0 else (lane >= HEAD_DIM)


def _spare_lane(shape, h):
    lane = lax.broadcasted_iota(jnp.int32, shape, len(shape) - 1)
    return lane - (HEAD_DIM if h % 2 == 0 else 0)


def _cumsum_kernel(logf_ref, qaug_ref, kaug_ref, *, nh):
    s = logf_ref.shape[0]
    r = lax.broadcasted_iota(jnp.int32, (LANES, LANES), 0)
    c = lax.broadcasted_iota(jnp.int32, (LANES, LANES), 1)
    tri = (c <= r).astype(F32)
    carry = jnp.zeros((1, LANES), F32)
    for i in range(s // LANES):
        rows = slice(i * LANES, (i + 1) * LANES)
        cs = jnp.dot(tri, logf_ref[rows, :], preferred_element_type=F32, precision=HIGHEST) + carry
        carry = cs[LANES - 1:LANES, :]
        for h in range(nh):
            full = jnp.broadcast_to(cs[:, h:h + 1], (LANES, LANES))
            parts, rest = [], full
            for _ in range(AUG_TERMS):
                piece = rest.astype(BF16).astype(F32)
                parts.append(piece)
                rest = rest - piece
            j = _spare_lane((LANES, LANES), h)
            qa = jnp.where((j >= AUG_TERMS) & (j < 2 * AUG_TERMS), 1.0, 0.0)
            ka = jnp.where((j >= 0) & (j < AUG_TERMS), 1.0, 0.0)
            for n, piece in enumerate(parts):
                qa = jnp.where(j == n, piece, qa)
                ka = jnp.where(j == AUG_TERMS + n, -piece, ka)
            qaug_ref[rows, h * LANES:(h + 1) * LANES] = qa.astype(BF16)
            kaug_ref[rows, h * LANES:(h + 1) * LANES] = ka.astype(BF16)


def forget_cumsum(logf, nh):
    b, s, _ = logf.shape
    out = pl.BlockSpec((None, s, nh * LANES), lambda i: (i, 0, 0))
    return pl.pallas_call(
        functools.partial(_cumsum_kernel, nh=nh),
        grid=(b,),
        in_specs=[pl.BlockSpec((None, s, LANES), lambda i: (i, 0, 0))],
        out_specs=[out, out],
        out_shape=[jax.ShapeDtypeStruct((b, s, nh * LANES), BF16)] * 2,
        compiler_params=_params(("arbitrary",)),
        name="forget_cumsum",
    )(logf)


def _tile_iotas(t):
    return lax.broadcasted_iota(jnp.int32, (t, t), 0), lax.broadcasted_iota(jnp.int32, (t, t), 1)


def _pair_tile(ref, rows, h):
    p = h // 2
    return ref[rows, p * LANES:(p + 1) * LANES]


def _keep_own(tile, h, other=0.0):
    return jnp.where(_own_half(tile.shape, h), tile.astype(F32), other).astype(BF16)


def _qk(q, k):
    return lax.dot_general(q, k, (((1,), (1,)), ((), ())), preferred_element_type=F32)


def _merge_pairs(per_head, o_ref):
    tiles = [jnp.where(_own_half(per_head[h].shape, h), per_head[h], per_head[h + 1])
             for h in range(0, len(per_head), 2)]
    o_ref[...] = jnp.concatenate(tiles, axis=-1).astype(o_ref.dtype)


def _sb_kernel(q_ref, k_ref, v_ref, o_ref, *, t, nh):
    qi = pl.program_id(1)
    row, col = _tile_iotas(t)
    strict = col < row
    upper = (row > col).astype(BF16)
    qs = [_keep_own(_pair_tile(q_ref, slice(None), h), h) for h in range(nh)]

    def tile(ki, carries, diag):
        rows = pl.ds(pl.multiple_of(ki * t, t), t)
        heads = range(nh)
        zs = [_qk(qs[h], _pair_tile(k_ref, rows, h)) for h in heads]
        sps = [_softplus(z) for z in zs]
        log_nots = [jnp.where(strict, -sp, 0.0) if diag else -sp for sp in sps]
        his = [ln.astype(BF16) for ln in log_nots]
        los = [(ln - hi.astype(F32)).astype(BF16) for ln, hi in zip(log_nots, his)]
        betweens = [jnp.dot(hi, upper, preferred_element_type=F32) + jnp.dot(lo, upper, preferred_element_type=F32)
                    for hi, lo in zip(his, los)]
        out = []
        for h in heads:
            suffix, acc = carries[h]
            w = jnp.exp(zs[h] - sps[h] + betweens[h] + suffix)
            if diag:
                w = jnp.where(strict, w, 0.0)
            acc = acc + jnp.dot(w.astype(BF16), _pair_tile(v_ref, rows, h), preferred_element_type=F32)
            out.append((suffix + betweens[h][:, 0:1] + log_nots[h][:, 0:1], acc))
        return tuple(out)

    init = tuple((jnp.zeros((t, 1), F32), jnp.zeros((t, LANES), F32)) for _ in range(nh))
    carries = tile(qi, init, True)
    carries = lax.fori_loop(0, qi, lambda j, cr: tile(qi - 1 - j, cr, False), carries)
    _merge_pairs([cr[1] for cr in carries], o_ref)


def _online_softmax_steps(scores, values, carries):
    m_news = [jnp.maximum(m, jnp.max(s, axis=-1, keepdims=True)) for s, (m, _) in zip(scores, carries)]
    ps = [jnp.exp(s - m_new).astype(BF16) for s, m_new in zip(scores, m_news)]
    return tuple((m_new, jnp.exp(m - m_new) * acc + jnp.dot(p, v, preferred_element_type=F32))
                 for p, v, m_new, (m, acc) in zip(ps, values, m_news, carries))


def _softmax_init(t, nh):
    return tuple((jnp.full((t, 1), NEG_BIG, F32), jnp.zeros((t, LANES), F32)) for _ in range(nh))


def _softmax_finish(carries, o_ref):
    outs = []
    for h, (_, acc) in enumerate(carries):
        denom = jnp.sum(jnp.where(_spare_lane(acc.shape, h) == 0, acc, 0.0), axis=-1, keepdims=True)
        outs.append(acc / denom)
    _merge_pairs(outs, o_ref)


def _pad_values(v_ref, vpad_ref, nh):
    for h in range(nh):
        tile = _pair_tile(v_ref, slice(None), h)
        ones_col = jnp.where(_spare_lane(tile.shape, h) == 0, 1.0, 0.0)
        vpad_ref[:, h * LANES:(h + 1) * LANES] = _keep_own(tile, h, ones_col)


def _fox_kernel(q_ref, k_ref, v_ref, qaug_ref, kaug_ref, o_ref, kpad_ref, vpad_ref, *, t, nh):
    qi = pl.program_id(1)
    row, col = _tile_iotas(t)

    @pl.when(qi == 0)
    def _():
        _pad_values(v_ref, vpad_ref, nh)
        for h in range(nh):
            lanes = slice(h * LANES, (h + 1) * LANES)
            kpad_ref[:, lanes] = _keep_own(_pair_tile(k_ref, slice(None), h), h, kaug_ref[:, lanes].astype(F32))

    qs = [_keep_own(_pair_tile(q_ref, slice(None), h), h, qaug_ref[:, h * LANES:(h + 1) * LANES].astype(F32))
          for h in range(nh)]

    def tile(ki, carries, diag):
        rows = pl.ds(pl.multiple_of(ki * t, t), t)
        lanes = [slice(h * LANES, (h + 1) * LANES) for h in range(nh)]
        scores = [_qk(qs[h], kpad_ref[rows, lanes[h]]) for h in range(nh)]
        if diag:
            scores = [jnp.where(col <= row, s, NEG_BIG) for s in scores]
        return _online_softmax_steps(scores, [vpad_ref[rows, ln] for ln in lanes], carries)

    carries = lax.fori_loop(0, qi, lambda ki, cr: tile(ki, cr, False), _softmax_init(t, nh))
    _softmax_finish(tile(qi, carries, True), o_ref)


DIL_NEAR_TILES = 3


def _dilation_log_count(d):
    cnt = None
    for window, dil in DIL_PATTERNS:
        hit = (d >= 0) & (d <= window) & ((d & (dil - 1)) == 0)
        term = jnp.where(hit, 1.0, 0.0)
        cnt = term if cnt is None else cnt + term
    return jnp.where(cnt > 0.0, jnp.log(jnp.maximum(cnt, 1.0)), NEG_BIG)


def _dil_kernel(q_ref, k_ref, v_ref, cos_ref, sin_ref, o_ref, kr_ref, vpad_ref, bias_ref, *, t, nh):
    qi = pl.program_id(1)

    @pl.when(qi == 0)
    def _():
        _pad_values(v_ref, vpad_ref, nh)
        for p in range(nh // 2):
            lanes = slice(p * LANES, (p + 1) * LANES)
            kr_ref[:, lanes] = _rope(k_ref[:, lanes].astype(F32), cos_ref[...], sin_ref[...]).astype(BF16)
        row, col = _tile_iotas(t)
        for delta in range(DIL_NEAR_TILES + 1):
            bias_ref[delta] = _dilation_log_count(delta * t + row - col)

    qrows = pl.ds(pl.multiple_of(qi * t, t), t)
    cos, sin = cos_ref[qrows, :], sin_ref[qrows, :]
    qs = [_keep_own(_rope(_pair_tile(q_ref, slice(None), h).astype(F32), cos, sin), h) for h in range(nh)]

    def tile(ki, carries):
        rows = pl.ds(pl.multiple_of(ki * t, t), t)
        bias = bias_ref[jnp.minimum(qi - ki, DIL_NEAR_TILES)]
        scores = [_qk(qs[h], _pair_tile(kr_ref, rows, h)) + bias for h in range(nh)]
        values = [vpad_ref[rows, h * LANES:(h + 1) * LANES] for h in range(nh)]
        return _online_softmax_steps(scores, values, carries)

    _softmax_finish(lax.fori_loop(0, qi + 1, tile, _softmax_init(t, nh)), o_ref)


def _attention_call(kernel, qkv, n_heads, extra_inputs, extra_specs, scratch, name):
    b, s, w3 = qkv.shape
    w = w3 // 3
    t = min(ATTN_TILE, s)
    return pl.pallas_call(
        functools.partial(kernel, t=t, nh=n_heads),
        grid=(b, s // t),
        in_specs=[pl.BlockSpec((None, t, w), lambda i, j: (i, j, 0)),
                  pl.BlockSpec((None, s, w), lambda i, j: (i, 0, 1)),
                  pl.BlockSpec((None, s, w), lambda i, j: (i, 0, 2))] + extra_specs,
        out_specs=pl.BlockSpec((None, t, w), lambda i, j: (i, j, 0)),
        out_shape=jax.ShapeDtypeStruct((b, s, w), BF16),
        scratch_shapes=scratch,
        compiler_params=_params(("arbitrary", "arbitrary")),
        name=name,
    )(qkv, qkv, qkv, *extra_inputs)


def sb_attention(qkv):
    return _attention_call(_sb_kernel, qkv, N_HEADS_SB, [], [], [], "sb_attention")


def dil_attention(qkv, cos, sin):
    s = qkv.shape[1]
    t = min(ATTN_TILE, s)
    widest_window, widest_dil = DIL_PATTERNS[-1]
    assert s - 1 <= widest_window and t % widest_dil == 0
    assert all(window < (DIL_NEAR_TILES - 1) * t + 1 for window, _ in DIL_PATTERNS[:-1])
    tab = pl.BlockSpec((None, s, LANES), lambda i, j: (i, 0, 0))
    scratch = [pltpu.VMEM((s, qkv.shape[-1] // 3), BF16), pltpu.VMEM((s, N_HEADS_DIL * LANES), BF16),
               pltpu.VMEM((DIL_NEAR_TILES + 1, t, t), F32)]
    return _attention_call(_dil_kernel, qkv, N_HEADS_DIL, [cos, sin], [tab, tab], scratch, "dil_attention")


def fox_attention(qkv, qaug, kaug):
    s = qkv.shape[1]
    t = min(ATTN_TILE, s)
    wide = N_HEADS_FOX * LANES
    specs = [pl.BlockSpec((None, t, wide), lambda i, j: (i, j, 0)),
             pl.BlockSpec((None, s, wide), lambda i, j: (i, 0, 0))]
    scratch = [pltpu.VMEM((s, wide), BF16), pltpu.VMEM((s, wide), BF16)]
    return _attention_call(_fox_kernel, qkv, N_HEADS_FOX, [qaug, kaug], specs, scratch, "fox_attention")


def _outproj_router_kernel(osb_ref, odil_ref, ofox_ref, gmix_ref, wout_ref, x_ref, ga_ref, sc_ref, sh_ref,
                           gffn_ref, wr_ref, br_ref,
                           xo_ref, h2_ref, idx_ref, gate_ref, rank_ref, cnt_ref, carry_ref):
    first = (pl.program_id(0) == 0) & (pl.program_id(1) == 0)

    @pl.when(first)
    def _():
        carry_ref[...] = jnp.zeros_like(carry_ref)

    a = None
    lo = 0
    for o_ref in (osb_ref, odil_ref, ofox_ref):
        w = o_ref.shape[-1]
        on = (_rms_rows(o_ref[...].astype(F32)) * gmix_ref[:, lo:lo + w]).astype(BF16)
        part = jnp.dot(on, wout_ref[lo:lo + w, :], preferred_element_type=F32)
        a = part if a is None else a + part
        lo += w
    xn = x_ref[...] + ga_ref[...] * a
    xo_ref[...] = xn
    h2 = _rms_rows(xn) * gffn_ref[...] * (1.0 + sc_ref[...]) + sh_ref[...]
    h2_ref[...] = h2

    logits = jnp.dot(h2, wr_ref[...], preferred_element_type=F32, precision=HIGHEST) + br_ref[...]
    tm = logits.shape[0]
    lane = lax.broadcasted_iota(jnp.int32, (tm, LANES), 1)
    vals, idxs = [], []
    rest = logits
    for _ in range(TOP_K):
        m = jnp.max(rest, axis=-1, keepdims=True)
        ik = jnp.min(jnp.where(rest == m, lane, LANES), axis=-1, keepdims=True)
        vals.append(m)
        idxs.append(ik)
        rest = jnp.where(lane == ik, -jnp.inf, rest)
    es = [jnp.exp(v - vals[0]) for v in vals]
    den = es[0] + es[1] + es[2] + es[3]

    hot = [(lane == ik) for ik in idxs]
    multi = jnp.where(hot[0] | hot[1] | hot[2] | hot[3], 1.0, 0.0)
    r = lax.broadcasted_iota(jnp.int32, (tm, tm), 0)
    c = lax.broadcasted_iota(jnp.int32, (tm, tm), 1)
    before = (c < r).astype(BF16)
    prior = jnp.dot(before, multi.astype(BF16), preferred_element_type=F32) + carry_ref[...]
    idx_out = jnp.zeros((tm, LANES), jnp.int32)
    gate_out = jnp.zeros((tm, LANES), F32)
    rank_out = jnp.zeros((tm, LANES), F32)
    for k in range(TOP_K):
        rk = jnp.sum(jnp.where(hot[k], prior, 0.0), axis=-1, keepdims=True)
        idx_out = jnp.where(lane == k, idxs[k], idx_out)
        gate_out = jnp.where(lane == k, es[k] / den, gate_out)
        rank_out = jnp.where(lane == k, rk, rank_out)
    idx_ref[...] = idx_out
    gate_ref[...] = gate_out
    rank_ref[...] = rank_out.astype(jnp.int32)
    total = prior[tm - 1:tm, :] + multi[tm - 1:tm, :]
    carry_ref[...] = total
    cnt_ref[...] = total.astype(jnp.int32)


def outproj_router(o_sb, o_dil, o_fox, g_mix, w_out, x, mod, g_ffn, w_r, b_r):
    b, s, d = x.shape
    tm = min(ROW_TILE, s)
    n = b * s

    def act(w):
        return pl.BlockSpec((None, tm, w), lambda i, j: (i, j, 0))

    def modspec(col):
        return pl.BlockSpec((None, 1, d), lambda i, j: (i, 0, col))

    def const(shape):
        return pl.BlockSpec(shape, lambda i, j: (0, 0))

    tok = pl.BlockSpec((tm, LANES), lambda i, j: (i * (s // tm) + j, 0))
    return pl.pallas_call(
        _outproj_router_kernel,
        grid=(b, s // tm),
        in_specs=[act(o_sb.shape[-1]), act(o_dil.shape[-1]), act(o_fox.shape[-1]),
                  const((1, d)), const(w_out.shape), act(d),
                  modspec(2), modspec(4), modspec(3),
                  const((1, d)), const((d, LANES)), const((1, LANES))],
        out_specs=[act(d), pl.BlockSpec((tm, d), lambda i, j: (i * (s // tm) + j, 0)), tok, tok, tok,
                   const((1, LANES))],
        out_shape=[jax.ShapeDtypeStruct((b, s, d), F32), jax.ShapeDtypeStruct((n, d), F32),
                   jax.ShapeDtypeStruct((n, LANES), jnp.int32), jax.ShapeDtypeStruct((n, LANES), F32),
                   jax.ShapeDtypeStruct((n, LANES), jnp.int32), jax.ShapeDtypeStruct((1, LANES), jnp.int32)],
        scratch_shapes=[pltpu.VMEM((1, LANES), F32)],
        compiler_params=_params(("arbitrary", "arbitrary")),
        name="outproj_router",
    )(o_sb, o_dil, o_fox, g_mix, w_out, x, mod, mod, mod, g_ffn, w_r, b_r)


def _row_copy(src_hbm, dst_hbm, sem, src_row, dst_row):
    return pltpu.make_async_copy(src_hbm.at[pl.ds(src_row, 1)], dst_hbm.at[pl.ds(dst_row, 1)], sem)


def _dispatch_kernel(ppos_ref, h_ref, zeros_hbm, xs_hbm, sem, *, tg):
    del zeros_hbm
    base = pl.program_id(0) * tg

    def issue(r, _):
        for k in range(TOP_K):
            _row_copy(h_ref, xs_hbm, sem, r, ppos_ref[(base + r) * TOP_K + k]).start(priority=k % 2)
        return 0

    def drain(r, _):
        for k in range(TOP_K):
            _row_copy(h_ref, xs_hbm, sem, 0, 0).wait()
        return 0

    lax.fori_loop(0, tg, issue, 0)
    lax.fori_loop(0, tg, drain, 0)


def dispatch_rows(ppos, h, cap):
    n, d = h.shape
    tg = min(DMA_TILE, n)
    zeros = jnp.zeros((cap, d), h.dtype)
    return pl.pallas_call(
        functools.partial(_dispatch_kernel, tg=tg),
        grid_spec=pltpu.PrefetchScalarGridSpec(
            num_scalar_prefetch=1,
            grid=(n // tg,),
            in_specs=[pl.BlockSpec((tg, d), lambda i, pp: (i, 0)), pl.BlockSpec(memory_space=pl.ANY)],
            out_specs=pl.BlockSpec(memory_space=pl.ANY),
            scratch_shapes=[pltpu.SemaphoreType.DMA(())]),
        out_shape=jax.ShapeDtypeStruct((cap, d), h.dtype),
        input_output_aliases={2: 0},
        compiler_params=pltpu.CompilerParams(dimension_semantics=("arbitrary",), has_side_effects=True),
        name="dispatch_rows",
    )(ppos, h, zeros)


def _expert_kernel(blk_e_ref, nused_ref, xs_ref, wgu_ref, bgu_ref, wdn_ref, bdn_ref, o_ref):
    del blk_e_ref

    @pl.when(pl.program_id(0) < nused_ref[0])
    def _():
        f = wdn_ref.shape[0]
        gu = jnp.dot(xs_ref[...].astype(BF16), wgu_ref[...], preferred_element_type=F32) + bgu_ref[...]
        gate = jnp.minimum(gu[:, :f], SWIGLU_LIMIT)
        up = jnp.clip(gu[:, f:], -SWIGLU_LIMIT, SWIGLU_LIMIT)
        act = (up + 1.0) * (gate * jax.nn.sigmoid(SWIGLU_ALPHA * gate))
        o_ref[...] = jnp.dot(act.astype(BF16), wdn_ref[...], preferred_element_type=F32) + bdn_ref[...]


def expert_ffn(blk_e, nused, xs, w_gu, b_gu, w_dn, b_dn, te):
    cap, d = xs.shape
    ne, _, f2 = w_gu.shape
    f = f2 // 2
    nblk = cap // te

    def rows(i, be, nu):
        return (jnp.minimum(i, nu[0] - 1), 0)

    def per_expert(i, be, nu):
        return (be[i], 0, 0)

    return pl.pallas_call(
        _expert_kernel,
        grid_spec=pltpu.PrefetchScalarGridSpec(
            num_scalar_prefetch=2,
            grid=(nblk,),
            in_specs=[pl.BlockSpec((te, d), rows),
                      pl.BlockSpec((None, d, f2), per_expert),
                      pl.BlockSpec((None, 1, f2), per_expert),
                      pl.BlockSpec((None, f, d), per_expert),
                      pl.BlockSpec((None, 1, d), per_expert)],
            out_specs=pl.BlockSpec((te, d), rows)),
        out_shape=jax.ShapeDtypeStruct((cap, d), F32),
        compiler_params=_params(("arbitrary",)),
        name="expert_ffn",
    )(blk_e, nused, xs, w_gu, b_gu.reshape(ne, 1, f2), w_dn, b_dn.reshape(ne, 1, d))


def _combine_kernel(ppos_ref, ys_hbm, gate_ref, x_ref, ga_ref, xo_ref, buf_ref, sem, *, tc):
    base = pl.program_id(0) * tc

    def issue(r, _):
        for k in range(TOP_K):
            src = ppos_ref[(base + r) * TOP_K + k]
            pltpu.make_async_copy(ys_hbm.at[pl.ds(src, 1)], buf_ref.at[k, pl.ds(r, 1)], sem).start(priority=k % 2)
        return 0

    def drain(r, _):
        for k in range(TOP_K):
            pltpu.make_async_copy(ys_hbm.at[pl.ds(0, 1)], buf_ref.at[0, pl.ds(0, 1)], sem).wait()
        return 0

    lax.fori_loop(0, tc, issue, 0)
    lax.fori_loop(0, tc, drain, 0)
    g = gate_ref[...]
    y = g[:, 0:1] * buf_ref[0]
    for k in range(1, TOP_K):
        y = y + g[:, k:k + 1] * buf_ref[k]
    xo_ref[...] = x_ref[...] + ga_ref[...] * y


def combine_rows(ppos, ys, gates, x, mod):
    b, s, d = x.shape
    tc = min(DMA_TILE, s)
    per_b = s // tc
    return pl.pallas_call(
        functools.partial(_combine_kernel, tc=tc),
        grid_spec=pltpu.PrefetchScalarGridSpec(
            num_scalar_prefetch=1,
            grid=(b * per_b,),
            in_specs=[pl.BlockSpec(memory_space=pl.ANY),
                      pl.BlockSpec((tc, LANES), lambda i, pp: (i, 0)),
                      pl.BlockSpec((None, tc, d), lambda i, pp: (i // per_b, i % per_b, 0)),
                      pl.BlockSpec((None, 1, d), lambda i, pp: (i // per_b, 0, 5))],
            out_specs=pl.BlockSpec((None, tc, d), lambda i, pp: (i // per_b, i % per_b, 0)),
            scratch_shapes=[pltpu.VMEM((TOP_K, tc, d), F32), pltpu.SemaphoreType.DMA(())]),
        out_shape=jax.ShapeDtypeStruct((b, s, d), F32),
        compiler_params=_params(("arbitrary",)),
        name="combine_rows",
    )(ppos, ys, gates, x, mod)


def _final_norm_kernel(x_ref, g_ref, o_ref):
    o_ref[...] = _rms_rows(x_ref[...]) * g_ref[...]


def final_norm(x, g):
    b, s, d = x.shape
    tm = min(ROW_TILE, s)
    spec = pl.BlockSpec((None, tm, d), lambda i, j: (i, j, 0))
    return pl.pallas_call(
        _final_norm_kernel,
        grid=(b, s // tm),
        in_specs=[spec, pl.BlockSpec((1, d), lambda i, j: (0, 0))],
        out_specs=spec,
        out_shape=jax.ShapeDtypeStruct((b, s, d), F32),
        compiler_params=_params(("arbitrary", "arbitrary")),
        name="final_norm",
    )(x, g)


def _routing_tables(idx, rank, counts, n_experts, te, nblk):
    counts = counts[0, :n_experts]
    padded = (counts + te - 1) // te * te
    ends = jnp.cumsum(padded)
    starts = ends - padded
    ppos = (starts[idx[:, :TOP_K]] + rank[:, :TOP_K]).reshape(-1).astype(jnp.int32)
    nused = (ends[-1] // te).astype(jnp.int32)
    blk = jnp.arange(nblk, dtype=jnp.int32)
    blk_e = jnp.sum((ends[None, :] <= (blk * te)[:, None]).astype(jnp.int32), axis=1)
    blk_e = jnp.minimum(blk_e, n_experts - 1)
    blk_e = jnp.where(blk < nused, blk_e, blk_e[nused - 1])
    return ppos, blk_e, nused.reshape(1)


def _pad_lanes(a, fill=0.0):
    return jnp.pad(a, [(0, 0)] * (a.ndim - 1) + [(0, LANES - a.shape[-1])], constant_values=fill)


def kernel(x, c, positions, w_mod, b_mod, g_attn, w_in, b_forget, g_mix, w_out, g_ffn, w_router, b_router,
           w_gate_up, b_gate_up, w_down, b_down, g_final):
    b, s, d = x.shape
    depth = w_mod.shape[0]
    n_experts = w_router.shape[-1]
    d_mix = w_out.shape[1]
    n = b * s
    te = min(EXPERT_TILE, n)
    nblk = -(-(n * TOP_K + n_experts * (te - 1)) // te)
    cap = nblk * te

    mod_all = modulation(c, w_mod, b_mod)
    cos, sin = rope_tables(positions)
    for layer in range(depth):
        mod = mod_all[layer].reshape(b, 1, N_MOD * d)
        w_qkv = family_major_qkv_weight(w_in[layer], d_mix)
        w_f = _pad_lanes(w_in[layer, :, 3 * d_mix:]).astype(BF16)
        b_f = _pad_lanes(b_forget[layer][None, :])
        qkv_sb, qkv_dil, qkv_fox, logf = in_projection(x, mod, g_attn[layer][None, :], w_qkv, w_f, b_f)
        qaug, kaug = forget_cumsum(logf, N_HEADS_FOX)
        o_sb = sb_attention(qkv_sb)
        o_dil = dil_attention(qkv_dil, cos, sin)
        o_fox = fox_attention(qkv_fox, qaug, kaug)
        x, h2, idx, gates, rank, counts = outproj_router(
            o_sb, o_dil, o_fox, g_mix[layer][None, :], w_out[layer].astype(BF16), x, mod, g_ffn[layer][None, :],
            _pad_lanes(w_router[layer]), _pad_lanes(b_router[layer][None, :], NEG_BIG))
        ppos, blk_e, nused = _routing_tables(idx, rank, counts, n_experts, te, nblk)
        xs = dispatch_rows(ppos, h2, cap)
        ys = expert_ffn(blk_e, nused, xs, w_gate_up[layer].astype(BF16), b_gate_up[layer],
                        w_down[layer].astype(BF16), b_down[layer], te)
        x = combine_rows(ppos, ys, gates, x, mod)
    return final_norm(x, g_final[None, :])
```

```python
import functools

import numpy as np
import jax
import jax.numpy as jnp
from jax import lax
from jax.experimental import pallas as pl
from jax.experimental.pallas import tpu as pltpu

F32 = jnp.float32
BF16 = jnp.bfloat16
HIGHEST = lax.Precision.HIGHEST

HEAD_DIM = 64
LANES = 128
N_HEADS_SB = 4
N_HEADS_DIL = 6
N_HEADS_FOX = 6
DIL_PATTERNS = ((128, 1), (512, 4), (2048, 16))
ROPE_THETA = 10000.0
TOP_K = 4
SWIGLU_LIMIT = 7.0
SWIGLU_ALPHA = 1.702
N_MOD = 6
EPS = 1e-6
NEG_BIG = -1e30
ATTN_TILE = 256
ROW_TILE = 512
EXPERT_TILE = 512
VMEM_LIMIT = 48 * 1024 * 1024


def _params(sem, vmem=VMEM_LIMIT):
    return pltpu.CompilerParams(dimension_semantics=sem, vmem_limit_bytes=vmem)


def _rms_rows(x):
    return x * lax.rsqrt(jnp.mean(x * x, axis=-1, keepdims=True) + EPS)


def _softplus(z):
    return jnp.maximum(z, 0.0) + jnp.log(1.0 + jnp.exp(-jnp.abs(z)))


def _mod_kernel(c_ref, w_ref, b_ref, o_ref):
    c = c_ref[...]
    ca = c * jax.nn.sigmoid(c)
    o_ref[...] = jnp.dot(ca, w_ref[...], preferred_element_type=F32, precision=HIGHEST) + b_ref[...]


def modulation(c, w_mod, b_mod):
    depth, d, n6 = w_mod.shape
    b = c.shape[0]
    tn = min(n6, 1536)
    return pl.pallas_call(
        _mod_kernel,
        grid=(depth, n6 // tn),
        in_specs=[pl.BlockSpec((b, d), lambda l, j: (0, 0)),
                  pl.BlockSpec((None, d, tn), lambda l, j: (l, 0, j)),
                  pl.BlockSpec((None, 1, tn), lambda l, j: (l, 0, j))],
        out_specs=pl.BlockSpec((None, b, tn), lambda l, j: (l, 0, j)),
        out_shape=jax.ShapeDtypeStruct((depth, b, n6), F32),
        compiler_params=_params(("arbitrary", "arbitrary")),
        name="modulation",
    )(c, w_mod, b_mod.reshape(depth, 1, n6))


def _rope_table_kernel(pos_ref, invf_ref, cos_ref, sin_ref):
    ang = pos_ref[...].astype(F32) * invf_ref[...]
    lane = lax.broadcasted_iota(jnp.int32, ang.shape, 1)
    first_half = (lane % HEAD_DIM) < (HEAD_DIM // 2)
    s = jnp.sin(ang)
    cos_ref[...] = jnp.cos(ang)
    sin_ref[...] = jnp.where(first_half, -s, s)


def rope_tables(positions):
    b, s = positions.shape
    half = HEAD_DIM // 2
    inv_freq = ROPE_THETA ** (-np.arange(0, HEAD_DIM, 2, dtype=np.float64) / HEAD_DIM)
    invf = jnp.asarray(np.tile(inv_freq, LANES // half)[None, :], F32)
    return pl.pallas_call(
        _rope_table_kernel,
        grid=(b,),
        in_specs=[pl.BlockSpec((None, s, 1), lambda i: (i, 0, 0)),
                  pl.BlockSpec((1, LANES), lambda i: (0, 0))],
        out_specs=[pl.BlockSpec((None, s, LANES), lambda i: (i, 0, 0))] * 2,
        out_shape=[jax.ShapeDtypeStruct((b, s, LANES), F32)] * 2,
        compiler_params=_params(("arbitrary",)),
        name="rope_tables",
    )(positions.reshape(b, s, 1), invf)


def _rope(x, cos, sin_signed):
    lane = lax.broadcasted_iota(jnp.int32, x.shape, 1)
    first_half = (lane % HEAD_DIM) < (HEAD_DIM // 2)
    half = HEAD_DIM // 2
    partner = jnp.where(first_half, pltpu.roll(x, LANES - half, 1), pltpu.roll(x, half, 1))
    return x * cos + partner * sin_signed


FAMILY_HEADS = (N_HEADS_SB, N_HEADS_DIL, N_HEADS_FOX)


def family_major_qkv_weight(w_in_layer, d_mix):
    wq = w_in_layer[:, :d_mix] * (HEAD_DIM ** -0.5)
    wk = w_in_layer[:, d_mix:2 * d_mix]
    wv = w_in_layer[:, 2 * d_mix:3 * d_mix]
    cols, lo = [], 0
    for nh in FAMILY_HEADS:
        hi = lo + nh * HEAD_DIM
        cols += [wq[:, lo:hi], wk[:, lo:hi], wv[:, lo:hi]]
        lo = hi
    return jnp.concatenate(cols, axis=1).astype(BF16)


def _inproj_kernel(x_ref, sh_ref, sc_ref, g_ref, wqkv_ref, wf_ref, bf_ref, sb_ref, dil_ref, fox_ref, logf_ref):
    x = x_ref[...]
    h = _rms_rows(x) * g_ref[...] * (1.0 + sc_ref[...]) + sh_ref[...]
    hb = h.astype(BF16)
    lo = 0
    for o_ref in (sb_ref, dil_ref, fox_ref):
        w = o_ref.shape[-1]
        o_ref[...] = jnp.dot(hb, wqkv_ref[:, lo:lo + w], preferred_element_type=F32).astype(BF16)
        lo += w
    f = jnp.dot(hb, wf_ref[...], preferred_element_type=F32) + bf_ref[...]
    logf_ref[...] = -_softplus(-f)


def in_projection(x, mod, g, w_qkv, w_f, b_f):
    b, s, d = x.shape
    d_mix = w_qkv.shape[1] // 3
    tm = min(ROW_TILE, s)
    widths = [3 * nh * HEAD_DIM for nh in FAMILY_HEADS]
    return pl.pallas_call(
        _inproj_kernel,
        grid=(b, s // tm),
        in_specs=[pl.BlockSpec((None, tm, d), lambda i, j: (i, j, 0)),
                  pl.BlockSpec((None, 1, d), lambda i, j: (i, 0, 0)),
                  pl.BlockSpec((None, 1, d), lambda i, j: (i, 0, 1)),
                  pl.BlockSpec((1, d), lambda i, j: (0, 0)),
                  pl.BlockSpec((d, 3 * d_mix), lambda i, j: (0, 0)),
                  pl.BlockSpec((d, LANES), lambda i, j: (0, 0)),
                  pl.BlockSpec((1, LANES), lambda i, j: (0, 0))],
        out_specs=[pl.BlockSpec((None, tm, w), lambda i, j: (i, j, 0)) for w in widths + [LANES]],
        out_shape=[jax.ShapeDtypeStruct((b, s, w), BF16) for w in widths]
        + [jax.ShapeDtypeStruct((b, s, LANES), F32)],
        compiler_params=_params(("arbitrary", "arbitrary")),
        name="in_projection",
    )(x, mod, mod, g, w_qkv, w_f, b_f)


AUG_TERMS = 3


def _own_half(shape, h):
    lane = lax.broadcasted_iota(jnp.int32, shape, len(shape) - 1)
    return (lane < HEAD_DIM) if h % 2 == 0 else (lane >= HEAD_DIM)


def _spare_lane(shape, h):
    lane = lax.broadcasted_iota(jnp.int32, shape, len(shape) - 1)
    return lane - (HEAD_DIM if h % 2 == 0 else 0)


def _cumsum_kernel(logf_ref, qaug_ref, kaug_ref, *, nh):
    s = logf_ref.shape[0]
    r = lax.broadcasted_iota(jnp.int32, (LANES, LANES), 0)
    c = lax.broadcasted_iota(jnp.int32, (LANES, LANES), 1)
    tri = (c <= r).astype(F32)
    carry = jnp.zeros((1, LANES), F32)
    for i in range(s // LANES):
        rows = slice(i * LANES, (i + 1) * LANES)
        cs = jnp.dot(tri, logf_ref[rows, :], preferred_element_type=F32, precision=HIGHEST) + carry
        carry = cs[LANES - 1:LANES, :]
        for h in range(nh):
            full = jnp.broadcast_to(cs[:, h:h + 1], (LANES, LANES))
            parts, rest = [], full
            for _ in range(AUG_TERMS):
                piece = rest.astype(BF16).astype(F32)
                parts.append(piece)
                rest = rest - piece
            j = _spare_lane((LANES, LANES), h)
            qa = jnp.where((j >= AUG_TERMS) & (j < 2 * AUG_TERMS), 1.0, 0.0)
            ka = jnp.where((j >= 0) & (j < AUG_TERMS), 1.0, 0.0)
            for n, piece in enumerate(parts):
                qa = jnp.where(j == n, piece, qa)
                ka = jnp.where(j == AUG_TERMS + n, -piece, ka)
            qaug_ref[rows, h * LANES:(h + 1) * LANES] = qa.astype(BF16)
            kaug_ref[rows, h * LANES:(h + 1) * LANES] = ka.astype(BF16)


def forget_cumsum(logf, nh):
    b, s, _ = logf.shape
    out = pl.BlockSpec((None, s, nh * LANES), lambda i: (i, 0, 0))
    return pl.pallas_call(
        functools.partial(_cumsum_kernel, nh=nh),
        grid=(b,),
        in_specs=[pl.BlockSpec((None, s, LANES), lambda i: (i, 0, 0))],
        out_specs=[out, out],
        out_shape=[jax.ShapeDtypeStruct((b, s, nh * LANES), BF16)] * 2,
        compiler_params=_params(("arbitrary",)),
        name="forget_cumsum",
    )(logf)


def _tile_iotas(t):
    return lax.broadcasted_iota(jnp.int32, (t, t), 0), lax.broadcasted_iota(jnp.int32, (t, t), 1)


def _pair_tile(ref, rows, h):
    p = h // 2
    return ref[rows, p * LANES:(p + 1) * LANES]


def _keep_own(tile, h, other=0.0):
    return jnp.where(_own_half(tile.shape, h), tile.astype(F32), other).astype(BF16)


def _qk(q, k):
    return lax.dot_general(q, k, (((1,), (1,)), ((), ())), preferred_element_type=F32)


def _merge_pairs(per_head, o_ref):
    tiles = [jnp.where(_own_half(per_head[h].shape, h), per_head[h], per_head[h + 1])
             for h in range(0, len(per_head), 2)]
    o_ref[...] = jnp.concatenate(tiles, axis=-1).astype(o_ref.dtype)


def _sb_kernel(q_ref, k_ref, v_ref, o_ref, *, t, nh):
    qi = pl.program_id(1)
    row, col = _tile_iotas(t)
    strict = col < row
    upper = (row > col).astype(BF16)
    qs = [_keep_own(_pair_tile(q_ref, slice(None), h), h) for h in range(nh)]

    def tile(ki, carries, diag):
        rows = pl.ds(pl.multiple_of(ki * t, t), t)
        heads = range(nh)
        zs = [_qk(qs[h], _pair_tile(k_ref, rows, h)) for h in heads]
        sps = [_softplus(z) for z in zs]
        log_nots = [jnp.where(strict, -sp, 0.0) if diag else -sp for sp in sps]
        his = [ln.astype(BF16) for ln in log_nots]
        los = [(ln - hi.astype(F32)).astype(BF16) for ln, hi in zip(log_nots, his)]
        betweens = [jnp.dot(hi, upper, preferred_element_type=F32) + jnp.dot(lo, upper, preferred_element_type=F32)
                    for hi, lo in zip(his, los)]
        out = []
        for h in heads:
            suffix, acc = carries[h]
            w = jnp.exp(zs[h] - sps[h] + betweens[h] + suffix)
            if diag:
                w = jnp.where(strict, w, 0.0)
            acc = acc + jnp.dot(w.astype(BF16), _pair_tile(v_ref, rows, h), preferred_element_type=F32)
            out.append((suffix + betweens[h][:, 0:1] + log_nots[h][:, 0:1], acc))
        return tuple(out)

    init = tuple((jnp.zeros((t, 1), F32), jnp.zeros((t, LANES), F32)) for _ in range(nh))
    carries = tile(qi, init, True)
    carries = lax.fori_loop(0, qi, lambda j, cr: tile(qi - 1 - j, cr, False), carries)
    _merge_pairs([cr[1] for cr in carries], o_ref)


def _online_softmax_steps(scores, values, carries):
    m_news = [jnp.maximum(m, jnp.max(s, axis=-1, keepdims=True)) for s, (m, _) in zip(scores, carries)]
    ps = [jnp.exp(s - m_new).astype(BF16) for s, m_new in zip(scores, m_news)]
    return tuple((m_new, jnp.exp(m - m_new) * acc + jnp.dot(p, v, preferred_element_type=F32))
                 for p, v, m_new, (m, acc) in zip(ps, values, m_news, carries))


def _softmax_init(t, nh):
    return tuple((jnp.full((t, 1), NEG_BIG, F32), jnp.zeros((t, LANES), F32)) for _ in range(nh))


def _softmax_finish(carries, o_ref):
    outs = []
    for h, (_, acc) in enumerate(carries):
        denom = jnp.sum(jnp.where(_spare_lane(acc.shape, h) == 0, acc, 0.0), axis=-1, keepdims=True)
        outs.append(acc / denom)
    _merge_pairs(outs, o_ref)


def _pad_values(v_ref, vpad_ref, nh):
    for h in range(nh):
        tile = _pair_tile(v_ref, slice(None), h)
        ones_col = jnp.where(_spare_lane(tile.shape, h) == 0, 1.0, 0.0)
        vpad_ref[:, h * LANES:(h + 1) * LANES] = _keep_own(tile, h, ones_col)


def _fox_kernel(q_ref, k_ref, v_ref, qaug_ref, kaug_ref, o_ref, kpad_ref, vpad_ref, *, t, nh):
    qi = pl.program_id(1)
    row, col = _tile_iotas(t)

    @pl.when(qi == 0)
    def _():
        _pad_values(v_ref, vpad_ref, nh)
        for h in range(nh):
            lanes = slice(h * LANES, (h + 1) * LANES)
            kpad_ref[:, lanes] = _keep_own(_pair_tile(k_ref, slice(None), h), h, kaug_ref[:, lanes].astype(F32))

    qs = [_keep_own(_pair_tile(q_ref, slice(None), h), h, qaug_ref[:, h * LANES:(h + 1) * LANES].astype(F32))
          for h in range(nh)]

    def tile(ki, carries, diag):
        rows = pl.ds(pl.multiple_of(ki * t, t), t)
        lanes = [slice(h * LANES, (h + 1) * LANES) for h in range(nh)]
        scores = [_qk(qs[h], kpad_ref[rows, lanes[h]]) for h in range(nh)]
        if diag:
            scores = [jnp.where(col <= row, s, NEG_BIG) for s in scores]
        return _online_softmax_steps(scores, [vpad_ref[rows, ln] for ln in lanes], carries)

    carries = lax.fori_loop(0, qi, lambda ki, cr: tile(ki, cr, False), _softmax_init(t, nh))
    _softmax_finish(tile(qi, carries, True), o_ref)


DIL_NEAR_TILES = 3


def _dilation_log_count(d):
    cnt = None
    for window, dil in DIL_PATTERNS:
        hit = (d >= 0) & (d <= window) & ((d & (dil - 1)) == 0)
        term = jnp.where(hit, 1.0, 0.0)
        cnt = term if cnt is None else cnt + term
    return jnp.where(cnt > 0.0, jnp.log(jnp.maximum(cnt, 1.0)), NEG_BIG)


def _dil_kernel(q_ref, k_ref, v_ref, cos_ref, sin_ref, o_ref, kr_ref, vpad_ref, bias_ref, *, t, nh):
    qi = pl.program_id(1)

    @pl.when(qi == 0)
    def _():
        _pad_values(v_ref, vpad_ref, nh)
        for p in range(nh // 2):
            lanes = slice(p * LANES, (p + 1) * LANES)
            kr_ref[:, lanes] = _rope(k_ref[:, lanes].astype(F32), cos_ref[...], sin_ref[...]).astype(BF16)
        row, col = _tile_iotas(t)
        for delta in range(DIL_NEAR_TILES + 1):
            bias_ref[delta] = _dilation_log_count(delta * t + row - col)

    qrows = pl.ds(pl.multiple_of(qi * t, t), t)
    cos, sin = cos_ref[qrows, :], sin_ref[qrows, :]
    qs = [_keep_own(_rope(_pair_tile(q_ref, slice(None), h).astype(F32), cos, sin), h) for h in range(nh)]

    def tile(ki, carries):
        rows = pl.ds(pl.multiple_of(ki * t, t), t)
        bias = bias_ref[jnp.minimum(qi - ki, DIL_NEAR_TILES)]
        scores = [_qk(qs[h], _pair_tile(kr_ref, rows, h)) + bias for h in range(nh)]
        values = [vpad_ref[rows, h * LANES:(h + 1) * LANES] for h in range(nh)]
        return _online_softmax_steps(scores, values, carries)

    _softmax_finish(lax.fori_loop(0, qi + 1, tile, _softmax_init(t, nh)), o_ref)


def _attention_call(kernel, qkv, n_heads, extra_inputs, extra_specs, scratch, name):
    b, s, w3 = qkv.shape
    w = w3 // 3
    t = min(ATTN_TILE, s)
    return pl.pallas_call(
        functools.partial(kernel, t=t, nh=n_heads),
        grid=(b, s // t),
        in_specs=[pl.BlockSpec((None, t, w), lambda i, j: (i, j, 0)),
                  pl.BlockSpec((None, s, w), lambda i, j: (i, 0, 1)),
                  pl.BlockSpec((None, s, w), lambda i, j: (i, 0, 2))] + extra_specs,
        out_specs=pl.BlockSpec((None, t, w), lambda i, j: (i, j, 0)),
        out_shape=jax.ShapeDtypeStruct((b, s, w), BF16),
        scratch_shapes=scratch,
        compiler_params=_params(("arbitrary", "arbitrary")),
        name=name,
    )(qkv, qkv, qkv, *extra_inputs)


def sb_attention(qkv):
    return _attention_call(_sb_kernel, qkv, N_HEADS_SB, [], [], [], "sb_attention")


def dil_attention(qkv, cos, sin):
    s = qkv.shape[1]
    t = min(ATTN_TILE, s)
    widest_window, widest_dil = DIL_PATTERNS[-1]
    assert s - 1 <= widest_window and t % widest_dil == 0
    assert all(window < (DIL_NEAR_TILES - 1) * t + 1 for window, _ in DIL_PATTERNS[:-1])
    tab = pl.BlockSpec((None, s, LANES), lambda i, j: (i, 0, 0))
    scratch = [pltpu.VMEM((s, qkv.shape[-1] // 3), BF16), pltpu.VMEM((s, N_HEADS_DIL * LANES), BF16),
               pltpu.VMEM((DIL_NEAR_TILES + 1, t, t), F32)]
    return _attention_call(_dil_kernel, qkv, N_HEADS_DIL, [cos, sin], [tab, tab], scratch, "dil_attention")


def fox_attention(qkv, qaug, kaug):
    s = qkv.shape[1]
    t = min(ATTN_TILE, s)
    wide = N_HEADS_FOX * LANES
    specs = [pl.BlockSpec((None, t, wide), lambda i, j: (i, j, 0)),
             pl.BlockSpec((None, s, wide), lambda i, j: (i, 0, 0))]
    scratch = [pltpu.VMEM((s, wide), BF16), pltpu.VMEM((s, wide), BF16)]
    return _attention_call(_fox_kernel, qkv, N_HEADS_FOX, [qaug, kaug], specs, scratch, "fox_attention")


def _outproj_router_kernel(osb_ref, odil_ref, ofox_ref, gmix_ref, wout_ref, x_ref, ga_ref, sc_ref, sh_ref,
                           gffn_ref, wr_ref, br_ref,
                           xo_ref, h2_ref, pos_ref, post_ref, gate_ref, start_ref, len_ref, base_ref, total_ref,
                           carry_ref):
    first = (pl.program_id(0) == 0) & (pl.program_id(1) == 0)

    @pl.when(first)
    def _():
        carry_ref[...] = jnp.zeros_like(carry_ref)

    a = None
    lo = 0
    for o_ref in (osb_ref, odil_ref, ofox_ref):
        w = o_ref.shape[-1]
        on = (_rms_rows(o_ref[...].astype(F32)) * gmix_ref[:, lo:lo + w]).astype(BF16)
        part = jnp.dot(on, wout_ref[lo:lo + w, :], preferred_element_type=F32)
        a = part if a is None else a + part
        lo += w
    xn = x_ref[...] + ga_ref[...] * a
    xo_ref[...] = xn
    h2 = _rms_rows(xn) * gffn_ref[...] * (1.0 + sc_ref[...]) + sh_ref[...]
    h2_ref[...] = h2

    logits = jnp.dot(h2, wr_ref[...], preferred_element_type=F32, precision=HIGHEST) + br_ref[...]
    tm = logits.shape[0]
    lane = lax.broadcasted_iota(jnp.int32, (tm, LANES), 1)
    vals, idxs = [], []
    rest = logits
    for _ in range(TOP_K):
        m = jnp.max(rest, axis=-1, keepdims=True)
        ik = jnp.min(jnp.where(rest == m, lane, LANES), axis=-1, keepdims=True)
        vals.append(m)
        idxs.append(ik)
        rest = jnp.where(lane == ik, -jnp.inf, rest)
    es = [jnp.exp(v - vals[0]) for v in vals]
    den = es[0] + es[1] + es[2] + es[3]

    hot = [(lane == ik) for ik in idxs]
    multi = jnp.where(hot[0] | hot[1] | hot[2] | hot[3], 1.0, 0.0)
    r = lax.broadcasted_iota(jnp.int32, (tm, tm), 0)
    c = lax.broadcasted_iota(jnp.int32, (tm, tm), 1)
    before = (c < r).astype(BF16)
    prior = jnp.dot(before, multi.astype(BF16), preferred_element_type=F32)
    counts = (prior[tm - 1:tm, :] + multi[tm - 1:tm, :]).astype(jnp.int32)
    group_len = (counts + (GROUP_ALIGN - 1)) // GROUP_ALIGN * GROUP_ALIGN
    er = lax.broadcasted_iota(jnp.int32, (LANES, LANES), 0)
    ec = lax.broadcasted_iota(jnp.int32, (LANES, LANES), 1)
    lens8 = jnp.broadcast_to(group_len.astype(F32), (8, LANES)).astype(BF16)
    group_start = jnp.dot(lens8, (er < ec).astype(BF16), preferred_element_type=F32)[0:1, :]
    where_in_tile = group_start + prior
    pos_out = jnp.zeros((tm, LANES), F32)
    gate_out = jnp.zeros((tm, LANES), F32)
    for k in range(TOP_K):
        pk = jnp.sum(jnp.where(hot[k], where_in_tile, 0.0), axis=-1, keepdims=True)
        pos_out = jnp.where(lane == k, pk, pos_out)
        gate_out = jnp.where(lane == k, es[k] / den, gate_out)
    pos_ref[...] = pos_out.astype(jnp.int32)
    post_ref[...] = pos_out.T[0:8, :].astype(jnp.int32)
    gate_ref[...] = gate_out
    start_ref[...] = group_start.astype(jnp.int32)
    len_ref[...] = group_len
    base_ref[...] = carry_ref[...]
    carry_ref[...] = carry_ref[...] + group_len
    total_ref[...] = carry_ref[...]


def outproj_router(o_sb, o_dil, o_fox, g_mix, w_out, x, mod, g_ffn, w_r, b_r):
    b, s, d = x.shape
    tm = min(ROW_TILE, s)
    n = b * s

    def act(w):
        return pl.BlockSpec((None, tm, w), lambda i, j: (i, j, 0))

    def modspec(col):
        return pl.BlockSpec((None, 1, d), lambda i, j: (i, 0, col))

    def const(shape):
        return pl.BlockSpec(shape, lambda i, j: (0, 0))

    per_b = s // tm
    tok = pl.BlockSpec((tm, LANES), lambda i, j: (i * per_b + j, 0))
    per_tile = pl.BlockSpec((None, 1, LANES), lambda i, j: (i * per_b + j, 0, 0))
    ntiles = b * per_b
    tile_tab = jax.ShapeDtypeStruct((ntiles, 1, LANES), jnp.int32)
    return pl.pallas_call(
        _outproj_router_kernel,
        grid=(b, per_b),
        in_specs=[act(o_sb.shape[-1]), act(o_dil.shape[-1]), act(o_fox.shape[-1]),
                  const((1, d)), const(w_out.shape), act(d),
                  modspec(2), modspec(4), modspec(3),
                  const((1, d)), const((d, LANES)), const((1, LANES))],
        out_specs=[act(d), pl.BlockSpec((tm, d), lambda i, j: (i * per_b + j, 0)), tok,
                   pl.BlockSpec((None, 8, tm), lambda i, j: (i * per_b + j, 0, 0)), tok,
                   per_tile, per_tile, per_tile, const((1, LANES))],
        out_shape=[jax.ShapeDtypeStruct((b, s, d), F32), jax.ShapeDtypeStruct((n, d), F32),
                   jax.ShapeDtypeStruct((n, LANES), jnp.int32), jax.ShapeDtypeStruct((ntiles, 8, tm), jnp.int32),
                   jax.ShapeDtypeStruct((n, LANES), F32), tile_tab, tile_tab, tile_tab,
                   jax.ShapeDtypeStruct((1, LANES), jnp.int32)],
        scratch_shapes=[pltpu.VMEM((1, LANES), jnp.int32)],
        compiler_params=_params(("arbitrary", "arbitrary")),
        name="outproj_router",
    )(o_sb, o_dil, o_fox, g_mix, w_out, x, mod, mod, mod, g_ffn, w_r, b_r)


GROUP_ALIGN = 8
GROUP_CHUNKS = (512, 256, 128, 64, 32, 16, 8)


def sorted_rows(tm, n_experts):
    return -(-(TOP_K * tm + n_experts * (GROUP_ALIGN - 1)) // 256) * 256


def _for_each_group_chunk(tabs, ne, fn):
    start_ref, len_ref, dst_ref = tabs
    first = pl.program_id(0) * ne

    def body(e, _):
        length = len_ref[first + e]
        buf_row = start_ref[first + e]
        hbm_row = dst_ref[first + e]
        for size in GROUP_CHUNKS:
            hit = (length & size) != 0

            @pl.when(hit)
            def _(buf_row=buf_row, hbm_row=hbm_row, size=size):
                fn(size, pl.multiple_of(buf_row, GROUP_ALIGN), pl.multiple_of(hbm_row, GROUP_ALIGN))

            step = jnp.where(hit, size, 0)
            buf_row = buf_row + step
            hbm_row = hbm_row + step
        return 0

    lax.fori_loop(0, ne, body, 0)


def _dispatch_kernel(start_ref, len_ref, dst_ref, h_ref, post_ref, zeros_hbm, xs_hbm, buf_ref, sem, *, ne):
    del zeros_hbm
    rows, tm = buf_ref.shape[0], h_ref.shape[0]
    r = lax.broadcasted_iota(jnp.int32, (rows, tm), 0)
    pick = r == post_ref[0:1, :]
    for k in range(1, TOP_K):
        pick = pick | (r == post_ref[k:k + 1, :])
    buf_ref[...] = jnp.dot(jnp.where(pick, 1.0, 0.0).astype(BF16), h_ref[...].astype(BF16),
                           preferred_element_type=F32)
    tabs = (start_ref, len_ref, dst_ref)

    def copy(size, buf_row, hbm_row):
        return pltpu.make_async_copy(buf_ref.at[pl.ds(buf_row, size)], xs_hbm.at[pl.ds(hbm_row, size)], sem)

    _for_each_group_chunk(tabs, ne, lambda *a: copy(*a).start())
    _for_each_group_chunk(tabs, ne, lambda *a: copy(*a).wait())


def dispatch_rows(tabs, h, post, cap, ne):
    n, d = h.shape
    tm = post.shape[-1]
    zeros = jnp.zeros((cap, d), F32)
    return pl.pallas_call(
        functools.partial(_dispatch_kernel, ne=ne),
        grid_spec=pltpu.PrefetchScalarGridSpec(
            num_scalar_prefetch=3,
            grid=(n // tm,),
            in_specs=[pl.BlockSpec((tm, d), lambda i, *_: (i, 0)),
                      pl.BlockSpec((None, 8, tm), lambda i, *_: (i, 0, 0)),
                      pl.BlockSpec(memory_space=pl.ANY)],
            out_specs=pl.BlockSpec(memory_space=pl.ANY),
            scratch_shapes=[pltpu.VMEM((sorted_rows(tm, ne), d), F32), pltpu.SemaphoreType.DMA(())]),
        out_shape=jax.ShapeDtypeStruct((cap, d), F32),
        input_output_aliases={5: 0},
        compiler_params=pltpu.CompilerParams(dimension_semantics=("arbitrary",), has_side_effects=True,
                                             vmem_limit_bytes=VMEM_LIMIT),
        name="dispatch_rows",
    )(*tabs, h, post, zeros)


def _expert_kernel(blk_e_ref, nused_ref, xs_ref, wgu_ref, bgu_ref, wdn_ref, bdn_ref, o_ref):
    del blk_e_ref
    used = pl.program_id(0) < nused_ref[0]

    @pl.when(jnp.logical_not(used))
    def _():
        o_ref[...] = jnp.zeros_like(o_ref)

    @pl.when(used)
    def _():
        f = wdn_ref.shape[0]
        gu = jnp.dot(xs_ref[...].astype(BF16), wgu_ref[...], preferred_element_type=F32) + bgu_ref[...]
        gate = jnp.minimum(gu[:, :f], SWIGLU_LIMIT)
        up = jnp.clip(gu[:, f:], -SWIGLU_LIMIT, SWIGLU_LIMIT)
        act = (up + 1.0) * (gate * jax.nn.sigmoid(SWIGLU_ALPHA * gate))
        o_ref[...] = jnp.dot(act.astype(BF16), wdn_ref[...], preferred_element_type=F32) + bdn_ref[...]


def expert_ffn(blk_e, nused, xs, w_gu, b_gu, w_dn, b_dn, te):
    cap, d = xs.shape
    ne, _, f2 = w_gu.shape
    f = f2 // 2
    nblk = cap // te

    def rows(i, be, nu):
        return (jnp.minimum(i, nu[0] - 1), 0)

    def per_expert(i, be, nu):
        return (be[i], 0, 0)

    return pl.pallas_call(
        _expert_kernel,
        grid_spec=pltpu.PrefetchScalarGridSpec(
            num_scalar_prefetch=2,
            grid=(nblk,),
            in_specs=[pl.BlockSpec((te, d), rows),
                      pl.BlockSpec((None, d, f2), per_expert),
                      pl.BlockSpec((None, 1, f2), per_expert),
                      pl.BlockSpec((None, f, d), per_expert),
                      pl.BlockSpec((None, 1, d), per_expert)],
            out_specs=pl.BlockSpec((te, d), lambda i, be, nu: (i, 0))),
        out_shape=jax.ShapeDtypeStruct((cap, d), F32),
        compiler_params=_params(("arbitrary",)),
        name="expert_ffn",
    )(blk_e, nused, xs, w_gu, b_gu.reshape(ne, 1, f2), w_dn, b_dn.reshape(ne, 1, d))


def _combine_kernel(start_ref, len_ref, dst_ref, ys_hbm, pos_ref, gate_ref, x_ref, ga_ref, xo_ref, buf_ref, sem,
                    *, ne):
    @pl.when(pl.program_id(0) == 0)
    def _():
        buf_ref[...] = jnp.zeros_like(buf_ref)

    tabs = (start_ref, len_ref, dst_ref)

    def copy(size, buf_row, hbm_row):
        return pltpu.make_async_copy(ys_hbm.at[pl.ds(hbm_row, size)], buf_ref.at[pl.ds(buf_row, size)], sem)

    _for_each_group_chunk(tabs, ne, lambda *a: copy(*a).start())
    rows, tm = buf_ref.shape[0], pos_ref.shape[0]
    r = lax.broadcasted_iota(jnp.int32, (tm, rows), 1)
    pos, g = pos_ref[...], gate_ref[...]
    weights = jnp.zeros((tm, rows), F32)
    for k in range(TOP_K):
        weights = jnp.where(r == pos[:, k:k + 1], g[:, k:k + 1], weights)
    _for_each_group_chunk(tabs, ne, lambda *a: copy(*a).wait())
    y = jnp.dot(weights.astype(BF16), buf_ref[...].astype(BF16), preferred_element_type=F32)
    xo_ref[...] = x_ref[...] + ga_ref[...] * y


def combine_rows(tabs, ys, pos, gates, x, mod, ne):
    b, s, d = x.shape
    tm = min(ROW_TILE, s)
    per_b = s // tm
    return pl.pallas_call(
        functools.partial(_combine_kernel, ne=ne),
        grid_spec=pltpu.PrefetchScalarGridSpec(
            num_scalar_prefetch=3,
            grid=(b * per_b,),
            in_specs=[pl.BlockSpec(memory_space=pl.ANY),
                      pl.BlockSpec((tm, LANES), lambda i, *_: (i, 0)),
                      pl.BlockSpec((tm, LANES), lambda i, *_: (i, 0)),
                      pl.BlockSpec((None, tm, d), lambda i, *_: (i // per_b, i % per_b, 0)),
                      pl.BlockSpec((None, 1, d), lambda i, *_: (i // per_b, 0, 5))],
            out_specs=pl.BlockSpec((None, tm, d), lambda i, *_: (i // per_b, i % per_b, 0)),
            scratch_shapes=[pltpu.VMEM((sorted_rows(tm, ne), d), F32), pltpu.SemaphoreType.DMA(())]),
        out_shape=jax.ShapeDtypeStruct((b, s, d), F32),
        compiler_params=_params(("arbitrary",)),
        name="combine_rows",
    )(*tabs, ys, pos, gates, x, mod)


def _final_norm_kernel(x_ref, g_ref, o_ref):
    o_ref[...] = _rms_rows(x_ref[...]) * g_ref[...]


def final_norm(x, g):
    b, s, d = x.shape
    tm = min(ROW_TILE, s)
    spec = pl.BlockSpec((None, tm, d), lambda i, j: (i, j, 0))
    return pl.pallas_call(
        _final_norm_kernel,
        grid=(b, s // tm),
        in_specs=[spec, pl.BlockSpec((1, d), lambda i, j: (0, 0))],
        out_specs=spec,
        out_shape=jax.ShapeDtypeStruct((b, s, d), F32),
        compiler_params=_params(("arbitrary", "arbitrary")),
        name="final_norm",
    )(x, g)


def _routing_tables(starts, lens, bases, totals, n_experts, te, nblk):
    totals = totals[0, :n_experts]
    padded = (totals + te - 1) // te * te
    ends = jnp.cumsum(padded)
    first_row = ends - padded
    dst = (first_row[None, :] + bases[:, 0, :n_experts]).reshape(-1).astype(jnp.int32)
    tabs = (starts[:, 0, :n_experts].reshape(-1), lens[:, 0, :n_experts].reshape(-1), dst)
    nused = (ends[-1] // te).astype(jnp.int32)
    blk = jnp.arange(nblk, dtype=jnp.int32)
    blk_e = jnp.sum((ends[None, :] <= (blk * te)[:, None]).astype(jnp.int32), axis=1)
    blk_e = jnp.minimum(blk_e, n_experts - 1)
    blk_e = jnp.where(blk < nused, blk_e, blk_e[nused - 1])
    return tabs, blk_e, nused.reshape(1)


def _pad_lanes(a, fill=0.0):
    return jnp.pad(a, [(0, 0)] * (a.ndim - 1) + [(0, LANES - a.shape[-1])], constant_values=fill)


def kernel(x, c, positions, w_mod, b_mod, g_attn, w_in, b_forget, g_mix, w_out, g_ffn, w_router, b_router,
           w_gate_up, b_gate_up, w_down, b_down, g_final):
    b, s, d = x.shape
    depth = w_mod.shape[0]
    n_experts = w_router.shape[-1]
    d_mix = w_out.shape[1]
    n = b * s
    te = min(EXPERT_TILE, n)
    ntiles = n // min(ROW_TILE, s)
    nblk = -(-(n * TOP_K + ntiles * n_experts * (GROUP_ALIGN - 1) + n_experts * (te - 1)) // te)
    cap = nblk * te

    mod_all = modulation(c, w_mod, b_mod)
    cos, sin = rope_tables(positions)
    for layer in range(depth):
        mod = mod_all[layer].reshape(b, 1, N_MOD * d)
        w_qkv = family_major_qkv_weight(w_in[layer], d_mix)
        w_f = _pad_lanes(w_in[layer, :, 3 * d_mix:]).astype(BF16)
        b_f = _pad_lanes(b_forget[layer][None, :])
        qkv_sb, qkv_dil, qkv_fox, logf = in_projection(x, mod, g_attn[layer][None, :], w_qkv, w_f, b_f)
        qaug, kaug = forget_cumsum(logf, N_HEADS_FOX)
        o_sb = sb_attention(qkv_sb)
        o_dil = dil_attention(qkv_dil, cos, sin)
        o_fox = fox_attention(qkv_fox, qaug, kaug)
        x, h2, pos, post, gates, starts, lens, bases, totals = outproj_router(
            o_sb, o_dil, o_fox, g_mix[layer][None, :], w_out[layer].astype(BF16), x, mod, g_ffn[layer][None, :],
            _pad_lanes(w_router[layer]), _pad_lanes(b_router[layer][None, :], NEG_BIG))
        tabs, blk_e, nused = _routing_tables(starts, lens, bases, totals, n_experts, te, nblk)
        xs = dispatch_rows(tabs, h2, post, cap, n_experts)
        ys = expert_ffn(blk_e, nused, xs, w_gate_up[layer].astype(BF16), b_gate_up[layer],
                        w_down[layer].astype(BF16), b_down[layer], te)
        x = combine_rows(tabs, ys, pos, gates, x, mod, n_experts)
    return final_norm(x, g_final[None, :])
```

```python
import functools

import numpy as np
import jax
import jax.numpy as jnp
from jax import lax
from jax.experimental import pallas as pl
from jax.experimental.pallas import tpu as pltpu

F32 = jnp.float32
BF16 = jnp.bfloat16
HIGHEST = lax.Precision.HIGHEST

HEAD_DIM = 64
LANES = 128
N_HEADS_SB = 4
N_HEADS_DIL = 6
N_HEADS_FOX = 6
DIL_PATTERNS = ((128, 1), (512, 4), (2048, 16))
ROPE_THETA = 10000.0
TOP_K = 4
SWIGLU_LIMIT = 7.0
SWIGLU_ALPHA = 1.702
N_MOD = 6
EPS = 1e-6
NEG_BIG = -1e30
ATTN_TILE = 256
ROW_TILE = 512
EXPERT_TILE = 512
VMEM_LIMIT = 48 * 1024 * 1024


def _params(sem, vmem=VMEM_LIMIT):
    return pltpu.CompilerParams(dimension_semantics=sem, vmem_limit_bytes=vmem)


def _rms_rows(x):
    return x * lax.rsqrt(jnp.mean(x * x, axis=-1, keepdims=True) + EPS)


def _softplus(z):
    return jnp.maximum(z, 0.0) + jnp.log(1.0 + jnp.exp(-jnp.abs(z)))


def _mod_kernel(c_ref, w_ref, b_ref, o_ref):
    c = c_ref[...]
    ca = c * jax.nn.sigmoid(c)
    o_ref[...] = jnp.dot(ca, w_ref[...], preferred_element_type=F32, precision=HIGHEST) + b_ref[...]


def modulation(c, w_mod, b_mod):
    depth, d, n6 = w_mod.shape
    b = c.shape[0]
    tn = min(n6, 1536)
    return pl.pallas_call(
        _mod_kernel,
        grid=(depth, n6 // tn),
        in_specs=[pl.BlockSpec((b, d), lambda l, j: (0, 0)),
                  pl.BlockSpec((None, d, tn), lambda l, j: (l, 0, j)),
                  pl.BlockSpec((None, 1, tn), lambda l, j: (l, 0, j))],
        out_specs=pl.BlockSpec((None, b, tn), lambda l, j: (l, 0, j)),
        out_shape=jax.ShapeDtypeStruct((depth, b, n6), F32),
        compiler_params=_params(("arbitrary", "arbitrary")),
        name="modulation",
    )(c, w_mod, b_mod.reshape(depth, 1, n6))


def _rope_table_kernel(pos_ref, invf_ref, cos_ref, sin_ref):
    ang = pos_ref[...].astype(F32) * invf_ref[...]
    lane = lax.broadcasted_iota(jnp.int32, ang.shape, 1)
    first_half = (lane % HEAD_DIM) < (HEAD_DIM // 2)
    s = jnp.sin(ang)
    cos_ref[...] = jnp.cos(ang)
    sin_ref[...] = jnp.where(first_half, -s, s)


def rope_tables(positions):
    b, s = positions.shape
    half = HEAD_DIM // 2
    inv_freq = ROPE_THETA ** (-np.arange(0, HEAD_DIM, 2, dtype=np.float64) / HEAD_DIM)
    invf = jnp.asarray(np.tile(inv_freq, LANES // half)[None, :], F32)
    return pl.pallas_call(
        _rope_table_kernel,
        grid=(b,),
        in_specs=[pl.BlockSpec((None, s, 1), lambda i: (i, 0, 0)),
                  pl.BlockSpec((1, LANES), lambda i: (0, 0))],
        out_specs=[pl.BlockSpec((None, s, LANES), lambda i: (i, 0, 0))] * 2,
        out_shape=[jax.ShapeDtypeStruct((b, s, LANES), F32)] * 2,
        compiler_params=_params(("arbitrary",)),
        name="rope_tables",
    )(positions.reshape(b, s, 1), invf)


def _rope(x, cos, sin_signed):
    lane = lax.broadcasted_iota(jnp.int32, x.shape, 1)
    first_half = (lane % HEAD_DIM) < (HEAD_DIM // 2)
    half = HEAD_DIM // 2
    partner = jnp.where(first_half, pltpu.roll(x, LANES - half, 1), pltpu.roll(x, half, 1))
    return x * cos + partner * sin_signed


FAMILY_HEADS = (N_HEADS_SB, N_HEADS_DIL, N_HEADS_FOX)


def family_major_qkv_weight(w_in_layer, d_mix):
    wq = w_in_layer[:, :d_mix] * (HEAD_DIM ** -0.5)
    wk = w_in_layer[:, d_mix:2 * d_mix]
    wv = w_in_layer[:, 2 * d_mix:3 * d_mix]
    cols, lo = [], 0
    for nh in FAMILY_HEADS:
        hi = lo + nh * HEAD_DIM
        cols += [wq[:, lo:hi], wk[:, lo:hi], wv[:, lo:hi]]
        lo = hi
    return jnp.concatenate(cols, axis=1).astype(BF16)


def _inproj_kernel(x_ref, sh_ref, sc_ref, g_ref, wqkv_ref, wf_ref, bf_ref, sb_ref, dil_ref, fox_ref, logf_ref):
    x = x_ref[...]
    h = _rms_rows(x) * g_ref[...] * (1.0 + sc_ref[...]) + sh_ref[...]
    hb = h.astype(BF16)
    lo = 0
    for o_ref in (sb_ref, dil_ref, fox_ref):
        w = o_ref.shape[-1]
        o_ref[...] = jnp.dot(hb, wqkv_ref[:, lo:lo + w], preferred_element_type=F32).astype(BF16)
        lo += w
    f = jnp.dot(hb, wf_ref[...], preferred_element_type=F32) + bf_ref[...]
    logf_ref[...] = -_softplus(-f)


def in_projection(x, mod, g, w_qkv, w_f, b_f):
    b, s, d = x.shape
    d_mix = w_qkv.shape[1] // 3
    tm = min(ROW_TILE, s)
    widths = [3 * nh * HEAD_DIM for nh in FAMILY_HEADS]
    return pl.pallas_call(
        _inproj_kernel,
        grid=(b, s // tm),
        in_specs=[pl.BlockSpec((None, tm, d), lambda i, j: (i, j, 0)),
                  pl.BlockSpec((None, 1, d), lambda i, j: (i, 0, 0)),
                  pl.BlockSpec((None, 1, d), lambda i, j: (i, 0, 1)),
                  pl.BlockSpec((1, d), lambda i, j: (0, 0)),
                  pl.BlockSpec((d, 3 * d_mix), lambda i, j: (0, 0)),
                  pl.BlockSpec((d, LANES), lambda i, j: (0, 0)),
                  pl.BlockSpec((1, LANES), lambda i, j: (0, 0))],
        out_specs=[pl.BlockSpec((None, tm, w), lambda i, j: (i, j, 0)) for w in widths + [LANES]],
        out_shape=[jax.ShapeDtypeStruct((b, s, w), BF16) for w in widths]
        + [jax.ShapeDtypeStruct((b, s, LANES), F32)],
        compiler_params=_params(("arbitrary", "arbitrary")),
        name="in_projection",
    )(x, mod, mod, g, w_qkv, w_f, b_f)


AUG_TERMS = 3


def _own_half(shape, h):
    lane = lax.broadcasted_iota(jnp.int32, shape, len(shape) - 1)
    return (lane < HEAD_DIM) if h % 2 == 0 else (lane >= HEAD_DIM)


def _spare_lane(shape, h):
    lane = lax.broadcasted_iota(jnp.int32, shape, len(shape) - 1)
    return lane - (HEAD_DIM if h % 2 == 0 else 0)


def _cumsum_kernel(logf_ref, qaug_ref, kaug_ref, *, nh):
    s = logf_ref.shape[0]
    r = lax.broadcasted_iota(jnp.int32, (LANES, LANES), 0)
    c = lax.broadcasted_iota(jnp.int32, (LANES, LANES), 1)
    tri = (c <= r).astype(F32)
    carry = jnp.zeros((1, LANES), F32)
    for i in range(s // LANES):
        rows = slice(i * LANES, (i + 1) * LANES)
        cs = jnp.dot(tri, logf_ref[rows, :], preferred_element_type=F32, precision=HIGHEST) + carry
        carry = cs[LANES - 1:LANES, :]
        for h in range(nh):
            full = jnp.broadcast_to(cs[:, h:h + 1], (LANES, LANES))
            parts, rest = [], full
            for _ in range(AUG_TERMS):
                piece = rest.astype(BF16).astype(F32)
                parts.append(piece)
                rest = rest - piece
            j = _spare_lane((LANES, LANES), h)
            qa = jnp.where((j >= AUG_TERMS) & (j < 2 * AUG_TERMS), 1.0, 0.0)
            ka = jnp.where((j >= 0) & (j < AUG_TERMS), 1.0, 0.0)
            for n, piece in enumerate(parts):
                qa = jnp.where(j == n, piece, qa)
                ka = jnp.where(j == AUG_TERMS + n, -piece, ka)
            qaug_ref[rows, h * LANES:(h + 1) * LANES] = qa.astype(BF16)
            kaug_ref[rows, h * LANES:(h + 1) * LANES] = ka.astype(BF16)


def forget_cumsum(logf, nh):
    b, s, _ = logf.shape
    out = pl.BlockSpec((None, s, nh * LANES), lambda i: (i, 0, 0))
    return pl.pallas_call(
        functools.partial(_cumsum_kernel, nh=nh),
        grid=(b,),
        in_specs=[pl.BlockSpec((None, s, LANES), lambda i: (i, 0, 0))],
        out_specs=[out, out],
        out_shape=[jax.ShapeDtypeStruct((b, s, nh * LANES), BF16)] * 2,
        compiler_params=_params(("arbitrary",)),
        name="forget_cumsum",
    )(logf)


def _tile_iotas(t):
    return lax.broadcasted_iota(jnp.int32, (t, t), 0), lax.broadcasted_iota(jnp.int32, (t, t), 1)


def _pair_tile(ref, rows, h):
    p = h // 2
    return ref[rows, p * LANES:(p + 1) * LANES]


def _keep_own(tile, h, other=0.0):
    return jnp.where(_own_half(tile.shape, h), tile.astype(F32), other).astype(BF16)


def _qk(q, k):
    return lax.dot_general(q, k, (((1,), (1,)), ((), ())), preferred_element_type=F32)


def _merge_pairs(per_head, o_ref):
    tiles = [jnp.where(_own_half(per_head[h].shape, h), per_head[h], per_head[h + 1])
             for h in range(0, len(per_head), 2)]
    o_ref[...] = jnp.concatenate(tiles, axis=-1).astype(o_ref.dtype)


def _sb_kernel(q_ref, k_ref, v_ref, o_ref, *, t, nh):
    qi = pl.program_id(1)
    row, col = _tile_iotas(t)
    strict = col < row
    upper = (row > col).astype(BF16)
    qs = [_keep_own(_pair_tile(q_ref, slice(None), h), h) for h in range(nh)]

    def tile(ki, carries, diag):
        rows = pl.ds(pl.multiple_of(ki * t, t), t)
        heads = range(nh)
        zs = [_qk(qs[h], _pair_tile(k_ref, rows, h)) for h in heads]
        sps = [_softplus(z) for z in zs]
        log_nots = [jnp.where(strict, -sp, 0.0) if diag else -sp for sp in sps]
        his = [ln.astype(BF16) for ln in log_nots]
        los = [(ln - hi.astype(F32)).astype(BF16) for ln, hi in zip(log_nots, his)]
        betweens = [jnp.dot(hi, upper, preferred_element_type=F32) + jnp.dot(lo, upper, preferred_element_type=F32)
                    for hi, lo in zip(his, los)]
        out = []
        for h in heads:
            suffix, acc = carries[h]
            w = jnp.exp(zs[h] - sps[h] + betweens[h] + suffix)
            if diag:
                w = jnp.where(strict, w, 0.0)
            acc = acc + jnp.dot(w.astype(BF16), _pair_tile(v_ref, rows, h), preferred_element_type=F32)
            out.append((suffix + betweens[h][:, 0:1] + log_nots[h][:, 0:1], acc))
        return tuple(out)

    init = tuple((jnp.zeros((t, 1), F32), jnp.zeros((t, LANES), F32)) for _ in range(nh))
    carries = tile(qi, init, True)
    carries = lax.fori_loop(0, qi, lambda j, cr: tile(qi - 1 - j, cr, False), carries)
    _merge_pairs([cr[1] for cr in carries], o_ref)


def _online_softmax_steps(scores, values, carries):
    m_news = [jnp.maximum(m, jnp.max(s, axis=-1, keepdims=True)) for s, (m, _) in zip(scores, carries)]
    ps = [jnp.exp(s - m_new).astype(BF16) for s, m_new in zip(scores, m_news)]
    return tuple((m_new, jnp.exp(m - m_new) * acc + jnp.dot(p, v, preferred_element_type=F32))
                 for p, v, m_new, (m, acc) in zip(ps, values, m_news, carries))


def _softmax_init(t, nh):
    return tuple((jnp.full((t, 1), NEG_BIG, F32), jnp.zeros((t, LANES), F32)) for _ in range(nh))


def _softmax_finish(carries, o_ref):
    outs = []
    for h, (_, acc) in enumerate(carries):
        denom = jnp.sum(jnp.where(_spare_lane(acc.shape, h) == 0, acc, 0.0), axis=-1, keepdims=True)
        outs.append(acc / denom)
    _merge_pairs(outs, o_ref)


def _pad_values(v_ref, vpad_ref, nh):
    for h in range(nh):
        tile = _pair_tile(v_ref, slice(None), h)
        ones_col = jnp.where(_spare_lane(tile.shape, h) == 0, 1.0, 0.0)
        vpad_ref[:, h * LANES:(h + 1) * LANES] = _keep_own(tile, h, ones_col)


def _fox_kernel(q_ref, k_ref, v_ref, qaug_ref, kaug_ref, o_ref, kpad_ref, vpad_ref, *, t, nh):
    qi = pl.program_id(1)
    row, col = _tile_iotas(t)

    @pl.when(qi == 0)
    def _():
        _pad_values(v_ref, vpad_ref, nh)
        for h in range(nh):
            lanes = slice(h * LANES, (h + 1) * LANES)
            kpad_ref[:, lanes] = _keep_own(_pair_tile(k_ref, slice(None), h), h, kaug_ref[:, lanes].astype(F32))

    qs = [_keep_own(_pair_tile(q_ref, slice(None), h), h, qaug_ref[:, h * LANES:(h + 1) * LANES].astype(F32))
          for h in range(nh)]

    def tile(ki, carries, diag):
        rows = pl.ds(pl.multiple_of(ki * t, t), t)
        lanes = [slice(h * LANES, (h + 1) * LANES) for h in range(nh)]
        scores = [_qk(qs[h], kpad_ref[rows, lanes[h]]) for h in range(nh)]
        if diag:
            scores = [jnp.where(col <= row, s, NEG_BIG) for s in scores]
        return _online_softmax_steps(scores, [vpad_ref[rows, ln] for ln in lanes], carries)

    carries = lax.fori_loop(0, qi, lambda ki, cr: tile(ki, cr, False), _softmax_init(t, nh))
    _softmax_finish(tile(qi, carries, True), o_ref)


DIL_NEAR_TILES = 3


def _dilation_log_count(d):
    cnt = None
    for window, dil in DIL_PATTERNS:
        hit = (d >= 0) & (d <= window) & ((d & (dil - 1)) == 0)
        term = jnp.where(hit, 1.0, 0.0)
        cnt = term if cnt is None else cnt + term
    return jnp.where(cnt > 0.0, jnp.log(jnp.maximum(cnt, 1.0)), NEG_BIG)


def _dil_kernel(q_ref, k_ref, v_ref, cos_ref, sin_ref, o_ref, kr_ref, vpad_ref, bias_ref, *, t, nh):
    qi = pl.program_id(1)

    @pl.when(qi == 0)
    def _():
        _pad_values(v_ref, vpad_ref, nh)
        for p in range(nh // 2):
            lanes = slice(p * LANES, (p + 1) * LANES)
            kr_ref[:, lanes] = _rope(k_ref[:, lanes].astype(F32), cos_ref[...], sin_ref[...]).astype(BF16)
        row, col = _tile_iotas(t)
        for delta in range(DIL_NEAR_TILES + 1):
            bias_ref[delta] = _dilation_log_count(delta * t + row - col)

    qrows = pl.ds(pl.multiple_of(qi * t, t), t)
    cos, sin = cos_ref[qrows, :], sin_ref[qrows, :]
    qs = [_keep_own(_rope(_pair_tile(q_ref, slice(None), h).astype(F32), cos, sin), h) for h in range(nh)]

    def tile(ki, carries):
        rows = pl.ds(pl.multiple_of(ki * t, t), t)
        bias = bias_ref[jnp.minimum(qi - ki, DIL_NEAR_TILES)]
        scores = [_qk(qs[h], _pair_tile(kr_ref, rows, h)) + bias for h in range(nh)]
        values = [vpad_ref[rows, h * LANES:(h + 1) * LANES] for h in range(nh)]
        return _online_softmax_steps(scores, values, carries)

    _softmax_finish(lax.fori_loop(0, qi + 1, tile, _softmax_init(t, nh)), o_ref)


def _attention_call(kernel, qkv, n_heads, extra_inputs, extra_specs, scratch, name):
    b, s, w3 = qkv.shape
    w = w3 // 3
    t = min(ATTN_TILE, s)
    return pl.pallas_call(
        functools.partial(kernel, t=t, nh=n_heads),
        grid=(b, s // t),
        in_specs=[pl.BlockSpec((None, t, w), lambda i, j: (i, j, 0)),
                  pl.BlockSpec((None, s, w), lambda i, j: (i, 0, 1)),
                  pl.BlockSpec((None, s, w), lambda i, j: (i, 0, 2))] + extra_specs,
        out_specs=pl.BlockSpec((None, t, w), lambda i, j: (i, j, 0)),
        out_shape=jax.ShapeDtypeStruct((b, s, w), BF16),
        scratch_shapes=scratch,
        compiler_params=_params(("arbitrary", "arbitrary")),
        name=name,
    )(qkv, qkv, qkv, *extra_inputs)


def sb_attention(qkv):
    return _attention_call(_sb_kernel, qkv, N_HEADS_SB, [], [], [], "sb_attention")


def dil_attention(qkv, cos, sin):
    s = qkv.shape[1]
    t = min(ATTN_TILE, s)
    widest_window, widest_dil = DIL_PATTERNS[-1]
    assert s - 1 <= widest_window and t % widest_dil == 0
    assert all(window < (DIL_NEAR_TILES - 1) * t + 1 for window, _ in DIL_PATTERNS[:-1])
    tab = pl.BlockSpec((None, s, LANES), lambda i, j: (i, 0, 0))
    scratch = [pltpu.VMEM((s, qkv.shape[-1] // 3), BF16), pltpu.VMEM((s, N_HEADS_DIL * LANES), BF16),
               pltpu.VMEM((DIL_NEAR_TILES + 1, t, t), F32)]
    return _attention_call(_dil_kernel, qkv, N_HEADS_DIL, [cos, sin], [tab, tab], scratch, "dil_attention")


def fox_attention(qkv, qaug, kaug):
    s = qkv.shape[1]
    t = min(ATTN_TILE, s)
    wide = N_HEADS_FOX * LANES
    specs = [pl.BlockSpec((None, t, wide), lambda i, j: (i, j, 0)),
             pl.BlockSpec((None, s, wide), lambda i, j: (i, 0, 0))]
    scratch = [pltpu.VMEM((s, wide), BF16), pltpu.VMEM((s, wide), BF16)]
    return _attention_call(_fox_kernel, qkv, N_HEADS_FOX, [qaug, kaug], specs, scratch, "fox_attention")


def _outproj_router_kernel(osb_ref, odil_ref, ofox_ref, gmix_ref, wout_ref, x_ref, ga_ref, sc_ref, sh_ref,
                           gffn_ref, wr_ref, br_ref,
                           xo_ref, h2_ref, pos_ref, post_ref, gate_ref, start_ref, len_ref, base_ref, total_ref,
                           carry_ref):
    first = (pl.program_id(0) == 0) & (pl.program_id(1) == 0)

    @pl.when(first)
    def _():
        carry_ref[...] = jnp.zeros_like(carry_ref)

    a = None
    lo = 0
    for o_ref in (osb_ref, odil_ref, ofox_ref):
        w = o_ref.shape[-1]
        on = (_rms_rows(o_ref[...].astype(F32)) * gmix_ref[:, lo:lo + w]).astype(BF16)
        part = jnp.dot(on, wout_ref[lo:lo + w, :], preferred_element_type=F32)
        a = part if a is None else a + part
        lo += w
    xn = x_ref[...] + ga_ref[...] * a
    xo_ref[...] = xn
    h2 = _rms_rows(xn) * gffn_ref[...] * (1.0 + sc_ref[...]) + sh_ref[...]
    h2_ref[...] = h2

    h_hi = h2.astype(BF16)
    h_lo = (h2 - h_hi.astype(F32)).astype(BF16)
    both = jnp.dot(h_hi, wr_ref[...], preferred_element_type=F32)
    logits = (both[:, :LANES] + both[:, LANES:]
              + jnp.dot(h_lo, wr_ref[:, :LANES], preferred_element_type=F32) + br_ref[...])
    tm = logits.shape[0]
    lane = lax.broadcasted_iota(jnp.int32, (tm, LANES), 1)
    vals, idxs = [], []
    rest = logits
    for _ in range(TOP_K):
        m = jnp.max(rest, axis=-1, keepdims=True)
        ik = jnp.min(jnp.where(rest == m, lane, LANES), axis=-1, keepdims=True)
        vals.append(m)
        idxs.append(ik)
        rest = jnp.where(lane == ik, -jnp.inf, rest)
    es = [jnp.exp(v - vals[0]) for v in vals]
    den = es[0] + es[1] + es[2] + es[3]

    hot = [(lane == ik) for ik in idxs]
    multi = jnp.where(hot[0] | hot[1] | hot[2] | hot[3], 1.0, 0.0)
    r = lax.broadcasted_iota(jnp.int32, (tm, tm), 0)
    c = lax.broadcasted_iota(jnp.int32, (tm, tm), 1)
    before = (c < r).astype(BF16)
    prior = jnp.dot(before, multi.astype(BF16), preferred_element_type=F32)
    counts = (prior[tm - 1:tm, :] + multi[tm - 1:tm, :]).astype(jnp.int32)
    group_len = (counts + (GROUP_ALIGN - 1)) // GROUP_ALIGN * GROUP_ALIGN
    er = lax.broadcasted_iota(jnp.int32, (LANES, LANES), 0)
    ec = lax.broadcasted_iota(jnp.int32, (LANES, LANES), 1)
    lens8 = jnp.broadcast_to(group_len.astype(F32), (8, LANES)).astype(BF16)
    group_start = jnp.dot(lens8, (er < ec).astype(BF16), preferred_element_type=F32)[0:1, :]
    where_in_tile = group_start + prior
    pos_out = jnp.zeros((tm, LANES), F32)
    gate_out = jnp.zeros((tm, LANES), F32)
    for k in range(TOP_K):
        pk = jnp.sum(jnp.where(hot[k], where_in_tile, 0.0), axis=-1, keepdims=True)
        pos_out = jnp.where(lane == k, pk, pos_out)
        gate_out = jnp.where(lane == k, es[k] / den, gate_out)
    pos_ref[...] = pos_out.astype(jnp.int32)
    post_ref[...] = pos_out.T[0:8, :].astype(jnp.int32)
    gate_ref[...] = gate_out
    start_ref[...] = group_start.astype(jnp.int32)
    len_ref[...] = group_len
    base_ref[...] = carry_ref[...]
    carry_ref[...] = carry_ref[...] + group_len
    total_ref[...] = carry_ref[...]


def outproj_router(o_sb, o_dil, o_fox, g_mix, w_out, x, mod, g_ffn, w_r, b_r):
    b, s, d = x.shape
    tm = min(ROW_TILE, s)
    n = b * s

    def act(w):
        return pl.BlockSpec((None, tm, w), lambda i, j: (i, j, 0))

    def modspec(col):
        return pl.BlockSpec((None, 1, d), lambda i, j: (i, 0, col))

    def const(shape):
        return pl.BlockSpec(shape, lambda i, j: (0, 0))

    per_b = s // tm
    tok = pl.BlockSpec((tm, LANES), lambda i, j: (i * per_b + j, 0))
    per_tile = pl.BlockSpec((None, 1, LANES), lambda i, j: (i * per_b + j, 0, 0))
    ntiles = b * per_b
    tile_tab = jax.ShapeDtypeStruct((ntiles, 1, LANES), jnp.int32)
    return pl.pallas_call(
        _outproj_router_kernel,
        grid=(b, per_b),
        in_specs=[act(o_sb.shape[-1]), act(o_dil.shape[-1]), act(o_fox.shape[-1]),
                  const((1, d)), const(w_out.shape), act(d),
                  modspec(2), modspec(4), modspec(3),
                  const((1, d)), const((d, 2 * LANES)), const((1, LANES))],
        out_specs=[act(d), pl.BlockSpec((tm, d), lambda i, j: (i * per_b + j, 0)), tok,
                   pl.BlockSpec((None, 8, tm), lambda i, j: (i * per_b + j, 0, 0)), tok,
                   per_tile, per_tile, per_tile, const((1, LANES))],
        out_shape=[jax.ShapeDtypeStruct((b, s, d), F32), jax.ShapeDtypeStruct((n, d), F32),
                   jax.ShapeDtypeStruct((n, LANES), jnp.int32), jax.ShapeDtypeStruct((ntiles, 8, tm), jnp.int32),
                   jax.ShapeDtypeStruct((n, LANES), F32), tile_tab, tile_tab, tile_tab,
                   jax.ShapeDtypeStruct((1, LANES), jnp.int32)],
        scratch_shapes=[pltpu.VMEM((1, LANES), jnp.int32)],
        compiler_params=_params(("arbitrary", "arbitrary")),
        name="outproj_router",
    )(o_sb, o_dil, o_fox, g_mix, w_out, x, mod, mod, mod, g_ffn, w_r, b_r)


GROUP_ALIGN = 8
GROUP_CHUNKS = (512, 256, 128, 64, 32, 16, 8)


def sorted_rows(tm, n_experts):
    return -(-(TOP_K * tm + n_experts * (GROUP_ALIGN - 1)) // 256) * 256


def _for_each_chunk(len_ref, dst_ref, first, count, fn):
    def body(e, _):
        length = len_ref[first + e]
        hbm_row = dst_ref[first + e]
        offset = jnp.int32(0)
        for size in GROUP_CHUNKS:
            hit = (length & size) != 0

            @pl.when(hit)
            def _(offset=offset, hbm_row=hbm_row, size=size):
                fn(size, first + e, pl.multiple_of(offset, GROUP_ALIGN), pl.multiple_of(hbm_row, GROUP_ALIGN))

            step = jnp.where(hit, size, 0)
            offset = offset + step
            hbm_row = hbm_row + step
        return 0

    lax.fori_loop(0, count, body, 0)


def _dispatch_kernel(start_ref, len_ref, dst_ref, tail_len_ref, tail_dst_ref, nused_ref, h_ref, post_ref, xs_hbm,
                     buf_ref, sem, *, ne, te):
    rows, tm = buf_ref.shape[0], h_ref.shape[0]
    nblk = xs_hbm.shape[0] // te

    @pl.when(pl.program_id(0) == 0)
    def _():
        buf_ref[0:te, :] = jnp.zeros((te, buf_ref.shape[1]), F32)

        def tail(size, e, offset, hbm_row):
            return pltpu.make_async_copy(buf_ref.at[pl.ds(0, size)], xs_hbm.at[pl.ds(hbm_row, size)], sem)

        def block(i):
            return pltpu.make_async_copy(buf_ref.at[pl.ds(0, te)], xs_hbm.at[pl.ds(pl.multiple_of(i * te, te), te)], sem)

        _for_each_chunk(tail_len_ref, tail_dst_ref, 0, ne, lambda *a: tail(*a).start())
        lax.fori_loop(nused_ref[0], nblk, lambda i, c: (block(i).start(), c)[1], 0)
        _for_each_chunk(tail_len_ref, tail_dst_ref, 0, ne, lambda *a: tail(*a).wait())
        lax.fori_loop(nused_ref[0], nblk, lambda i, c: (block(i).wait(), c)[1], 0)

    r = lax.broadcasted_iota(jnp.int32, (rows, tm), 0)
    pick = r == post_ref[0:1, :]
    for k in range(1, TOP_K):
        pick = pick | (r == post_ref[k:k + 1, :])
    buf_ref[...] = jnp.dot(jnp.where(pick, 1.0, 0.0).astype(BF16), h_ref[...].astype(BF16),
                           preferred_element_type=F32)

    def copy(size, g, offset, hbm_row):
        buf_row = pl.multiple_of(start_ref[g] + offset, GROUP_ALIGN)
        return pltpu.make_async_copy(buf_ref.at[pl.ds(buf_row, size)], xs_hbm.at[pl.ds(hbm_row, size)], sem)

    first = pl.program_id(0) * ne
    _for_each_chunk(len_ref, dst_ref, first, ne, lambda *a: copy(*a).start())
    _for_each_chunk(len_ref, dst_ref, first, ne, lambda *a: copy(*a).wait())


def dispatch_rows(tabs, tails, nused, h, post, cap, ne, te):
    n, d = h.shape
    tm = post.shape[-1]
    return pl.pallas_call(
        functools.partial(_dispatch_kernel, ne=ne, te=te),
        grid_spec=pltpu.PrefetchScalarGridSpec(
            num_scalar_prefetch=6,
            grid=(n // tm,),
            in_specs=[pl.BlockSpec((tm, d), lambda i, *_: (i, 0)),
                      pl.BlockSpec((None, 8, tm), lambda i, *_: (i, 0, 0))],
            out_specs=pl.BlockSpec(memory_space=pl.ANY),
            scratch_shapes=[pltpu.VMEM((sorted_rows(tm, ne), d), F32), pltpu.SemaphoreType.DMA(())]),
        out_shape=jax.ShapeDtypeStruct((cap, d), F32),
        compiler_params=pltpu.CompilerParams(dimension_semantics=("arbitrary",), has_side_effects=True,
                                             vmem_limit_bytes=VMEM_LIMIT),
        name="dispatch_rows",
    )(*tabs, *tails, nused, h, post)


def _expert_kernel(blk_e_ref, nused_ref, xs_ref, wgu_ref, bgu_ref, wdn_ref, bdn_ref, o_ref):
    del blk_e_ref
    used = pl.program_id(0) < nused_ref[0]

    @pl.when(jnp.logical_not(used))
    def _():
        o_ref[...] = jnp.zeros_like(o_ref)

    @pl.when(used)
    def _():
        f = wdn_ref.shape[0]
        gu = jnp.dot(xs_ref[...].astype(BF16), wgu_ref[...], preferred_element_type=F32) + bgu_ref[...]
        gate = jnp.minimum(gu[:, :f], SWIGLU_LIMIT)
        up = jnp.clip(gu[:, f:], -SWIGLU_LIMIT, SWIGLU_LIMIT)
        act = (up + 1.0) * (gate * jax.nn.sigmoid(SWIGLU_ALPHA * gate))
        o_ref[...] = jnp.dot(act.astype(BF16), wdn_ref[...], preferred_element_type=F32) + bdn_ref[...]


def expert_ffn(blk_e, nused, xs, w_gu, b_gu, w_dn, b_dn, te):
    cap, d = xs.shape
    ne, _, f2 = w_gu.shape
    f = f2 // 2
    nblk = cap // te

    def rows(i, be, nu):
        return (jnp.minimum(i, nu[0] - 1), 0)

    def per_expert(i, be, nu):
        return (be[i], 0, 0)

    return pl.pallas_call(
        _expert_kernel,
        grid_spec=pltpu.PrefetchScalarGridSpec(
            num_scalar_prefetch=2,
            grid=(nblk,),
            in_specs=[pl.BlockSpec((te, d), rows),
                      pl.BlockSpec((None, d, f2), per_expert),
                      pl.BlockSpec((None, 1, f2), per_expert),
                      pl.BlockSpec((None, f, d), per_expert),
                      pl.BlockSpec((None, 1, d), per_expert)],
            out_specs=pl.BlockSpec((te, d), lambda i, be, nu: (i, 0))),
        out_shape=jax.ShapeDtypeStruct((cap, d), F32),
        compiler_params=_params(("arbitrary",)),
        name="expert_ffn",
    )(blk_e, nused, xs, w_gu, b_gu.reshape(ne, 1, f2), w_dn, b_dn.reshape(ne, 1, d))


def _combine_kernel(start_ref, len_ref, dst_ref, ys_hbm, pos_ref, gate_ref, x_ref, ga_ref, xo_ref, buf_ref, sem,
                    *, ne):
    @pl.when(pl.program_id(0) == 0)
    def _():
        buf_ref[...] = jnp.zeros_like(buf_ref)

    def copy(size, g, offset, hbm_row):
        buf_row = pl.multiple_of(start_ref[g] + offset, GROUP_ALIGN)
        return pltpu.make_async_copy(ys_hbm.at[pl.ds(hbm_row, size)], buf_ref.at[pl.ds(buf_row, size)], sem)

    first = pl.program_id(0) * ne
    _for_each_chunk(len_ref, dst_ref, first, ne, lambda *a: copy(*a).start())
    rows, tm = buf_ref.shape[0], pos_ref.shape[0]
    r = lax.broadcasted_iota(jnp.int32, (tm, rows), 1)
    pos, g = pos_ref[...], gate_ref[...]
    weights = jnp.zeros((tm, rows), F32)
    for k in range(TOP_K):
        weights = jnp.where(r == pos[:, k:k + 1], g[:, k:k + 1], weights)
    _for_each_chunk(len_ref, dst_ref, first, ne, lambda *a: copy(*a).wait())
    y = jnp.dot(weights.astype(BF16), buf_ref[...].astype(BF16), preferred_element_type=F32)
    xo_ref[...] = x_ref[...] + ga_ref[...] * y


def combine_rows(tabs, ys, pos, gates, x, mod, ne):
    b, s, d = x.shape
    tm = min(ROW_TILE, s)
    per_b = s // tm
    return pl.pallas_call(
        functools.partial(_combine_kernel, ne=ne),
        grid_spec=pltpu.PrefetchScalarGridSpec(
            num_scalar_prefetch=3,
            grid=(b * per_b,),
            in_specs=[pl.BlockSpec(memory_space=pl.ANY),
                      pl.BlockSpec((tm, LANES), lambda i, *_: (i, 0)),
                      pl.BlockSpec((tm, LANES), lambda i, *_: (i, 0)),
                      pl.BlockSpec((None, tm, d), lambda i, *_: (i // per_b, i % per_b, 0)),
                      pl.BlockSpec((None, 1, d), lambda i, *_: (i // per_b, 0, 5))],
            out_specs=pl.BlockSpec((None, tm, d), lambda i, *_: (i // per_b, i % per_b, 0)),
            scratch_shapes=[pltpu.VMEM((sorted_rows(tm, ne), d), F32), pltpu.SemaphoreType.DMA(())]),
        out_shape=jax.ShapeDtypeStruct((b, s, d), F32),
        compiler_params=_params(("arbitrary",)),
        name="combine_rows",
    )(*tabs, ys, pos, gates, x, mod)


def _final_norm_kernel(x_ref, g_ref, o_ref):
    o_ref[...] = _rms_rows(x_ref[...]) * g_ref[...]


def final_norm(x, g):
    b, s, d = x.shape
    tm = min(ROW_TILE, s)
    spec = pl.BlockSpec((None, tm, d), lambda i, j: (i, j, 0))
    return pl.pallas_call(
        _final_norm_kernel,
        grid=(b, s // tm),
        in_specs=[spec, pl.BlockSpec((1, d), lambda i, j: (0, 0))],
        out_specs=spec,
        out_shape=jax.ShapeDtypeStruct((b, s, d), F32),
        compiler_params=_params(("arbitrary", "arbitrary")),
        name="final_norm",
    )(x, g)


def _routing_tables(starts, lens, bases, totals, n_experts, te, nblk):
    totals = totals[0, :n_experts]
    padded = (totals + te - 1) // te * te
    ends = jnp.cumsum(padded)
    first_row = ends - padded
    dst = (first_row[None, :] + bases[:, 0, :n_experts]).reshape(-1).astype(jnp.int32)
    tabs = (starts[:, 0, :n_experts].reshape(-1), lens[:, 0, :n_experts].reshape(-1), dst)
    tails = ((padded - totals).astype(jnp.int32), (first_row + totals).astype(jnp.int32))
    nused = (ends[-1] // te).astype(jnp.int32)
    blk = jnp.arange(nblk, dtype=jnp.int32)
    blk_e = jnp.sum((ends[None, :] <= (blk * te)[:, None]).astype(jnp.int32), axis=1)
    blk_e = jnp.minimum(blk_e, n_experts - 1)
    blk_e = jnp.where(blk < nused, blk_e, blk_e[nused - 1])
    return tabs, tails, blk_e, nused.reshape(1)


def _split_bf16(w):
    hi = w.astype(BF16)
    return jnp.concatenate([hi, (w - hi.astype(F32)).astype(BF16)], axis=1)


def _pad_lanes(a, fill=0.0):
    return jnp.pad(a, [(0, 0)] * (a.ndim - 1) + [(0, LANES - a.shape[-1])], constant_values=fill)


def kernel(x, c, positions, w_mod, b_mod, g_attn, w_in, b_forget, g_mix, w_out, g_ffn, w_router, b_router,
           w_gate_up, b_gate_up, w_down, b_down, g_final):
    b, s, d = x.shape
    depth = w_mod.shape[0]
    n_experts = w_router.shape[-1]
    d_mix = w_out.shape[1]
    n = b * s
    te = min(EXPERT_TILE, n)
    ntiles = n // min(ROW_TILE, s)
    nblk = -(-(n * TOP_K + ntiles * n_experts * (GROUP_ALIGN - 1) + n_experts * (te - 1)) // te)
    cap = nblk * te

    mod_all = modulation(c, w_mod, b_mod)
    cos, sin = rope_tables(positions)
    for layer in range(depth):
        mod = mod_all[layer].reshape(b, 1, N_MOD * d)
        w_qkv = family_major_qkv_weight(w_in[layer], d_mix)
        w_f = _pad_lanes(w_in[layer, :, 3 * d_mix:]).astype(BF16)
        b_f = _pad_lanes(b_forget[layer][None, :])
        qkv_sb, qkv_dil, qkv_fox, logf = in_projection(x, mod, g_attn[layer][None, :], w_qkv, w_f, b_f)
        qaug, kaug = forget_cumsum(logf, N_HEADS_FOX)
        o_sb = sb_attention(qkv_sb)
        o_dil = dil_attention(qkv_dil, cos, sin)
        o_fox = fox_attention(qkv_fox, qaug, kaug)
        x, h2, pos, post, gates, starts, lens, bases, totals = outproj_router(
            o_sb, o_dil, o_fox, g_mix[layer][None, :], w_out[layer].astype(BF16), x, mod, g_ffn[layer][None, :],
            _split_bf16(_pad_lanes(w_router[layer])), _pad_lanes(b_router[layer][None, :], NEG_BIG))
        tabs, tails, blk_e, nused = _routing_tables(starts, lens, bases, totals, n_experts, te, nblk)
        xs = dispatch_rows(tabs, tails, nused, h2, post, cap, n_experts, te)
        ys = expert_ffn(blk_e, nused, xs, w_gate_up[layer].astype(BF16), b_gate_up[layer],
                        w_down[layer].astype(BF16), b_down[layer], te)
        x = combine_rows(tabs, ys, pos, gates, x, mod, n_experts)
    return final_norm(x, g_final[None, :])
```

```python
import functools

import numpy as np
import jax
import jax.numpy as jnp
from jax import lax
from jax.experimental import pallas as pl
from jax.experimental.pallas import tpu as pltpu

F32 = jnp.float32
BF16 = jnp.bfloat16
HIGHEST = lax.Precision.HIGHEST

HEAD_DIM = 64
LANES = 128
N_HEADS_SB = 4
N_HEADS_DIL = 6
N_HEADS_FOX = 6
DIL_PATTERNS = ((128, 1), (512, 4), (2048, 16))
ROPE_THETA = 10000.0
TOP_K = 4
SWIGLU_LIMIT = 7.0
SWIGLU_ALPHA = 1.702
N_MOD = 6
EPS = 1e-6
NEG_BIG = -1e30
ATTN_TILE = 256
ROW_TILE = 512
EXPERT_TILE = 512
VMEM_LIMIT = 48 * 1024 * 1024
EXPERT_VMEM_LIMIT = 60 * 1024 * 1024


def _params(sem, vmem=VMEM_LIMIT):
    return pltpu.CompilerParams(dimension_semantics=sem, vmem_limit_bytes=vmem)


def _rms_rows(x):
    return x * lax.rsqrt(jnp.mean(x * x, axis=-1, keepdims=True) + EPS)


def _softplus(z):
    return jnp.maximum(z, 0.0) + jnp.log(1.0 + jnp.exp(-jnp.abs(z)))


def _mod_kernel(c_ref, w_ref, b_ref, o_ref):
    c = c_ref[...]
    ca = c * jax.nn.sigmoid(c)
    o_ref[...] = jnp.dot(ca, w_ref[...], preferred_element_type=F32, precision=HIGHEST) + b_ref[...]


def modulation(c, w_mod, b_mod):
    depth, d, n6 = w_mod.shape
    b = c.shape[0]
    tn = min(n6, 1536)
    return pl.pallas_call(
        _mod_kernel,
        grid=(depth, n6 // tn),
        in_specs=[pl.BlockSpec((b, d), lambda l, j: (0, 0)),
                  pl.BlockSpec((None, d, tn), lambda l, j: (l, 0, j)),
                  pl.BlockSpec((None, 1, tn), lambda l, j: (l, 0, j))],
        out_specs=pl.BlockSpec((None, b, tn), lambda l, j: (l, 0, j)),
        out_shape=jax.ShapeDtypeStruct((depth, b, n6), F32),
        compiler_params=_params(("arbitrary", "arbitrary")),
        name="modulation",
    )(c, w_mod, b_mod.reshape(depth, 1, n6))


def _rope_table_kernel(pos_ref, invf_ref, cos_ref, sin_ref):
    ang = pos_ref[...].astype(F32) * invf_ref[...]
    lane = lax.broadcasted_iota(jnp.int32, ang.shape, 1)
    first_half = (lane % HEAD_DIM) < (HEAD_DIM // 2)
    s = jnp.sin(ang)
    cos_ref[...] = jnp.cos(ang)
    sin_ref[...] = jnp.where(first_half, -s, s)


def rope_tables(positions):
    b, s = positions.shape
    half = HEAD_DIM // 2
    inv_freq = ROPE_THETA ** (-np.arange(0, HEAD_DIM, 2, dtype=np.float64) / HEAD_DIM)
    invf = jnp.asarray(np.tile(inv_freq, LANES // half)[None, :], F32)
    return pl.pallas_call(
        _rope_table_kernel,
        grid=(b,),
        in_specs=[pl.BlockSpec((None, s, 1), lambda i: (i, 0, 0)),
                  pl.BlockSpec((1, LANES), lambda i: (0, 0))],
        out_specs=[pl.BlockSpec((None, s, LANES), lambda i: (i, 0, 0))] * 2,
        out_shape=[jax.ShapeDtypeStruct((b, s, LANES), F32)] * 2,
        compiler_params=_params(("arbitrary",)),
        name="rope_tables",
    )(positions.reshape(b, s, 1), invf)


def _rope(x, cos, sin_signed):
    lane = lax.broadcasted_iota(jnp.int32, x.shape, 1)
    first_half = (lane % HEAD_DIM) < (HEAD_DIM // 2)
    half = HEAD_DIM // 2
    partner = jnp.where(first_half, pltpu.roll(x, LANES - half, 1), pltpu.roll(x, half, 1))
    return x * cos + partner * sin_signed


FAMILY_HEADS = (N_HEADS_SB, N_HEADS_DIL, N_HEADS_FOX)


def family_major_qkv_weight(w_in_layer, d_mix):
    wq = w_in_layer[:, :d_mix] * (HEAD_DIM ** -0.5)
    wk = w_in_layer[:, d_mix:2 * d_mix]
    wv = w_in_layer[:, 2 * d_mix:3 * d_mix]
    cols, lo = [], 0
    for nh in FAMILY_HEADS:
        hi = lo + nh * HEAD_DIM
        cols += [wq[:, lo:hi], wk[:, lo:hi], wv[:, lo:hi]]
        lo = hi
    return jnp.concatenate(cols, axis=1).astype(BF16)


def _inproj_kernel(x_ref, sh_ref, sc_ref, g_ref, wqkv_ref, wf_ref, bf_ref, sb_ref, dil_ref, fox_ref, logf_ref):
    x = x_ref[...]
    h = _rms_rows(x) * g_ref[...] * (1.0 + sc_ref[...]) + sh_ref[...]
    hb = h.astype(BF16)
    lo = 0
    for o_ref in (sb_ref, dil_ref, fox_ref):
        w = o_ref.shape[-1]
        o_ref[...] = jnp.dot(hb, wqkv_ref[:, lo:lo + w], preferred_element_type=F32).astype(BF16)
        lo += w
    f = jnp.dot(hb, wf_ref[...], preferred_element_type=F32) + bf_ref[...]
    logf_ref[...] = -_softplus(-f)


def in_projection(x, mod, g, w_qkv, w_f, b_f):
    b, s, d = x.shape
    d_mix = w_qkv.shape[1] // 3
    tm = min(ROW_TILE, s)
    widths = [3 * nh * HEAD_DIM for nh in FAMILY_HEADS]
    return pl.pallas_call(
        _inproj_kernel,
        grid=(b, s // tm),
        in_specs=[pl.BlockSpec((None, tm, d), lambda i, j: (i, j, 0)),
                  pl.BlockSpec((None, 1, d), lambda i, j: (i, 0, 0)),
                  pl.BlockSpec((None, 1, d), lambda i, j: (i, 0, 1)),
                  pl.BlockSpec((1, d), lambda i, j: (0, 0)),
                  pl.BlockSpec((d, 3 * d_mix), lambda i, j: (0, 0)),
                  pl.BlockSpec((d, LANES), lambda i, j: (0, 0)),
                  pl.BlockSpec((1, LANES), lambda i, j: (0, 0))],
        out_specs=[pl.BlockSpec((None, tm, w), lambda i, j: (i, j, 0)) for w in widths + [LANES]],
        out_shape=[jax.ShapeDtypeStruct((b, s, w), BF16) for w in widths]
        + [jax.ShapeDtypeStruct((b, s, LANES), F32)],
        compiler_params=_params(("arbitrary", "arbitrary")),
        name="in_projection",
    )(x, mod, mod, g, w_qkv, w_f, b_f)


AUG_TERMS = 3


def _own_half(shape, h):
    lane = lax.broadcasted_iota(jnp.int32, shape, len(shape) - 1)
    return (lane < HEAD_DIM) if h % 2 == 0 else (lane >= HEAD_DIM)


def _spare_lane(shape, h):
    lane = lax.broadcasted_iota(jnp.int32, shape, len(shape) - 1)
    return lane - (HEAD_DIM if h % 2 == 0 else 0)


def _cumsum_kernel(logf_ref, qaug_ref, kaug_ref, *, nh):
    s = logf_ref.shape[0]
    r = lax.broadcasted_iota(jnp.int32, (LANES, LANES), 0)
    c = lax.broadcasted_iota(jnp.int32, (LANES, LANES), 1)
    tri = (c <= r).astype(F32)
    carry = jnp.zeros((1, LANES), F32)
    for i in range(s // LANES):
        rows = slice(i * LANES, (i + 1) * LANES)
        cs = jnp.dot(tri, logf_ref[rows, :], preferred_element_type=F32, precision=HIGHEST) + carry
        carry = cs[LANES - 1:LANES, :]
        for h in range(nh):
            full = jnp.broadcast_to(cs[:, h:h + 1], (LANES, LANES))
            parts, rest = [], full
            for _ in range(AUG_TERMS):
                piece = rest.astype(BF16).astype(F32)
                parts.append(piece)
                rest = rest - piece
            j = _spare_lane((LANES, LANES), h)
            qa = jnp.where((j >= AUG_TERMS) & (j < 2 * AUG_TERMS), 1.0, 0.0)
            ka = jnp.where((j >= 0) & (j < AUG_TERMS), 1.0, 0.0)
            for n, piece in enumerate(parts):
                qa = jnp.where(j == n, piece, qa)
                ka = jnp.where(j == AUG_TERMS + n, -piece, ka)
            qaug_ref[rows, h * LANES:(h + 1) * LANES] = qa.astype(BF16)
            kaug_ref[rows, h * LANES:(h + 1) * LANES] = ka.astype(BF16)


def forget_cumsum(logf, nh):
    b, s, _ = logf.shape
    out = pl.BlockSpec((None, s, nh * LANES), lambda i: (i, 0, 0))
    return pl.pallas_call(
        functools.partial(_cumsum_kernel, nh=nh),
        grid=(b,),
        in_specs=[pl.BlockSpec((None, s, LANES), lambda i: (i, 0, 0))],
        out_specs=[out, out],
        out_shape=[jax.ShapeDtypeStruct((b, s, nh * LANES), BF16)] * 2,
        compiler_params=_params(("arbitrary",)),
        name="forget_cumsum",
    )(logf)


def _tile_iotas(t):
    return lax.broadcasted_iota(jnp.int32, (t, t), 0), lax.broadcasted_iota(jnp.int32, (t, t), 1)


def _pair_tile(ref, rows, h):
    p = h // 2
    return ref[rows, p * LANES:(p + 1) * LANES]


def _keep_own(tile, h, other=0.0):
    return jnp.where(_own_half(tile.shape, h), tile.astype(F32), other).astype(BF16)


def _qk(q, k):
    return lax.dot_general(q, k, (((1,), (1,)), ((), ())), preferred_element_type=F32)


def _merge_pairs(per_head, o_ref):
    tiles = [jnp.where(_own_half(per_head[h].shape, h), per_head[h], per_head[h + 1])
             for h in range(0, len(per_head), 2)]
    o_ref[...] = jnp.concatenate(tiles, axis=-1).astype(o_ref.dtype)


def _sb_kernel(q_ref, k_ref, v_ref, o_ref, *, t, nh):
    qi = pl.program_id(1)
    row, col = _tile_iotas(t)
    strict = col < row
    upper = (row > col).astype(BF16)
    qs = [_keep_own(_pair_tile(q_ref, slice(None), h), h) for h in range(nh)]

    def tile(ki, carries, diag):
        rows = pl.ds(pl.multiple_of(ki * t, t), t)
        heads = range(nh)
        zs = [_qk(qs[h], _pair_tile(k_ref, rows, h)) for h in heads]
        sps = [_softplus(z) for z in zs]
        log_nots = [jnp.where(strict, -sp, 0.0) if diag else -sp for sp in sps]
        his = [ln.astype(BF16) for ln in log_nots]
        los = [(ln - hi.astype(F32)).astype(BF16) for ln, hi in zip(log_nots, his)]
        betweens = [jnp.dot(hi, upper, preferred_element_type=F32) + jnp.dot(lo, upper, preferred_element_type=F32)
                    for hi, lo in zip(his, los)]
        out = []
        for h in heads:
            suffix, acc = carries[h]
            w = jnp.exp(zs[h] - sps[h] + betweens[h] + suffix)
            if diag:
                w = jnp.where(strict, w, 0.0)
            acc = acc + jnp.dot(w.astype(BF16), _pair_tile(v_ref, rows, h), preferred_element_type=F32)
            out.append((suffix + betweens[h][:, 0:1] + log_nots[h][:, 0:1], acc))
        return tuple(out)

    init = tuple((jnp.zeros((t, 1), F32), jnp.zeros((t, LANES), F32)) for _ in range(nh))
    carries = tile(qi, init, True)
    carries = lax.fori_loop(0, qi, lambda j, cr: tile(qi - 1 - j, cr, False), carries)
    _merge_pairs([cr[1] for cr in carries], o_ref)


def _online_softmax_steps(scores, values, carries):
    m_news = [jnp.maximum(m, jnp.max(s, axis=-1, keepdims=True)) for s, (m, _) in zip(scores, carries)]
    ps = [jnp.exp(s - m_new).astype(BF16) for s, m_new in zip(scores, m_news)]
    return tuple((m_new, jnp.exp(m - m_new) * acc + jnp.dot(p, v, preferred_element_type=F32))
                 for p, v, m_new, (m, acc) in zip(ps, values, m_news, carries))


def _softmax_init(t, nh):
    return tuple((jnp.full((t, 1), NEG_BIG, F32), jnp.zeros((t, LANES), F32)) for _ in range(nh))


def _softmax_finish(carries, o_ref):
    outs = []
    for h, (_, acc) in enumerate(carries):
        denom = jnp.sum(jnp.where(_spare_lane(acc.shape, h) == 0, acc, 0.0), axis=-1, keepdims=True)
        outs.append(acc / denom)
    _merge_pairs(outs, o_ref)


def _pad_values(v_ref, vpad_ref, nh):
    for h in range(nh):
        tile = _pair_tile(v_ref, slice(None), h)
        ones_col = jnp.where(_spare_lane(tile.shape, h) == 0, 1.0, 0.0)
        vpad_ref[:, h * LANES:(h + 1) * LANES] = _keep_own(tile, h, ones_col)


def _fox_kernel(q_ref, k_ref, v_ref, qaug_ref, kaug_ref, o_ref, kpad_ref, vpad_ref, *, t, nh):
    qi = pl.program_id(1)
    row, col = _tile_iotas(t)

    @pl.when(qi == 0)
    def _():
        _pad_values(v_ref, vpad_ref, nh)
        for h in range(nh):
            lanes = slice(h * LANES, (h + 1) * LANES)
            kpad_ref[:, lanes] = _keep_own(_pair_tile(k_ref, slice(None), h), h, kaug_ref[:, lanes].astype(F32))

    qs = [_keep_own(_pair_tile(q_ref, slice(None), h), h, qaug_ref[:, h * LANES:(h + 1) * LANES].astype(F32))
          for h in range(nh)]

    def tile(ki, carries, diag):
        rows = pl.ds(pl.multiple_of(ki * t, t), t)
        lanes = [slice(h * LANES, (h + 1) * LANES) for h in range(nh)]
        scores = [_qk(qs[h], kpad_ref[rows, lanes[h]]) for h in range(nh)]
        if diag:
            scores = [jnp.where(col <= row, s, NEG_BIG) for s in scores]
        return _online_softmax_steps(scores, [vpad_ref[rows, ln] for ln in lanes], carries)

    carries = lax.fori_loop(0, qi, lambda ki, cr: tile(ki, cr, False), _softmax_init(t, nh))
    _softmax_finish(tile(qi, carries, True), o_ref)


DIL_NEAR_TILES = 3


def _dilation_log_count(d):
    cnt = None
    for window, dil in DIL_PATTERNS:
        hit = (d >= 0) & (d <= window) & ((d & (dil - 1)) == 0)
        term = jnp.where(hit, 1.0, 0.0)
        cnt = term if cnt is None else cnt + term
    return jnp.where(cnt > 0.0, jnp.log(jnp.maximum(cnt, 1.0)), NEG_BIG)


def _dil_kernel(q_ref, k_ref, v_ref, cos_ref, sin_ref, o_ref, kr_ref, vpad_ref, bias_ref, *, t, nh):
    qi = pl.program_id(1)

    @pl.when(qi == 0)
    def _():
        _pad_values(v_ref, vpad_ref, nh)
        for p in range(nh // 2):
            lanes = slice(p * LANES, (p + 1) * LANES)
            kr_ref[:, lanes] = _rope(k_ref[:, lanes].astype(F32), cos_ref[...], sin_ref[...]).astype(BF16)
        row, col = _tile_iotas(t)
        for delta in range(DIL_NEAR_TILES + 1):
            bias_ref[delta] = _dilation_log_count(delta * t + row - col)

    qrows = pl.ds(pl.multiple_of(qi * t, t), t)
    cos, sin = cos_ref[qrows, :], sin_ref[qrows, :]
    qs = [_keep_own(_rope(_pair_tile(q_ref, slice(None), h).astype(F32), cos, sin), h) for h in range(nh)]

    def tile(ki, carries):
        rows = pl.ds(pl.multiple_of(ki * t, t), t)
        bias = bias_ref[jnp.minimum(qi - ki, DIL_NEAR_TILES)]
        scores = [_qk(qs[h], _pair_tile(kr_ref, rows, h)) + bias for h in range(nh)]
        values = [vpad_ref[rows, h * LANES:(h + 1) * LANES] for h in range(nh)]
        return _online_softmax_steps(scores, values, carries)

    _softmax_finish(lax.fori_loop(0, qi + 1, tile, _softmax_init(t, nh)), o_ref)


def _attention_call(kernel, qkv, n_heads, extra_inputs, extra_specs, scratch, name):
    b, s, w3 = qkv.shape
    w = w3 // 3
    t = min(ATTN_TILE, s)
    return pl.pallas_call(
        functools.partial(kernel, t=t, nh=n_heads),
        grid=(b, s // t),
        in_specs=[pl.BlockSpec((None, t, w), lambda i, j: (i, j, 0)),
                  pl.BlockSpec((None, s, w), lambda i, j: (i, 0, 1)),
                  pl.BlockSpec((None, s, w), lambda i, j: (i, 0, 2))] + extra_specs,
        out_specs=pl.BlockSpec((None, t, w), lambda i, j: (i, j, 0)),
        out_shape=jax.ShapeDtypeStruct((b, s, w), BF16),
        scratch_shapes=scratch,
        compiler_params=_params(("arbitrary", "arbitrary")),
        name=name,
    )(qkv, qkv, qkv, *extra_inputs)


def sb_attention(qkv):
    return _attention_call(_sb_kernel, qkv, N_HEADS_SB, [], [], [], "sb_attention")


def dil_attention(qkv, cos, sin):
    s = qkv.shape[1]
    t = min(ATTN_TILE, s)
    widest_window, widest_dil = DIL_PATTERNS[-1]
    assert s - 1 <= widest_window and t % widest_dil == 0
    assert all(window < (DIL_NEAR_TILES - 1) * t + 1 for window, _ in DIL_PATTERNS[:-1])
    tab = pl.BlockSpec((None, s, LANES), lambda i, j: (i, 0, 0))
    scratch = [pltpu.VMEM((s, qkv.shape[-1] // 3), BF16), pltpu.VMEM((s, N_HEADS_DIL * LANES), BF16),
               pltpu.VMEM((DIL_NEAR_TILES + 1, t, t), F32)]
    return _attention_call(_dil_kernel, qkv, N_HEADS_DIL, [cos, sin], [tab, tab], scratch, "dil_attention")


def fox_attention(qkv, qaug, kaug):
    s = qkv.shape[1]
    t = min(ATTN_TILE, s)
    wide = N_HEADS_FOX * LANES
    specs = [pl.BlockSpec((None, t, wide), lambda i, j: (i, j, 0)),
             pl.BlockSpec((None, s, wide), lambda i, j: (i, 0, 0))]
    scratch = [pltpu.VMEM((s, wide), BF16), pltpu.VMEM((s, wide), BF16)]
    return _attention_call(_fox_kernel, qkv, N_HEADS_FOX, [qaug, kaug], specs, scratch, "fox_attention")


def _outproj_router_kernel(osb_ref, odil_ref, ofox_ref, gmix_ref, wout_ref, x_ref, ga_ref, sc_ref, sh_ref,
                           gffn_ref, wr_ref, br_ref,
                           xo_ref, h2_ref, pos_ref, post_ref, gate_ref, start_ref, len_ref, base_ref, total_ref,
                           carry_ref):
    first = (pl.program_id(0) == 0) & (pl.program_id(1) == 0)

    @pl.when(first)
    def _():
        carry_ref[...] = jnp.zeros_like(carry_ref)

    a = None
    lo = 0
    for o_ref in (osb_ref, odil_ref, ofox_ref):
        w = o_ref.shape[-1]
        on = (_rms_rows(o_ref[...].astype(F32)) * gmix_ref[:, lo:lo + w]).astype(BF16)
        part = jnp.dot(on, wout_ref[lo:lo + w, :], preferred_element_type=F32)
        a = part if a is None else a + part
        lo += w
    xn = x_ref[...] + ga_ref[...] * a
    xo_ref[...] = xn
    h2 = _rms_rows(xn) * gffn_ref[...] * (1.0 + sc_ref[...]) + sh_ref[...]
    h2_ref[...] = h2

    h_hi = h2.astype(BF16)
    h_lo = (h2 - h_hi.astype(F32)).astype(BF16)
    both = jnp.dot(h_hi, wr_ref[...], preferred_element_type=F32)
    logits = (both[:, :LANES] + both[:, LANES:]
              + jnp.dot(h_lo, wr_ref[:, :LANES], preferred_element_type=F32) + br_ref[...])
    tm = logits.shape[0]
    lane = lax.broadcasted_iota(jnp.int32, (tm, LANES), 1)
    vals, idxs = [], []
    rest = logits
    for _ in range(TOP_K):
        m = jnp.max(rest, axis=-1, keepdims=True)
        ik = jnp.min(jnp.where(rest == m, lane, LANES), axis=-1, keepdims=True)
        vals.append(m)
        idxs.append(ik)
        rest = jnp.where(lane == ik, -jnp.inf, rest)
    es = [jnp.exp(v - vals[0]) for v in vals]
    den = es[0] + es[1] + es[2] + es[3]

    hot = [(lane == ik) for ik in idxs]
    multi = jnp.where(hot[0] | hot[1] | hot[2] | hot[3], 1.0, 0.0)
    r = lax.broadcasted_iota(jnp.int32, (tm, tm), 0)
    c = lax.broadcasted_iota(jnp.int32, (tm, tm), 1)
    before = (c < r).astype(BF16)
    prior = jnp.dot(before, multi.astype(BF16), preferred_element_type=F32)
    counts = (prior[tm - 1:tm, :] + multi[tm - 1:tm, :]).astype(jnp.int32)
    group_len = (counts + (GROUP_ALIGN - 1)) // GROUP_ALIGN * GROUP_ALIGN
    er = lax.broadcasted_iota(jnp.int32, (LANES, LANES), 0)
    ec = lax.broadcasted_iota(jnp.int32, (LANES, LANES), 1)
    lens8 = jnp.broadcast_to(group_len.astype(F32), (8, LANES)).astype(BF16)
    group_start = jnp.dot(lens8, (er < ec).astype(BF16), preferred_element_type=F32)[0:1, :]
    where_in_tile = group_start + prior
    pos_out = jnp.zeros((tm, LANES), F32)
    gate_out = jnp.zeros((tm, LANES), F32)
    for k in range(TOP_K):
        pk = jnp.sum(jnp.where(hot[k], where_in_tile, 0.0), axis=-1, keepdims=True)
        pos_out = jnp.where(lane == k, pk, pos_out)
        gate_out = jnp.where(lane == k, es[k] / den, gate_out)
    pos_ref[...] = pos_out.astype(jnp.int32)
    post_ref[...] = pos_out.T[0:8, :].astype(jnp.int32)
    gate_ref[...] = gate_out
    start_ref[...] = group_start.astype(jnp.int32)
    len_ref[...] = group_len
    base_ref[...] = carry_ref[...]
    carry_ref[...] = carry_ref[...] + group_len
    total_ref[...] = carry_ref[...]


def outproj_router(o_sb, o_dil, o_fox, g_mix, w_out, x, mod, g_ffn, w_r, b_r):
    b, s, d = x.shape
    tm = min(ROW_TILE, s)
    n = b * s

    def act(w):
        return pl.BlockSpec((None, tm, w), lambda i, j: (i, j, 0))

    def modspec(col):
        return pl.BlockSpec((None, 1, d), lambda i, j: (i, 0, col))

    def const(shape):
        return pl.BlockSpec(shape, lambda i, j: (0, 0))

    per_b = s // tm
    tok = pl.BlockSpec((tm, LANES), lambda i, j: (i * per_b + j, 0))
    per_tile = pl.BlockSpec((None, 1, LANES), lambda i, j: (i * per_b + j, 0, 0))
    ntiles = b * per_b
    tile_tab = jax.ShapeDtypeStruct((ntiles, 1, LANES), jnp.int32)
    return pl.pallas_call(
        _outproj_router_kernel,
        grid=(b, per_b),
        in_specs=[act(o_sb.shape[-1]), act(o_dil.shape[-1]), act(o_fox.shape[-1]),
                  const((1, d)), const(w_out.shape), act(d),
                  modspec(2), modspec(4), modspec(3),
                  const((1, d)), const((d, 2 * LANES)), const((1, LANES))],
        out_specs=[act(d), pl.BlockSpec((tm, d), lambda i, j: (i * per_b + j, 0)), tok,
                   pl.BlockSpec((None, 8, tm), lambda i, j: (i * per_b + j, 0, 0)), tok,
                   per_tile, per_tile, per_tile, const((1, LANES))],
        out_shape=[jax.ShapeDtypeStruct((b, s, d), F32), jax.ShapeDtypeStruct((n, d), F32),
                   jax.ShapeDtypeStruct((n, LANES), jnp.int32), jax.ShapeDtypeStruct((ntiles, 8, tm), jnp.int32),
                   jax.ShapeDtypeStruct((n, LANES), F32), tile_tab, tile_tab, tile_tab,
                   jax.ShapeDtypeStruct((1, LANES), jnp.int32)],
        scratch_shapes=[pltpu.VMEM((1, LANES), jnp.int32)],
        compiler_params=_params(("arbitrary", "arbitrary")),
        name="outproj_router",
    )(o_sb, o_dil, o_fox, g_mix, w_out, x, mod, mod, mod, g_ffn, w_r, b_r)


GROUP_ALIGN = 8
GROUP_CHUNKS = (512, 256, 128, 64, 32, 16, 8)


def sorted_rows(tm, n_experts):
    return -(-(TOP_K * tm + n_experts * (GROUP_ALIGN - 1)) // 256) * 256


def _for_each_chunk(len_ref, dst_ref, first, count, fn):
    def body(e, _):
        length = len_ref[first + e]
        hbm_row = dst_ref[first + e]
        offset = jnp.int32(0)
        for size in GROUP_CHUNKS:
            hit = (length & size) != 0

            @pl.when(hit)
            def _(offset=offset, hbm_row=hbm_row, size=size):
                fn(size, first + e, pl.multiple_of(offset, GROUP_ALIGN), pl.multiple_of(hbm_row, GROUP_ALIGN))

            step = jnp.where(hit, size, 0)
            offset = offset + step
            hbm_row = hbm_row + step
        return 0

    lax.fori_loop(0, count, body, 0)


def _dispatch_kernel(start_ref, len_ref, dst_ref, tail_len_ref, tail_dst_ref, nused_ref, h_ref, post_ref, xs_hbm,
                     buf_ref, sem, *, ne, te):
    rows, tm = buf_ref.shape[0], h_ref.shape[0]
    nblk = xs_hbm.shape[0] // te

    @pl.when(pl.program_id(0) == 0)
    def _():
        buf_ref[0:te, :] = jnp.zeros((te, buf_ref.shape[1]), F32)

        def tail(size, e, offset, hbm_row):
            return pltpu.make_async_copy(buf_ref.at[pl.ds(0, size)], xs_hbm.at[pl.ds(hbm_row, size)], sem)

        def block(i):
            return pltpu.make_async_copy(buf_ref.at[pl.ds(0, te)], xs_hbm.at[pl.ds(pl.multiple_of(i * te, te), te)], sem)

        _for_each_chunk(tail_len_ref, tail_dst_ref, 0, ne, lambda *a: tail(*a).start())
        lax.fori_loop(nused_ref[0], nblk, lambda i, c: (block(i).start(), c)[1], 0)
        _for_each_chunk(tail_len_ref, tail_dst_ref, 0, ne, lambda *a: tail(*a).wait())
        lax.fori_loop(nused_ref[0], nblk, lambda i, c: (block(i).wait(), c)[1], 0)

    r = lax.broadcasted_iota(jnp.int32, (rows, tm), 0)
    pick = r == post_ref[0:1, :]
    for k in range(1, TOP_K):
        pick = pick | (r == post_ref[k:k + 1, :])
    buf_ref[...] = jnp.dot(jnp.where(pick, 1.0, 0.0).astype(BF16), h_ref[...].astype(BF16),
                           preferred_element_type=F32)

    def copy(size, g, offset, hbm_row):
        buf_row = pl.multiple_of(start_ref[g] + offset, GROUP_ALIGN)
        return pltpu.make_async_copy(buf_ref.at[pl.ds(buf_row, size)], xs_hbm.at[pl.ds(hbm_row, size)], sem)

    first = pl.program_id(0) * ne
    _for_each_chunk(len_ref, dst_ref, first, ne, lambda *a: copy(*a).start())
    _for_each_chunk(len_ref, dst_ref, first, ne, lambda *a: copy(*a).wait())


def dispatch_rows(tabs, tails, nused, h, post, cap, ne, te):
    n, d = h.shape
    tm = post.shape[-1]
    return pl.pallas_call(
        functools.partial(_dispatch_kernel, ne=ne, te=te),
        grid_spec=pltpu.PrefetchScalarGridSpec(
            num_scalar_prefetch=6,
            grid=(n // tm,),
            in_specs=[pl.BlockSpec((tm, d), lambda i, *_: (i, 0)),
                      pl.BlockSpec((None, 8, tm), lambda i, *_: (i, 0, 0))],
            out_specs=pl.BlockSpec(memory_space=pl.ANY),
            scratch_shapes=[pltpu.VMEM((sorted_rows(tm, ne), d), F32), pltpu.SemaphoreType.DMA(())]),
        out_shape=jax.ShapeDtypeStruct((cap, d), F32),
        compiler_params=pltpu.CompilerParams(dimension_semantics=("arbitrary",), has_side_effects=True,
                                             vmem_limit_bytes=VMEM_LIMIT),
        name="dispatch_rows",
    )(*tabs, *tails, nused, h, post)


def _expert_kernel(blk_e_ref, nused_ref, xs_ref, wgu_ref, bgu_ref, wdn_ref, bdn_ref, o_ref, wgu_bf, wdn_bf):
    i = pl.program_id(0)
    used = i < nused_ref[0]

    @pl.when(jnp.logical_not(used))
    def _():
        o_ref[...] = jnp.zeros_like(o_ref)

    @pl.when(used & ((i == 0) | (blk_e_ref[i] != blk_e_ref[jnp.maximum(i - 1, 0)])))
    def _():
        wgu_bf[...] = wgu_ref[...].astype(BF16)
        wdn_bf[...] = wdn_ref[...].astype(BF16)

    @pl.when(used)
    def _():
        f = wdn_ref.shape[0]
        gu = jnp.dot(xs_ref[...].astype(BF16), wgu_bf[...], preferred_element_type=F32) + bgu_ref[...]
        gate = jnp.minimum(gu[:, :f], SWIGLU_LIMIT)
        up = jnp.clip(gu[:, f:], -SWIGLU_LIMIT, SWIGLU_LIMIT)
        act = (up + 1.0) * (gate * jax.nn.sigmoid(SWIGLU_ALPHA * gate))
        o_ref[...] = jnp.dot(act.astype(BF16), wdn_bf[...], preferred_element_type=F32) + bdn_ref[...]


def expert_ffn(blk_e, nused, xs, layer, w_gu, b_gu, w_dn, b_dn, te):
    cap, d = xs.shape
    f2 = w_gu.shape[-1]
    f = f2 // 2
    nblk = cap // te

    def rows(i, be, nu):
        return (jnp.minimum(i, nu[0] - 1), 0)

    def per_expert(i, be, nu):
        return (layer, be[i], 0, 0)

    return pl.pallas_call(
        _expert_kernel,
        grid_spec=pltpu.PrefetchScalarGridSpec(
            num_scalar_prefetch=2,
            grid=(nblk,),
            in_specs=[pl.BlockSpec((te, d), rows),
                      pl.BlockSpec((None, None, d, f2), per_expert),
                      pl.BlockSpec((None, None, 1, f2), per_expert),
                      pl.BlockSpec((None, None, f, d), per_expert),
                      pl.BlockSpec((None, None, 1, d), per_expert)],
            out_specs=pl.BlockSpec((te, d), lambda i, be, nu: (i, 0)),
            scratch_shapes=[pltpu.VMEM((d, f2), BF16), pltpu.VMEM((f, d), BF16)]),
        out_shape=jax.ShapeDtypeStruct((cap, d), F32),
        compiler_params=_params(("arbitrary",), EXPERT_VMEM_LIMIT),
        name="expert_ffn",
    )(blk_e, nused, xs, w_gu, b_gu, w_dn, b_dn)


def _combine_kernel(start_ref, len_ref, dst_ref, ys_hbm, pos_ref, gate_ref, x_ref, ga_ref, xo_ref, buf_ref, sem,
                    *, ne):
    @pl.when(pl.program_id(0) == 0)
    def _():
        buf_ref[...] = jnp.zeros_like(buf_ref)

    def copy(size, g, offset, hbm_row):
        buf_row = pl.multiple_of(start_ref[g] + offset, GROUP_ALIGN)
        return pltpu.make_async_copy(ys_hbm.at[pl.ds(hbm_row, size)], buf_ref.at[pl.ds(buf_row, size)], sem)

    first = pl.program_id(0) * ne
    _for_each_chunk(len_ref, dst_ref, first, ne, lambda *a: copy(*a).start())
    rows, tm = buf_ref.shape[0], pos_ref.shape[0]
    r = lax.broadcasted_iota(jnp.int32, (tm, rows), 1)
    pos, g = pos_ref[...], gate_ref[...]
    weights = jnp.zeros((tm, rows), F32)
    for k in range(TOP_K):
        weights = jnp.where(r == pos[:, k:k + 1], g[:, k:k + 1], weights)
    _for_each_chunk(len_ref, dst_ref, first, ne, lambda *a: copy(*a).wait())
    y = jnp.dot(weights.astype(BF16), buf_ref[...].astype(BF16), preferred_element_type=F32)
    xo_ref[...] = x_ref[...] + ga_ref[...] * y


def combine_rows(tabs, ys, pos, gates, x, mod, ne):
    b, s, d = x.shape
    tm = min(ROW_TILE, s)
    per_b = s // tm
    return pl.pallas_call(
        functools.partial(_combine_kernel, ne=ne),
        grid_spec=pltpu.PrefetchScalarGridSpec(
            num_scalar_prefetch=3,
            grid=(b * per_b,),
            in_specs=[pl.BlockSpec(memory_space=pl.ANY),
                      pl.BlockSpec((tm, LANES), lambda i, *_: (i, 0)),
                      pl.BlockSpec((tm, LANES), lambda i, *_: (i, 0)),
                      pl.BlockSpec((None, tm, d), lambda i, *_: (i // per_b, i % per_b, 0)),
                      pl.BlockSpec((None, 1, d), lambda i, *_: (i // per_b, 0, 5))],
            out_specs=pl.BlockSpec((None, tm, d), lambda i, *_: (i // per_b, i % per_b, 0)),
            scratch_shapes=[pltpu.VMEM((sorted_rows(tm, ne), d), F32), pltpu.SemaphoreType.DMA(())]),
        out_shape=jax.ShapeDtypeStruct((b, s, d), F32),
        compiler_params=_params(("arbitrary",)),
        name="combine_rows",
    )(*tabs, ys, pos, gates, x, mod)


def _final_norm_kernel(x_ref, g_ref, o_ref):
    o_ref[...] = _rms_rows(x_ref[...]) * g_ref[...]


def final_norm(x, g):
    b, s, d = x.shape
    tm = min(ROW_TILE, s)
    spec = pl.BlockSpec((None, tm, d), lambda i, j: (i, j, 0))
    return pl.pallas_call(
        _final_norm_kernel,
        grid=(b, s // tm),
        in_specs=[spec, pl.BlockSpec((1, d), lambda i, j: (0, 0))],
        out_specs=spec,
        out_shape=jax.ShapeDtypeStruct((b, s, d), F32),
        compiler_params=_params(("arbitrary", "arbitrary")),
        name="final_norm",
    )(x, g)


def _routing_tables(starts, lens, bases, totals, n_experts, te, nblk):
    totals = totals[0, :n_experts]
    padded = (totals + te - 1) // te * te
    ends = jnp.cumsum(padded)
    first_row = ends - padded
    dst = (first_row[None, :] + bases[:, 0, :n_experts]).reshape(-1).astype(jnp.int32)
    tabs = (starts[:, 0, :n_experts].reshape(-1), lens[:, 0, :n_experts].reshape(-1), dst)
    tails = ((padded - totals).astype(jnp.int32), (first_row + totals).astype(jnp.int32))
    nused = (ends[-1] // te).astype(jnp.int32)
    blk = jnp.arange(nblk, dtype=jnp.int32)
    blk_e = jnp.sum((ends[None, :] <= (blk * te)[:, None]).astype(jnp.int32), axis=1)
    blk_e = jnp.minimum(blk_e, n_experts - 1)
    blk_e = jnp.where(blk < nused, blk_e, blk_e[nused - 1])
    return tabs, tails, blk_e, nused.reshape(1)


def _split_bf16(w):
    hi = w.astype(BF16)
    return jnp.concatenate([hi, (w - hi.astype(F32)).astype(BF16)], axis=1)


def _pad_lanes(a, fill=0.0):
    return jnp.pad(a, [(0, 0)] * (a.ndim - 1) + [(0, LANES - a.shape[-1])], constant_values=fill)


def kernel(x, c, positions, w_mod, b_mod, g_attn, w_in, b_forget, g_mix, w_out, g_ffn, w_router, b_router,
           w_gate_up, b_gate_up, w_down, b_down, g_final):
    b, s, d = x.shape
    depth = w_mod.shape[0]
    n_experts = w_router.shape[-1]
    d_mix = w_out.shape[1]
    n = b * s
    te = min(EXPERT_TILE, n)
    ntiles = n // min(ROW_TILE, s)
    nblk = -(-(n * TOP_K + ntiles * n_experts * (GROUP_ALIGN - 1) + n_experts * (te - 1)) // te)
    cap = nblk * te

    mod_all = modulation(c, w_mod, b_mod)
    cos, sin = rope_tables(positions)
    for layer in range(depth):
        mod = mod_all[layer].reshape(b, 1, N_MOD * d)
        w_qkv = family_major_qkv_weight(w_in[layer], d_mix)
        w_f = _pad_lanes(w_in[layer, :, 3 * d_mix:]).astype(BF16)
        b_f = _pad_lanes(b_forget[layer][None, :])
        qkv_sb, qkv_dil, qkv_fox, logf = in_projection(x, mod, g_attn[layer][None, :], w_qkv, w_f, b_f)
        qaug, kaug = forget_cumsum(logf, N_HEADS_FOX)
        o_sb = sb_attention(qkv_sb)
        o_dil = dil_attention(qkv_dil, cos, sin)
        o_fox = fox_attention(qkv_fox, qaug, kaug)
        x, h2, pos, post, gates, starts, lens, bases, totals = outproj_router(
            o_sb, o_dil, o_fox, g_mix[layer][None, :], w_out[layer].astype(BF16), x, mod, g_ffn[layer][None, :],
            _split_bf16(_pad_lanes(w_router[layer])), _pad_lanes(b_router[layer][None, :], NEG_BIG))
        tabs, tails, blk_e, nused = _routing_tables(starts, lens, bases, totals, n_experts, te, nblk)
        xs = dispatch_rows(tabs, tails, nused, h2, post, cap, n_experts, te)
        ys = expert_ffn(blk_e, nused, xs, layer, w_gate_up, b_gate_up[:, :, None, :], w_down,
                        b_down[:, :, None, :], te)
        x = combine_rows(tabs, ys, pos, gates, x, mod, n_experts)
    return final_norm(x, g_final[None, :])
```

```python
import functools

import numpy as np
import jax
import jax.numpy as jnp
from jax import lax
from jax.experimental import pallas as pl
from jax.experimental.pallas import tpu as pltpu

F32 = jnp.float32
BF16 = jnp.bfloat16
HIGHEST = lax.Precision.HIGHEST

HEAD_DIM = 64
LANES = 128
N_HEADS_SB = 4
N_HEADS_DIL = 6
N_HEADS_FOX = 6
DIL_PATTERNS = ((128, 1), (512, 4), (2048, 16))
ROPE_THETA = 10000.0
TOP_K = 4
SWIGLU_LIMIT = 7.0
SWIGLU_ALPHA = 1.702
N_MOD = 6
EPS = 1e-6
NEG_BIG = -1e30
ATTN_TILE = 256
ROW_TILE = 512
EXPERT_TILE = 512
VMEM_LIMIT = 48 * 1024 * 1024
EXPERT_VMEM_LIMIT = 60 * 1024 * 1024


def _params(sem, vmem=VMEM_LIMIT):
    return pltpu.CompilerParams(dimension_semantics=sem, vmem_limit_bytes=vmem)


def _rms_rows(x):
    return x * lax.rsqrt(jnp.mean(x * x, axis=-1, keepdims=True) + EPS)


def _softplus(z):
    return jnp.maximum(z, 0.0) + jnp.log(1.0 + jnp.exp(-jnp.abs(z)))


def _mod_kernel(c_ref, w_ref, b_ref, o_ref):
    c = c_ref[...]
    ca = c * jax.nn.sigmoid(c)
    o_ref[...] = jnp.dot(ca, w_ref[...], preferred_element_type=F32, precision=HIGHEST) + b_ref[...]


def modulation(c, w_mod, b_mod):
    depth, d, n6 = w_mod.shape
    b = c.shape[0]
    tn = min(n6, 1536)
    return pl.pallas_call(
        _mod_kernel,
        grid=(depth, n6 // tn),
        in_specs=[pl.BlockSpec((b, d), lambda l, j: (0, 0)),
                  pl.BlockSpec((None, d, tn), lambda l, j: (l, 0, j)),
                  pl.BlockSpec((None, 1, tn), lambda l, j: (l, 0, j))],
        out_specs=pl.BlockSpec((None, b, tn), lambda l, j: (l, 0, j)),
        out_shape=jax.ShapeDtypeStruct((depth, b, n6), F32),
        compiler_params=_params(("arbitrary", "arbitrary")),
        name="modulation",
    )(c, w_mod, b_mod.reshape(depth, 1, n6))


def _rope_table_kernel(pos_ref, invf_ref, cos_ref, sin_ref):
    ang = pos_ref[...].astype(F32) * invf_ref[...]
    lane = lax.broadcasted_iota(jnp.int32, ang.shape, 1)
    first_half = (lane % HEAD_DIM) < (HEAD_DIM // 2)
    s = jnp.sin(ang)
    cos_ref[...] = jnp.cos(ang)
    sin_ref[...] = jnp.where(first_half, -s, s)


def rope_tables(positions):
    b, s = positions.shape
    half = HEAD_DIM // 2
    inv_freq = ROPE_THETA ** (-np.arange(0, HEAD_DIM, 2, dtype=np.float64) / HEAD_DIM)
    invf = jnp.asarray(np.tile(inv_freq, LANES // half)[None, :], F32)
    return pl.pallas_call(
        _rope_table_kernel,
        grid=(b,),
        in_specs=[pl.BlockSpec((None, s, 1), lambda i: (i, 0, 0)),
                  pl.BlockSpec((1, LANES), lambda i: (0, 0))],
        out_specs=[pl.BlockSpec((None, s, LANES), lambda i: (i, 0, 0))] * 2,
        out_shape=[jax.ShapeDtypeStruct((b, s, LANES), F32)] * 2,
        compiler_params=_params(("arbitrary",)),
        name="rope_tables",
    )(positions.reshape(b, s, 1), invf)


def _rope(x, cos, sin_signed):
    lane = lax.broadcasted_iota(jnp.int32, x.shape, 1)
    first_half = (lane % HEAD_DIM) < (HEAD_DIM // 2)
    half = HEAD_DIM // 2
    partner = jnp.where(first_half, pltpu.roll(x, LANES - half, 1), pltpu.roll(x, half, 1))
    return x * cos + partner * sin_signed


FAMILY_HEADS = (N_HEADS_SB, N_HEADS_DIL, N_HEADS_FOX)


def family_major_qkv_weight(w_in_layer, d_mix):
    wq = w_in_layer[:, :d_mix] * (HEAD_DIM ** -0.5)
    wk = w_in_layer[:, d_mix:2 * d_mix]
    wv = w_in_layer[:, 2 * d_mix:3 * d_mix]
    cols, lo = [], 0
    for nh in FAMILY_HEADS:
        hi = lo + nh * HEAD_DIM
        cols += [wq[:, lo:hi], wk[:, lo:hi], wv[:, lo:hi]]
        lo = hi
    return jnp.concatenate(cols, axis=1).astype(BF16)


def _inproj_kernel(x_ref, sh_ref, sc_ref, g_ref, wqkv_ref, wf_ref, bf_ref, sb_ref, dil_ref, fox_ref, logf_ref):
    x = x_ref[...]
    h = _rms_rows(x) * g_ref[...] * (1.0 + sc_ref[...]) + sh_ref[...]
    hb = h.astype(BF16)
    lo = 0
    for o_ref in (sb_ref, dil_ref, fox_ref):
        w = o_ref.shape[-1]
        o_ref[...] = jnp.dot(hb, wqkv_ref[:, lo:lo + w], preferred_element_type=F32).astype(BF16)
        lo += w
    f = jnp.dot(hb, wf_ref[...], preferred_element_type=F32) + bf_ref[...]
    logf_ref[...] = -_softplus(-f)


def in_projection(x, mod, g, w_qkv, w_f, b_f):
    b, s, d = x.shape
    d_mix = w_qkv.shape[1] // 3
    tm = min(ROW_TILE, s)
    widths = [3 * nh * HEAD_DIM for nh in FAMILY_HEADS]
    return pl.pallas_call(
        _inproj_kernel,
        grid=(b, s // tm),
        in_specs=[pl.BlockSpec((None, tm, d), lambda i, j: (i, j, 0)),
                  pl.BlockSpec((None, 1, d), lambda i, j: (i, 0, 0)),
                  pl.BlockSpec((None, 1, d), lambda i, j: (i, 0, 1)),
                  pl.BlockSpec((1, d), lambda i, j: (0, 0)),
                  pl.BlockSpec((d, 3 * d_mix), lambda i, j: (0, 0)),
                  pl.BlockSpec((d, LANES), lambda i, j: (0, 0)),
                  pl.BlockSpec((1, LANES), lambda i, j: (0, 0))],
        out_specs=[pl.BlockSpec((None, tm, w), lambda i, j: (i, j, 0)) for w in widths + [LANES]],
        out_shape=[jax.ShapeDtypeStruct((b, s, w), BF16) for w in widths]
        + [jax.ShapeDtypeStruct((b, s, LANES), F32)],
        compiler_params=_params(("arbitrary", "arbitrary")),
        name="in_projection",
    )(x, mod, mod, g, w_qkv, w_f, b_f)


AUG_TERMS = 3


def _own_half(shape, h):
    lane = lax.broadcasted_iota(jnp.int32, shape, len(shape) - 1)
    return (lane < HEAD_DIM) if h % 2 == 0 else (lane >= HEAD_DIM)


def _spare_lane(shape, h):
    lane = lax.broadcasted_iota(jnp.int32, shape, len(shape) - 1)
    return lane - (HEAD_DIM if h % 2 == 0 else 0)


def _cumsum_kernel(logf_ref, qaug_ref, kaug_ref, *, nh):
    s = logf_ref.shape[0]
    r = lax.broadcasted_iota(jnp.int32, (LANES, LANES), 0)
    c = lax.broadcasted_iota(jnp.int32, (LANES, LANES), 1)
    tri = (c <= r).astype(F32)
    carry = jnp.zeros((1, LANES), F32)
    for i in range(s // LANES):
        rows = slice(i * LANES, (i + 1) * LANES)
        cs = jnp.dot(tri, logf_ref[rows, :], preferred_element_type=F32, precision=HIGHEST) + carry
        carry = cs[LANES - 1:LANES, :]
        for h in range(nh):
            full = jnp.broadcast_to(cs[:, h:h + 1], (LANES, LANES))
            parts, rest = [], full
            for _ in range(AUG_TERMS):
                piece = rest.astype(BF16).astype(F32)
                parts.append(piece)
                rest = rest - piece
            j = _spare_lane((LANES, LANES), h)
            qa = jnp.where((j >= AUG_TERMS) & (j < 2 * AUG_TERMS), 1.0, 0.0)
            ka = jnp.where((j >= 0) & (j < AUG_TERMS), 1.0, 0.0)
            for n, piece in enumerate(parts):
                qa = jnp.where(j == n, piece, qa)
                ka = jnp.where(j == AUG_TERMS + n, -piece, ka)
            qaug_ref[rows, h * LANES:(h + 1) * LANES] = qa.astype(BF16)
            kaug_ref[rows, h * LANES:(h + 1) * LANES] = ka.astype(BF16)


def forget_cumsum(logf, nh):
    b, s, _ = logf.shape
    out = pl.BlockSpec((None, s, nh * LANES), lambda i: (i, 0, 0))
    return pl.pallas_call(
        functools.partial(_cumsum_kernel, nh=nh),
        grid=(b,),
        in_specs=[pl.BlockSpec((None, s, LANES), lambda i: (i, 0, 0))],
        out_specs=[out, out],
        out_shape=[jax.ShapeDtypeStruct((b, s, nh * LANES), BF16)] * 2,
        compiler_params=_params(("arbitrary",)),
        name="forget_cumsum",
    )(logf)


def _tile_iotas(t):
    return lax.broadcasted_iota(jnp.int32, (t, t), 0), lax.broadcasted_iota(jnp.int32, (t, t), 1)


def _pair_tile(ref, rows, h):
    p = h // 2
    return ref[rows, p * LANES:(p + 1) * LANES]


def _keep_own(tile, h, other=0.0):
    return jnp.where(_own_half(tile.shape, h), tile.astype(F32), other).astype(BF16)


def _qk(q, k):
    return lax.dot_general(q, k, (((1,), (1,)), ((), ())), preferred_element_type=F32)


def _merge_pairs(per_head, o_ref):
    tiles = [jnp.where(_own_half(per_head[h].shape, h), per_head[h], per_head[h + 1])
             for h in range(0, len(per_head), 2)]
    o_ref[...] = jnp.concatenate(tiles, axis=-1).astype(o_ref.dtype)


def _sb_kernel(q_ref, k_ref, v_ref, o_ref, *, t, nh):
    qi = pl.program_id(1)
    row, col = _tile_iotas(t)
    strict = col < row
    upper = (row > col).astype(BF16)
    qs = [_keep_own(_pair_tile(q_ref, slice(None), h), h) for h in range(nh)]

    def tile(ki, carries, diag):
        rows = pl.ds(pl.multiple_of(ki * t, t), t)
        heads = range(nh)
        zs = [_qk(qs[h], _pair_tile(k_ref, rows, h)) for h in heads]
        sps = [_softplus(z) for z in zs]
        log_nots = [jnp.where(strict, -sp, 0.0) if diag else -sp for sp in sps]
        his = [ln.astype(BF16) for ln in log_nots]
        los = [(ln - hi.astype(F32)).astype(BF16) for ln, hi in zip(log_nots, his)]
        betweens = [jnp.dot(hi, upper, preferred_element_type=F32) + jnp.dot(lo, upper, preferred_element_type=F32)
                    for hi, lo in zip(his, los)]
        out = []
        for h in heads:
            suffix, acc = carries[h]
            w = jnp.exp(zs[h] - sps[h] + betweens[h] + suffix)
            if diag:
                w = jnp.where(strict, w, 0.0)
            acc = acc + jnp.dot(w.astype(BF16), _pair_tile(v_ref, rows, h), preferred_element_type=F32)
            out.append((suffix + betweens[h][:, 0:1] + log_nots[h][:, 0:1], acc))
        return tuple(out)

    init = tuple((jnp.zeros((t, 1), F32), jnp.zeros((t, LANES), F32)) for _ in range(nh))
    carries = tile(qi, init, True)
    carries = lax.fori_loop(0, qi, lambda j, cr: tile(qi - 1 - j, cr, False), carries)
    _merge_pairs([cr[1] for cr in carries], o_ref)


def _online_softmax_steps(scores, values, carries):
    m_news = [jnp.maximum(m, jnp.max(s, axis=-1, keepdims=True)) for s, (m, _) in zip(scores, carries)]
    ps = [jnp.exp(s - m_new).astype(BF16) for s, m_new in zip(scores, m_news)]
    return tuple((m_new, jnp.exp(m - m_new) * acc + jnp.dot(p, v, preferred_element_type=F32))
                 for p, v, m_new, (m, acc) in zip(ps, values, m_news, carries))


def _softmax_init(t, nh):
    return tuple((jnp.full((t, 1), NEG_BIG, F32), jnp.zeros((t, LANES), F32)) for _ in range(nh))


def _softmax_finish(carries, o_ref):
    outs = []
    for h, (_, acc) in enumerate(carries):
        denom = jnp.sum(jnp.where(_spare_lane(acc.shape, h) == 0, acc, 0.0), axis=-1, keepdims=True)
        outs.append(acc / denom)
    _merge_pairs(outs, o_ref)


def _pad_values(v_ref, vpad_ref, nh):
    for h in range(nh):
        tile = _pair_tile(v_ref, slice(None), h)
        ones_col = jnp.where(_spare_lane(tile.shape, h) == 0, 1.0, 0.0)
        vpad_ref[:, h * LANES:(h + 1) * LANES] = _keep_own(tile, h, ones_col)


def _fox_kernel(q_ref, k_ref, v_ref, qaug_ref, kaug_ref, o_ref, kpad_ref, vpad_ref, *, t, nh):
    qi = pl.program_id(1)
    row, col = _tile_iotas(t)

    @pl.when(qi == 0)
    def _():
        _pad_values(v_ref, vpad_ref, nh)
        for h in range(nh):
            lanes = slice(h * LANES, (h + 1) * LANES)
            kpad_ref[:, lanes] = _keep_own(_pair_tile(k_ref, slice(None), h), h, kaug_ref[:, lanes].astype(F32))

    qs = [_keep_own(_pair_tile(q_ref, slice(None), h), h, qaug_ref[:, h * LANES:(h + 1) * LANES].astype(F32))
          for h in range(nh)]

    def tile(ki, carries, diag):
        rows = pl.ds(pl.multiple_of(ki * t, t), t)
        lanes = [slice(h * LANES, (h + 1) * LANES) for h in range(nh)]
        scores = [_qk(qs[h], kpad_ref[rows, lanes[h]]) for h in range(nh)]
        if diag:
            scores = [jnp.where(col <= row, s, NEG_BIG) for s in scores]
        return _online_softmax_steps(scores, [vpad_ref[rows, ln] for ln in lanes], carries)

    carries = lax.fori_loop(0, qi, lambda ki, cr: tile(ki, cr, False), _softmax_init(t, nh))
    _softmax_finish(tile(qi, carries, True), o_ref)


DIL_NEAR_TILES = 3


def _dilation_log_count(d):
    cnt = None
    for window, dil in DIL_PATTERNS:
        hit = (d >= 0) & (d <= window) & ((d & (dil - 1)) == 0)
        term = jnp.where(hit, 1.0, 0.0)
        cnt = term if cnt is None else cnt + term
    return jnp.where(cnt > 0.0, jnp.log(jnp.maximum(cnt, 1.0)), NEG_BIG)


def _dil_kernel(q_ref, k_ref, v_ref, cos_ref, sin_ref, o_ref, kr_ref, vpad_ref, bias_ref, *, t, nh):
    qi = pl.program_id(1)

    @pl.when(qi == 0)
    def _():
        _pad_values(v_ref, vpad_ref, nh)
        for p in range(nh // 2):
            lanes = slice(p * LANES, (p + 1) * LANES)
            kr_ref[:, lanes] = _rope(k_ref[:, lanes].astype(F32), cos_ref[...], sin_ref[...]).astype(BF16)
        row, col = _tile_iotas(t)
        for delta in range(DIL_NEAR_TILES + 1):
            bias_ref[delta] = _dilation_log_count(delta * t + row - col)

    qrows = pl.ds(pl.multiple_of(qi * t, t), t)
    cos, sin = cos_ref[qrows, :], sin_ref[qrows, :]
    qs = [_keep_own(_rope(_pair_tile(q_ref, slice(None), h).astype(F32), cos, sin), h) for h in range(nh)]

    def tile(ki, carries):
        rows = pl.ds(pl.multiple_of(ki * t, t), t)
        bias = bias_ref[jnp.minimum(qi - ki, DIL_NEAR_TILES)]
        scores = [_qk(qs[h], _pair_tile(kr_ref, rows, h)) + bias for h in range(nh)]
        values = [vpad_ref[rows, h * LANES:(h + 1) * LANES] for h in range(nh)]
        return _online_softmax_steps(scores, values, carries)

    _softmax_finish(lax.fori_loop(0, qi + 1, tile, _softmax_init(t, nh)), o_ref)


def _attention_call(kernel, qkv, n_heads, extra_inputs, extra_specs, scratch, name):
    b, s, w3 = qkv.shape
    w = w3 // 3
    t = min(ATTN_TILE, s)
    return pl.pallas_call(
        functools.partial(kernel, t=t, nh=n_heads),
        grid=(b, s // t),
        in_specs=[pl.BlockSpec((None, t, w), lambda i, j: (i, j, 0)),
                  pl.BlockSpec((None, s, w), lambda i, j: (i, 0, 1)),
                  pl.BlockSpec((None, s, w), lambda i, j: (i, 0, 2))] + extra_specs,
        out_specs=pl.BlockSpec((None, t, w), lambda i, j: (i, j, 0)),
        out_shape=jax.ShapeDtypeStruct((b, s, w), BF16),
        scratch_shapes=scratch,
        compiler_params=_params(("arbitrary", "arbitrary")),
        name=name,
    )(qkv, qkv, qkv, *extra_inputs)


def sb_attention(qkv):
    return _attention_call(_sb_kernel, qkv, N_HEADS_SB, [], [], [], "sb_attention")


def dil_attention(qkv, cos, sin):
    s = qkv.shape[1]
    t = min(ATTN_TILE, s)
    widest_window, widest_dil = DIL_PATTERNS[-1]
    assert s - 1 <= widest_window and t % widest_dil == 0
    assert all(window < (DIL_NEAR_TILES - 1) * t + 1 for window, _ in DIL_PATTERNS[:-1])
    tab = pl.BlockSpec((None, s, LANES), lambda i, j: (i, 0, 0))
    scratch = [pltpu.VMEM((s, qkv.shape[-1] // 3), BF16), pltpu.VMEM((s, N_HEADS_DIL * LANES), BF16),
               pltpu.VMEM((DIL_NEAR_TILES + 1, t, t), F32)]
    return _attention_call(_dil_kernel, qkv, N_HEADS_DIL, [cos, sin], [tab, tab], scratch, "dil_attention")


def fox_attention(qkv, qaug, kaug):
    s = qkv.shape[1]
    t = min(ATTN_TILE, s)
    wide = N_HEADS_FOX * LANES
    specs = [pl.BlockSpec((None, t, wide), lambda i, j: (i, j, 0)),
             pl.BlockSpec((None, s, wide), lambda i, j: (i, 0, 0))]
    scratch = [pltpu.VMEM((s, wide), BF16), pltpu.VMEM((s, wide), BF16)]
    return _attention_call(_fox_kernel, qkv, N_HEADS_FOX, [qaug, kaug], specs, scratch, "fox_attention")


def _outproj_router_kernel(osb_ref, odil_ref, ofox_ref, gmix_ref, wout_ref, x_ref, ga_ref, sc_ref, sh_ref,
                           gffn_ref, wr_ref, br_ref,
                           xo_ref, h2_ref, pos_ref, post_ref, gate_ref, start_ref, len_ref, base_ref, total_ref,
                           carry_ref):
    first = (pl.program_id(0) == 0) & (pl.program_id(1) == 0)

    @pl.when(first)
    def _():
        carry_ref[...] = jnp.zeros_like(carry_ref)

    a = None
    lo = 0
    for o_ref in (osb_ref, odil_ref, ofox_ref):
        w = o_ref.shape[-1]
        on = (_rms_rows(o_ref[...].astype(F32)) * gmix_ref[:, lo:lo + w]).astype(BF16)
        part = jnp.dot(on, wout_ref[lo:lo + w, :], preferred_element_type=F32)
        a = part if a is None else a + part
        lo += w
    xn = x_ref[...] + ga_ref[...] * a
    xo_ref[...] = xn
    h2 = _rms_rows(xn) * gffn_ref[...] * (1.0 + sc_ref[...]) + sh_ref[...]
    h2_ref[...] = h2

    h_hi = h2.astype(BF16)
    h_lo = (h2 - h_hi.astype(F32)).astype(BF16)
    both = jnp.dot(h_hi, wr_ref[...], preferred_element_type=F32)
    logits = (both[:, :LANES] + both[:, LANES:]
              + jnp.dot(h_lo, wr_ref[:, :LANES], preferred_element_type=F32) + br_ref[...])
    tm = logits.shape[0]
    lane = lax.broadcasted_iota(jnp.int32, (tm, LANES), 1)
    vals, idxs = [], []
    rest = logits
    for _ in range(TOP_K):
        m = jnp.max(rest, axis=-1, keepdims=True)
        ik = jnp.min(jnp.where(rest == m, lane, LANES), axis=-1, keepdims=True)
        vals.append(m)
        idxs.append(ik)
        rest = jnp.where(lane == ik, -jnp.inf, rest)
    es = [jnp.exp(v - vals[0]) for v in vals]
    den = es[0] + es[1] + es[2] + es[3]

    hot = [(lane == ik) for ik in idxs]
    multi = jnp.where(hot[0] | hot[1] | hot[2] | hot[3], 1.0, 0.0)
    r = lax.broadcasted_iota(jnp.int32, (tm, tm), 0)
    c = lax.broadcasted_iota(jnp.int32, (tm, tm), 1)
    before = (c < r).astype(BF16)
    prior = jnp.dot(before, multi.astype(BF16), preferred_element_type=F32)
    counts = (prior[tm - 1:tm, :] + multi[tm - 1:tm, :]).astype(jnp.int32)
    group_len = (counts + (GROUP_ALIGN - 1)) // GROUP_ALIGN * GROUP_ALIGN
    er = lax.broadcasted_iota(jnp.int32, (LANES, LANES), 0)
    ec = lax.broadcasted_iota(jnp.int32, (LANES, LANES), 1)
    lens8 = jnp.broadcast_to(group_len.astype(F32), (8, LANES)).astype(BF16)
    group_start = jnp.dot(lens8, (er < ec).astype(BF16), preferred_element_type=F32)[0:1, :]
    where_in_tile = group_start + prior
    pos_out = jnp.zeros((tm, LANES), F32)
    gate_out = jnp.zeros((tm, LANES), F32)
    for k in range(TOP_K):
        pk = jnp.sum(jnp.where(hot[k], where_in_tile, 0.0), axis=-1, keepdims=True)
        pos_out = jnp.where(lane == k, pk, pos_out)
        gate_out = jnp.where(lane == k, es[k] / den, gate_out)
    pos_ref[...] = pos_out.astype(jnp.int32)
    post_ref[...] = pos_out.T[0:8, :].astype(jnp.int32)
    gate_ref[...] = gate_out
    start_ref[...] = group_start.astype(jnp.int32)
    len_ref[...] = group_len
    base_ref[...] = carry_ref[...]
    carry_ref[...] = carry_ref[...] + group_len
    total_ref[...] = carry_ref[...]


def outproj_router(o_sb, o_dil, o_fox, g_mix, w_out, x, mod, g_ffn, w_r, b_r):
    b, s, d = x.shape
    tm = min(ROW_TILE, s)
    n = b * s

    def act(w):
        return pl.BlockSpec((None, tm, w), lambda i, j: (i, j, 0))

    def modspec(col):
        return pl.BlockSpec((None, 1, d), lambda i, j: (i, 0, col))

    def const(shape):
        return pl.BlockSpec(shape, lambda i, j: (0, 0))

    per_b = s // tm
    tok = pl.BlockSpec((tm, LANES), lambda i, j: (i * per_b + j, 0))
    per_tile = pl.BlockSpec((None, 1, LANES), lambda i, j: (i * per_b + j, 0, 0))
    ntiles = b * per_b
    tile_tab = jax.ShapeDtypeStruct((ntiles, 1, LANES), jnp.int32)
    return pl.pallas_call(
        _outproj_router_kernel,
        grid=(b, per_b),
        in_specs=[act(o_sb.shape[-1]), act(o_dil.shape[-1]), act(o_fox.shape[-1]),
                  const((1, d)), const(w_out.shape), act(d),
                  modspec(2), modspec(4), modspec(3),
                  const((1, d)), const((d, 2 * LANES)), const((1, LANES))],
        out_specs=[act(d), pl.BlockSpec((tm, d), lambda i, j: (i * per_b + j, 0)), tok,
                   pl.BlockSpec((None, 8, tm), lambda i, j: (i * per_b + j, 0, 0)), tok,
                   per_tile, per_tile, per_tile, const((1, LANES))],
        out_shape=[jax.ShapeDtypeStruct((b, s, d), F32), jax.ShapeDtypeStruct((n, d), F32),
                   jax.ShapeDtypeStruct((n, LANES), jnp.int32), jax.ShapeDtypeStruct((ntiles, 8, tm), jnp.int32),
                   jax.ShapeDtypeStruct((n, LANES), F32), tile_tab, tile_tab, tile_tab,
                   jax.ShapeDtypeStruct((1, LANES), jnp.int32)],
        scratch_shapes=[pltpu.VMEM((1, LANES), jnp.int32)],
        compiler_params=_params(("arbitrary", "arbitrary")),
        name="outproj_router",
    )(o_sb, o_dil, o_fox, g_mix, w_out, x, mod, mod, mod, g_ffn, w_r, b_r)


GROUP_ALIGN = 8
GROUP_CHUNKS = (512, 256, 128, 64, 32, 16, 8)


def sorted_rows(tm, n_experts):
    return -(-(TOP_K * tm + n_experts * (GROUP_ALIGN - 1)) // 256) * 256


def _for_each_chunk(len_ref, dst_ref, first, count, fn):
    def body(e, _):
        length = len_ref[first + e]
        hbm_row = dst_ref[first + e]
        offset = jnp.int32(0)
        for size in GROUP_CHUNKS:
            hit = (length & size) != 0

            @pl.when(hit)
            def _(offset=offset, hbm_row=hbm_row, size=size):
                fn(size, first + e, pl.multiple_of(offset, GROUP_ALIGN), pl.multiple_of(hbm_row, GROUP_ALIGN))

            step = jnp.where(hit, size, 0)
            offset = offset + step
            hbm_row = hbm_row + step
        return 0

    lax.fori_loop(0, count, body, 0)


def _dispatch_kernel(start_ref, len_ref, dst_ref, tail_len_ref, tail_dst_ref, nused_ref, h_ref, post_ref, xs_hbm,
                     buf_ref, sem, *, ne, te):
    rows, tm = buf_ref.shape[1], h_ref.shape[0]
    nblk = xs_hbm.shape[0] // te
    j = pl.program_id(0)
    slot = j % 2

    @pl.when(j == 0)
    def _():
        zero_rows, zsem = buf_ref.at[1], sem.at[1]
        zero_rows[0:te, :] = jnp.zeros((te, buf_ref.shape[2]), F32)

        def tail(size, e, offset, hbm_row):
            return pltpu.make_async_copy(zero_rows.at[pl.ds(0, size)], xs_hbm.at[pl.ds(hbm_row, size)], zsem)

        def block(i):
            dst = xs_hbm.at[pl.ds(pl.multiple_of(i * te, te), te)]
            return pltpu.make_async_copy(zero_rows.at[pl.ds(0, te)], dst, zsem)

        _for_each_chunk(tail_len_ref, tail_dst_ref, 0, ne, lambda *a: tail(*a).start())
        lax.fori_loop(nused_ref[0], nblk, lambda i, c: (block(i).start(), c)[1], 0)
        _for_each_chunk(tail_len_ref, tail_dst_ref, 0, ne, lambda *a: tail(*a).wait())
        lax.fori_loop(nused_ref[0], nblk, lambda i, c: (block(i).wait(), c)[1], 0)

    r = lax.broadcasted_iota(jnp.int32, (rows, tm), 0)
    pick = r == post_ref[0:1, :]
    for k in range(1, TOP_K):
        pick = pick | (r == post_ref[k:k + 1, :])
    buf_ref[slot] = jnp.dot(jnp.where(pick, 1.0, 0.0).astype(BF16), h_ref[...].astype(BF16),
                            preferred_element_type=F32)

    def copy(which):
        def make(size, g, offset, hbm_row):
            buf_row = pl.multiple_of(start_ref[g] + offset, GROUP_ALIGN)
            return pltpu.make_async_copy(buf_ref.at[which, pl.ds(buf_row, size)], xs_hbm.at[pl.ds(hbm_row, size)],
                                         sem.at[which])
        return make

    @pl.when(j > 0)
    def _():
        _for_each_chunk(len_ref, dst_ref, (j - 1) * ne, ne, lambda *a: copy(1 - slot)(*a).wait())

    _for_each_chunk(len_ref, dst_ref, j * ne, ne, lambda *a: copy(slot)(*a).start())

    @pl.when(j == pl.num_programs(0) - 1)
    def _():
        _for_each_chunk(len_ref, dst_ref, j * ne, ne, lambda *a: copy(slot)(*a).wait())


def dispatch_rows(tabs, tails, nused, h, post, cap, ne, te):
    n, d = h.shape
    tm = post.shape[-1]
    return pl.pallas_call(
        functools.partial(_dispatch_kernel, ne=ne, te=te),
        grid_spec=pltpu.PrefetchScalarGridSpec(
            num_scalar_prefetch=6,
            grid=(n // tm,),
            in_specs=[pl.BlockSpec((tm, d), lambda i, *_: (i, 0)),
                      pl.BlockSpec((None, 8, tm), lambda i, *_: (i, 0, 0))],
            out_specs=pl.BlockSpec(memory_space=pl.ANY),
            scratch_shapes=[pltpu.VMEM((2, sorted_rows(tm, ne), d), F32), pltpu.SemaphoreType.DMA((2,))]),
        out_shape=jax.ShapeDtypeStruct((cap, d), F32),
        compiler_params=pltpu.CompilerParams(dimension_semantics=("arbitrary",), has_side_effects=True,
                                             vmem_limit_bytes=VMEM_LIMIT),
        name="dispatch_rows",
    )(*tabs, *tails, nused, h, post)


def _expert_kernel(blk_e_ref, nused_ref, xs_ref, wgu_ref, bgu_ref, wdn_ref, bdn_ref, o_ref, wgu_bf, wdn_bf):
    i = pl.program_id(0)
    used = i < nused_ref[0]

    @pl.when(jnp.logical_not(used))
    def _():
        o_ref[...] = jnp.zeros_like(o_ref)

    @pl.when(used & ((i == 0) | (blk_e_ref[i] != blk_e_ref[jnp.maximum(i - 1, 0)])))
    def _():
        wgu_bf[...] = wgu_ref[...].astype(BF16)
        wdn_bf[...] = wdn_ref[...].astype(BF16)

    @pl.when(used)
    def _():
        f = wdn_ref.shape[0]
        gu = jnp.dot(xs_ref[...].astype(BF16), wgu_bf[...], preferred_element_type=F32) + bgu_ref[...]
        gate = jnp.minimum(gu[:, :f], SWIGLU_LIMIT)
        up = jnp.clip(gu[:, f:], -SWIGLU_LIMIT, SWIGLU_LIMIT)
        act = (up + 1.0) * (gate * jax.nn.sigmoid(SWIGLU_ALPHA * gate))
        o_ref[...] = jnp.dot(act.astype(BF16), wdn_bf[...], preferred_element_type=F32) + bdn_ref[...]


def expert_ffn(blk_e, nused, xs, layer, w_gu, b_gu, w_dn, b_dn, te):
    cap, d = xs.shape
    f2 = w_gu.shape[-1]
    f = f2 // 2
    nblk = cap // te

    def rows(i, be, nu):
        return (jnp.minimum(i, nu[0] - 1), 0)

    def per_expert(i, be, nu):
        return (layer, be[i], 0, 0)

    return pl.pallas_call(
        _expert_kernel,
        grid_spec=pltpu.PrefetchScalarGridSpec(
            num_scalar_prefetch=2,
            grid=(nblk,),
            in_specs=[pl.BlockSpec((te, d), rows),
                      pl.BlockSpec((None, None, d, f2), per_expert),
                      pl.BlockSpec((None, None, 1, f2), per_expert),
                      pl.BlockSpec((None, None, f, d), per_expert),
                      pl.BlockSpec((None, None, 1, d), per_expert)],
            out_specs=pl.BlockSpec((te, d), lambda i, be, nu: (i, 0)),
            scratch_shapes=[pltpu.VMEM((d, f2), BF16), pltpu.VMEM((f, d), BF16)]),
        out_shape=jax.ShapeDtypeStruct((cap, d), F32),
        compiler_params=_params(("arbitrary",), EXPERT_VMEM_LIMIT),
        name="expert_ffn",
    )(blk_e, nused, xs, w_gu, b_gu, w_dn, b_dn)


def _combine_kernel(start_ref, len_ref, dst_ref, ys_hbm, pos_ref, gate_ref, x_ref, ga_ref, xo_ref, buf_ref, sem,
                    *, ne):
    j = pl.program_id(0)
    slot = j % 2

    def copy(which):
        def make(size, g, offset, hbm_row):
            buf_row = pl.multiple_of(start_ref[g] + offset, GROUP_ALIGN)
            return pltpu.make_async_copy(ys_hbm.at[pl.ds(hbm_row, size)], buf_ref.at[which, pl.ds(buf_row, size)],
                                         sem.at[which])
        return make

    @pl.when(j == 0)
    def _():
        buf_ref[...] = jnp.zeros_like(buf_ref)
        _for_each_chunk(len_ref, dst_ref, 0, ne, lambda *a: copy(0)(*a).start())

    @pl.when(j + 1 < pl.num_programs(0))
    def _():
        _for_each_chunk(len_ref, dst_ref, (j + 1) * ne, ne, lambda *a: copy(1 - slot)(*a).start())

    rows, tm = buf_ref.shape[1], pos_ref.shape[0]
    r = lax.broadcasted_iota(jnp.int32, (tm, rows), 1)
    pos, g = pos_ref[...], gate_ref[...]
    weights = jnp.zeros((tm, rows), F32)
    for k in range(TOP_K):
        weights = jnp.where(r == pos[:, k:k + 1], g[:, k:k + 1], weights)
    _for_each_chunk(len_ref, dst_ref, j * ne, ne, lambda *a: copy(slot)(*a).wait())
    y = jnp.dot(weights.astype(BF16), buf_ref[slot].astype(BF16), preferred_element_type=F32)
    xo_ref[...] = x_ref[...] + ga_ref[...] * y


def combine_rows(tabs, ys, pos, gates, x, mod, ne):
    b, s, d = x.shape
    tm = min(ROW_TILE, s)
    per_b = s // tm
    return pl.pallas_call(
        functools.partial(_combine_kernel, ne=ne),
        grid_spec=pltpu.PrefetchScalarGridSpec(
            num_scalar_prefetch=3,
            grid=(b * per_b,),
            in_specs=[pl.BlockSpec(memory_space=pl.ANY),
                      pl.BlockSpec((tm, LANES), lambda i, *_: (i, 0)),
                      pl.BlockSpec((tm, LANES), lambda i, *_: (i, 0)),
                      pl.BlockSpec((None, tm, d), lambda i, *_: (i // per_b, i % per_b, 0)),
                      pl.BlockSpec((None, 1, d), lambda i, *_: (i // per_b, 0, 5))],
            out_specs=pl.BlockSpec((None, tm, d), lambda i, *_: (i // per_b, i % per_b, 0)),
            scratch_shapes=[pltpu.VMEM((2, sorted_rows(tm, ne), d), F32), pltpu.SemaphoreType.DMA((2,))]),
        out_shape=jax.ShapeDtypeStruct((b, s, d), F32),
        compiler_params=_params(("arbitrary",), EXPERT_VMEM_LIMIT),
        name="combine_rows",
    )(*tabs, ys, pos, gates, x, mod)


def _final_norm_kernel(x_ref, g_ref, o_ref):
    o_ref[...] = _rms_rows(x_ref[...]) * g_ref[...]


def final_norm(x, g):
    b, s, d = x.shape
    tm = min(ROW_TILE, s)
    spec = pl.BlockSpec((None, tm, d), lambda i, j: (i, j, 0))
    return pl.pallas_call(
        _final_norm_kernel,
        grid=(b, s // tm),
        in_specs=[spec, pl.BlockSpec((1, d), lambda i, j: (0, 0))],
        out_specs=spec,
        out_shape=jax.ShapeDtypeStruct((b, s, d), F32),
        compiler_params=_params(("arbitrary", "arbitrary")),
        name="final_norm",
    )(x, g)


def _routing_tables(starts, lens, bases, totals, n_experts, te, nblk):
    totals = totals[0, :n_experts]
    padded = (totals + te - 1) // te * te
    ends = jnp.cumsum(padded)
    first_row = ends - padded
    dst = (first_row[None, :] + bases[:, 0, :n_experts]).reshape(-1).astype(jnp.int32)
    tabs = (starts[:, 0, :n_experts].reshape(-1), lens[:, 0, :n_experts].reshape(-1), dst)
    tails = ((padded - totals).astype(jnp.int32), (first_row + totals).astype(jnp.int32))
    nused = (ends[-1] // te).astype(jnp.int32)
    blk = jnp.arange(nblk, dtype=jnp.int32)
    blk_e = jnp.sum((ends[None, :] <= (blk * te)[:, None]).astype(jnp.int32), axis=1)
    blk_e = jnp.minimum(blk_e, n_experts - 1)
    blk_e = jnp.where(blk < nused, blk_e, blk_e[nused - 1])
    return tabs, tails, blk_e, nused.reshape(1)


def _split_bf16(w):
    hi = w.astype(BF16)
    return jnp.concatenate([hi, (w - hi.astype(F32)).astype(BF16)], axis=1)


def _pad_lanes(a, fill=0.0):
    return jnp.pad(a, [(0, 0)] * (a.ndim - 1) + [(0, LANES - a.shape[-1])], constant_values=fill)


def kernel(x, c, positions, w_mod, b_mod, g_attn, w_in, b_forget, g_mix, w_out, g_ffn, w_router, b_router,
           w_gate_up, b_gate_up, w_down, b_down, g_final):
    b, s, d = x.shape
    depth = w_mod.shape[0]
    n_experts = w_router.shape[-1]
    d_mix = w_out.shape[1]
    n = b * s
    te = min(EXPERT_TILE, n)
    ntiles = n // min(ROW_TILE, s)
    nblk = -(-(n * TOP_K + ntiles * n_experts * (GROUP_ALIGN - 1) + n_experts * (te - 1)) // te)
    cap = nblk * te

    mod_all = modulation(c, w_mod, b_mod)
    cos, sin = rope_tables(positions)
    for layer in range(depth):
        mod = mod_all[layer].reshape(b, 1, N_MOD * d)
        w_qkv = family_major_qkv_weight(w_in[layer], d_mix)
        w_f = _pad_lanes(w_in[layer, :, 3 * d_mix:]).astype(BF16)
        b_f = _pad_lanes(b_forget[layer][None, :])
        qkv_sb, qkv_dil, qkv_fox, logf = in_projection(x, mod, g_attn[layer][None, :], w_qkv, w_f, b_f)
        qaug, kaug = forget_cumsum(logf, N_HEADS_FOX)
        o_sb = sb_attention(qkv_sb)
        o_dil = dil_attention(qkv_dil, cos, sin)
        o_fox = fox_attention(qkv_fox, qaug, kaug)
        x, h2, pos, post, gates, starts, lens, bases, totals = outproj_router(
            o_sb, o_dil, o_fox, g_mix[layer][None, :], w_out[layer].astype(BF16), x, mod, g_ffn[layer][None, :],
            _split_bf16(_pad_lanes(w_router[layer])), _pad_lanes(b_router[layer][None, :], NEG_BIG))
        tabs, tails, blk_e, nused = _routing_tables(starts, lens, bases, totals, n_experts, te, nblk)
        xs = dispatch_rows(tabs, tails, nused, h2, post, cap, n_experts, te)
        ys = expert_ffn(blk_e, nused, xs, layer, w_gate_up, b_gate_up[:, :, None, :], w_down,
                        b_down[:, :, None, :], te)
        x = combine_rows(tabs, ys, pos, gates, x, mod, n_experts)
    return final_norm(x, g_final[None, :])
```

```python
import functools

import numpy as np
import jax
import jax.numpy as jnp
from jax import lax
from jax.experimental import pallas as pl
from jax.experimental.pallas import tpu as pltpu

F32 = jnp.float32
BF16 = jnp.bfloat16
HIGHEST = lax.Precision.HIGHEST

HEAD_DIM = 64
LANES = 128
N_HEADS_SB = 4
N_HEADS_DIL = 6
N_HEADS_FOX = 6
DIL_PATTERNS = ((128, 1), (512, 4), (2048, 16))
ROPE_THETA = 10000.0
TOP_K = 4
SWIGLU_LIMIT = 7.0
SWIGLU_ALPHA = 1.702
N_MOD = 6
EPS = 1e-6
NEG_BIG = -1e30
SB_TILE = 256
SOFTMAX_TILE = 512
ROW_TILE = 512
EXPERT_TILE = 512
VMEM_LIMIT = 48 * 1024 * 1024
EXPERT_VMEM_LIMIT = 60 * 1024 * 1024


def _params(sem, vmem=VMEM_LIMIT):
    return pltpu.CompilerParams(dimension_semantics=sem, vmem_limit_bytes=vmem)


def _rms_rows(x):
    return x * lax.rsqrt(jnp.mean(x * x, axis=-1, keepdims=True) + EPS)


def _softplus(z):
    return jnp.maximum(z, 0.0) + jnp.log(1.0 + jnp.exp(-jnp.abs(z)))


def _mod_kernel(c_ref, w_ref, b_ref, o_ref):
    c = c_ref[...]
    ca = c * jax.nn.sigmoid(c)
    o_ref[...] = jnp.dot(ca, w_ref[...], preferred_element_type=F32, precision=HIGHEST) + b_ref[...]


def modulation(c, w_mod, b_mod):
    depth, d, n6 = w_mod.shape
    b = c.shape[0]
    tn = min(n6, 1536)
    return pl.pallas_call(
        _mod_kernel,
        grid=(depth, n6 // tn),
        in_specs=[pl.BlockSpec((b, d), lambda l, j: (0, 0)),
                  pl.BlockSpec((None, d, tn), lambda l, j: (l, 0, j)),
                  pl.BlockSpec((None, 1, tn), lambda l, j: (l, 0, j))],
        out_specs=pl.BlockSpec((None, b, tn), lambda l, j: (l, 0, j)),
        out_shape=jax.ShapeDtypeStruct((depth, b, n6), F32),
        compiler_params=_params(("arbitrary", "arbitrary")),
        name="modulation",
    )(c, w_mod, b_mod.reshape(depth, 1, n6))


def _rope_table_kernel(pos_ref, invf_ref, cos_ref, sin_ref):
    ang = pos_ref[...].astype(F32) * invf_ref[...]
    lane = lax.broadcasted_iota(jnp.int32, ang.shape, 1)
    first_half = (lane % HEAD_DIM) < (HEAD_DIM // 2)
    s = jnp.sin(ang)
    cos_ref[...] = jnp.cos(ang)
    sin_ref[...] = jnp.where(first_half, -s, s)


def rope_tables(positions):
    b, s = positions.shape
    half = HEAD_DIM // 2
    inv_freq = ROPE_THETA ** (-np.arange(0, HEAD_DIM, 2, dtype=np.float64) / HEAD_DIM)
    invf = jnp.asarray(np.tile(inv_freq, LANES // half)[None, :], F32)
    return pl.pallas_call(
        _rope_table_kernel,
        grid=(b,),
        in_specs=[pl.BlockSpec((None, s, 1), lambda i: (i, 0, 0)),
                  pl.BlockSpec((1, LANES), lambda i: (0, 0))],
        out_specs=[pl.BlockSpec((None, s, LANES), lambda i: (i, 0, 0))] * 2,
        out_shape=[jax.ShapeDtypeStruct((b, s, LANES), F32)] * 2,
        compiler_params=_params(("arbitrary",)),
        name="rope_tables",
    )(positions.reshape(b, s, 1), invf)


def _rope(x, cos, sin_signed):
    lane = lax.broadcasted_iota(jnp.int32, x.shape, 1)
    first_half = (lane % HEAD_DIM) < (HEAD_DIM // 2)
    half = HEAD_DIM // 2
    partner = jnp.where(first_half, pltpu.roll(x, LANES - half, 1), pltpu.roll(x, half, 1))
    return x * cos + partner * sin_signed


FAMILY_HEADS = (N_HEADS_SB, N_HEADS_DIL, N_HEADS_FOX)


def family_major_qkv_weight(w_in_layer, d_mix):
    wq = w_in_layer[:, :d_mix] * (HEAD_DIM ** -0.5)
    wk = w_in_layer[:, d_mix:2 * d_mix]
    wv = w_in_layer[:, 2 * d_mix:3 * d_mix]
    cols, lo = [], 0
    for nh in FAMILY_HEADS:
        hi = lo + nh * HEAD_DIM
        cols += [wq[:, lo:hi], wk[:, lo:hi], wv[:, lo:hi]]
        lo = hi
    return jnp.concatenate(cols, axis=1).astype(BF16)


def _inproj_kernel(x_ref, sh_ref, sc_ref, g_ref, wqkv_ref, wf_ref, bf_ref, sb_ref, dil_ref, fox_ref, logf_ref):
    x = x_ref[...]
    h = _rms_rows(x) * g_ref[...] * (1.0 + sc_ref[...]) + sh_ref[...]
    hb = h.astype(BF16)
    lo = 0
    for o_ref in (sb_ref, dil_ref, fox_ref):
        w = o_ref.shape[-1]
        o_ref[...] = jnp.dot(hb, wqkv_ref[:, lo:lo + w], preferred_element_type=F32).astype(BF16)
        lo += w
    f = jnp.dot(hb, wf_ref[...], preferred_element_type=F32) + bf_ref[...]
    logf_ref[...] = -_softplus(-f)


def in_projection(x, mod, g, w_qkv, w_f, b_f):
    b, s, d = x.shape
    d_mix = w_qkv.shape[1] // 3
    tm = min(ROW_TILE, s)
    widths = [3 * nh * HEAD_DIM for nh in FAMILY_HEADS]
    return pl.pallas_call(
        _inproj_kernel,
        grid=(b, s // tm),
        in_specs=[pl.BlockSpec((None, tm, d), lambda i, j: (i, j, 0)),
                  pl.BlockSpec((None, 1, d), lambda i, j: (i, 0, 0)),
                  pl.BlockSpec((None, 1, d), lambda i, j: (i, 0, 1)),
                  pl.BlockSpec((1, d), lambda i, j: (0, 0)),
                  pl.BlockSpec((d, 3 * d_mix), lambda i, j: (0, 0)),
                  pl.BlockSpec((d, LANES), lambda i, j: (0, 0)),
                  pl.BlockSpec((1, LANES), lambda i, j: (0, 0))],
        out_specs=[pl.BlockSpec((None, tm, w), lambda i, j: (i, j, 0)) for w in widths + [LANES]],
        out_shape=[jax.ShapeDtypeStruct((b, s, w), BF16) for w in widths]
        + [jax.ShapeDtypeStruct((b, s, LANES), F32)],
        compiler_params=_params(("arbitrary", "arbitrary")),
        name="in_projection",
    )(x, mod, mod, g, w_qkv, w_f, b_f)


AUG_TERMS = 3


def _own_half(shape, h):
    lane = lax.broadcasted_iota(jnp.int32, shape, len(shape) - 1)
    return (lane < HEAD_DIM) if h % 2 == 0 else (lane >= HEAD_DIM)


def _spare_lane(shape, h):
    lane = lax.broadcasted_iota(jnp.int32, shape, len(shape) - 1)
    return lane - (HEAD_DIM if h % 2 == 0 else 0)


def _cumsum_kernel(logf_ref, qaug_ref, kaug_ref, *, nh):
    s = logf_ref.shape[0]
    r = lax.broadcasted_iota(jnp.int32, (LANES, LANES), 0)
    c = lax.broadcasted_iota(jnp.int32, (LANES, LANES), 1)
    tri = (c <= r).astype(F32)
    carry = jnp.zeros((1, LANES), F32)
    for i in range(s // LANES):
        rows = slice(i * LANES, (i + 1) * LANES)
        cs = jnp.dot(tri, logf_ref[rows, :], preferred_element_type=F32, precision=HIGHEST) + carry
        carry = cs[LANES - 1:LANES, :]
        for h in range(nh):
            full = jnp.broadcast_to(cs[:, h:h + 1], (LANES, LANES))
            parts, rest = [], full
            for _ in range(AUG_TERMS):
                piece = rest.astype(BF16).astype(F32)
                parts.append(piece)
                rest = rest - piece
            j = _spare_lane((LANES, LANES), h)
            qa = jnp.where((j >= AUG_TERMS) & (j < 2 * AUG_TERMS), 1.0, 0.0)
            ka = jnp.where((j >= 0) & (j < AUG_TERMS), 1.0, 0.0)
            for n, piece in enumerate(parts):
                qa = jnp.where(j == n, piece, qa)
                ka = jnp.where(j == AUG_TERMS + n, -piece, ka)
            qaug_ref[rows, h * LANES:(h + 1) * LANES] = qa.astype(BF16)
            kaug_ref[rows, h * LANES:(h + 1) * LANES] = ka.astype(BF16)


def forget_cumsum(logf, nh):
    b, s, _ = logf.shape
    out = pl.BlockSpec((None, s, nh * LANES), lambda i: (i, 0, 0))
    return pl.pallas_call(
        functools.partial(_cumsum_kernel, nh=nh),
        grid=(b,),
        in_specs=[pl.BlockSpec((None, s, LANES), lambda i: (i, 0, 0))],
        out_specs=[out, out],
        out_shape=[jax.ShapeDtypeStruct((b, s, nh * LANES), BF16)] * 2,
        compiler_params=_params(("arbitrary",)),
        name="forget_cumsum",
    )(logf)


def _tile_iotas(t):
    return lax.broadcasted_iota(jnp.int32, (t, t), 0), lax.broadcasted_iota(jnp.int32, (t, t), 1)


def _pair_tile(ref, rows, h):
    p = h // 2
    return ref[rows, p * LANES:(p + 1) * LANES]


def _keep_own(tile, h, other=0.0):
    return jnp.where(_own_half(tile.shape, h), tile.astype(F32), other).astype(BF16)


def _qk(q, k):
    return lax.dot_general(q, k, (((1,), (1,)), ((), ())), preferred_element_type=F32)


def _merge_pairs(per_head, o_ref):
    tiles = [jnp.where(_own_half(per_head[h].shape, h), per_head[h], per_head[h + 1])
             for h in range(0, len(per_head), 2)]
    o_ref[...] = jnp.concatenate(tiles, axis=-1).astype(o_ref.dtype)


def _sb_kernel(q_ref, k_ref, v_ref, o_ref, *, t, nh):
    qi = pl.program_id(1)
    row, col = _tile_iotas(t)
    strict = col < row
    upper = (row > col).astype(BF16)
    qs = [_keep_own(_pair_tile(q_ref, slice(None), h), h) for h in range(nh)]

    def tile(ki, carries, diag):
        rows = pl.ds(pl.multiple_of(ki * t, t), t)
        heads = range(nh)
        zs = [_qk(qs[h], _pair_tile(k_ref, rows, h)) for h in heads]
        sps = [_softplus(z) for z in zs]
        log_nots = [jnp.where(strict, -sp, 0.0) if diag else -sp for sp in sps]
        betweens = [jnp.dot(ln.astype(BF16), upper, preferred_element_type=F32) for ln in log_nots]
        out = []
        for h in heads:
            suffix, acc = carries[h]
            w = jnp.exp(zs[h] - sps[h] + betweens[h] + suffix)
            if diag:
                w = jnp.where(strict, w, 0.0)
            acc = acc + jnp.dot(w.astype(BF16), _pair_tile(v_ref, rows, h), preferred_element_type=F32)
            out.append((suffix + betweens[h][:, 0:1] + log_nots[h][:, 0:1], acc))
        return tuple(out)

    init = tuple((jnp.zeros((t, 1), F32), jnp.zeros((t, LANES), F32)) for _ in range(nh))
    carries = tile(qi, init, True)
    carries = lax.fori_loop(0, qi, lambda j, cr: tile(qi - 1 - j, cr, False), carries)
    _merge_pairs([cr[1] for cr in carries], o_ref)


def _online_softmax_steps(scores, values, carries):
    m_news = [jnp.maximum(m, jnp.max(s, axis=-1, keepdims=True)) for s, (m, _) in zip(scores, carries)]
    ps = [jnp.exp(s - m_new).astype(BF16) for s, m_new in zip(scores, m_news)]
    return tuple((m_new, jnp.exp(m - m_new) * acc + jnp.dot(p, v, preferred_element_type=F32))
                 for p, v, m_new, (m, acc) in zip(ps, values, m_news, carries))


def _softmax_init(t, nh):
    return tuple((jnp.full((t, 1), NEG_BIG, F32), jnp.zeros((t, LANES), F32)) for _ in range(nh))


def _softmax_finish(carries, o_ref):
    outs = []
    for h, (_, acc) in enumerate(carries):
        denom = jnp.sum(jnp.where(_spare_lane(acc.shape, h) == 0, acc, 0.0), axis=-1, keepdims=True)
        outs.append(acc / denom)
    _merge_pairs(outs, o_ref)


def _pad_values(v_ref, vpad_ref, nh):
    for h in range(nh):
        tile = _pair_tile(v_ref, slice(None), h)
        ones_col = jnp.where(_spare_lane(tile.shape, h) == 0, 1.0, 0.0)
        vpad_ref[:, h * LANES:(h + 1) * LANES] = _keep_own(tile, h, ones_col)


def _fox_kernel(q_ref, k_ref, v_ref, qaug_ref, kaug_ref, o_ref, kpad_ref, vpad_ref, *, t, nh):
    qi = pl.program_id(1)
    row, col = _tile_iotas(t)

    @pl.when(qi == 0)
    def _():
        _pad_values(v_ref, vpad_ref, nh)
        for h in range(nh):
            lanes = slice(h * LANES, (h + 1) * LANES)
            kpad_ref[:, lanes] = _keep_own(_pair_tile(k_ref, slice(None), h), h, kaug_ref[:, lanes].astype(F32))

    qs = [_keep_own(_pair_tile(q_ref, slice(None), h), h, qaug_ref[:, h * LANES:(h + 1) * LANES].astype(F32))
          for h in range(nh)]

    def tile(ki, carries, diag):
        rows = pl.ds(pl.multiple_of(ki * t, t), t)
        lanes = [slice(h * LANES, (h + 1) * LANES) for h in range(nh)]
        scores = [_qk(qs[h], kpad_ref[rows, lanes[h]]) for h in range(nh)]
        if diag:
            scores = [jnp.where(col <= row, s, NEG_BIG) for s in scores]
        return _online_softmax_steps(scores, [vpad_ref[rows, ln] for ln in lanes], carries)

    carries = lax.fori_loop(0, qi, lambda ki, cr: tile(ki, cr, False), _softmax_init(t, nh))
    _softmax_finish(tile(qi, carries, True), o_ref)


DIL_NEAR_TILES = 3


def _dilation_log_count(d):
    cnt = None
    for window, dil in DIL_PATTERNS:
        hit = (d >= 0) & (d <= window) & ((d & (dil - 1)) == 0)
        term = jnp.where(hit, 1.0, 0.0)
        cnt = term if cnt is None else cnt + term
    return jnp.where(cnt > 0.0, jnp.log(jnp.maximum(cnt, 1.0)), NEG_BIG)


def _dil_kernel(q_ref, k_ref, v_ref, cos_ref, sin_ref, o_ref, kr_ref, vpad_ref, bias_ref, *, t, nh):
    qi = pl.program_id(1)

    @pl.when(qi == 0)
    def _():
        _pad_values(v_ref, vpad_ref, nh)
        for p in range(nh // 2):
            lanes = slice(p * LANES, (p + 1) * LANES)
            kr_ref[:, lanes] = _rope(k_ref[:, lanes].astype(F32), cos_ref[...], sin_ref[...]).astype(BF16)
        row, col = _tile_iotas(t)
        for delta in range(DIL_NEAR_TILES + 1):
            bias_ref[delta] = _dilation_log_count(delta * t + row - col)

    qrows = pl.ds(pl.multiple_of(qi * t, t), t)
    cos, sin = cos_ref[qrows, :], sin_ref[qrows, :]
    qs = [_keep_own(_rope(_pair_tile(q_ref, slice(None), h).astype(F32), cos, sin), h) for h in range(nh)]

    def tile(ki, carries):
        rows = pl.ds(pl.multiple_of(ki * t, t), t)
        bias = bias_ref[jnp.minimum(qi - ki, DIL_NEAR_TILES)]
        scores = [_qk(qs[h], _pair_tile(kr_ref, rows, h)) + bias for h in range(nh)]
        values = [vpad_ref[rows, h * LANES:(h + 1) * LANES] for h in range(nh)]
        return _online_softmax_steps(scores, values, carries)

    _softmax_finish(lax.fori_loop(0, qi + 1, tile, _softmax_init(t, nh)), o_ref)


def _attention_call(kernel, qkv, n_heads, t, extra_inputs, extra_specs, scratch, name):
    b, s, w3 = qkv.shape
    w = w3 // 3
    return pl.pallas_call(
        functools.partial(kernel, t=t, nh=n_heads),
        grid=(b, s // t),
        in_specs=[pl.BlockSpec((None, t, w), lambda i, j: (i, j, 0)),
                  pl.BlockSpec((None, s, w), lambda i, j: (i, 0, 1)),
                  pl.BlockSpec((None, s, w), lambda i, j: (i, 0, 2))] + extra_specs,
        out_specs=pl.BlockSpec((None, t, w), lambda i, j: (i, j, 0)),
        out_shape=jax.ShapeDtypeStruct((b, s, w), BF16),
        scratch_shapes=scratch,
        compiler_params=_params(("arbitrary", "arbitrary")),
        name=name,
    )(qkv, qkv, qkv, *extra_inputs)


def sb_attention(qkv):
    return _attention_call(_sb_kernel, qkv, N_HEADS_SB, min(SB_TILE, qkv.shape[1]), [], [], [], "sb_attention")


def dil_attention(qkv, cos, sin):
    s = qkv.shape[1]
    t = min(SOFTMAX_TILE, s)
    widest_window, widest_dil = DIL_PATTERNS[-1]
    assert s - 1 <= widest_window and t % widest_dil == 0
    assert all(window < (DIL_NEAR_TILES - 1) * t + 1 for window, _ in DIL_PATTERNS[:-1])
    tab = pl.BlockSpec((None, s, LANES), lambda i, j: (i, 0, 0))
    scratch = [pltpu.VMEM((s, qkv.shape[-1] // 3), BF16), pltpu.VMEM((s, N_HEADS_DIL * LANES), BF16),
               pltpu.VMEM((DIL_NEAR_TILES + 1, t, t), F32)]
    return _attention_call(_dil_kernel, qkv, N_HEADS_DIL, t, [cos, sin], [tab, tab], scratch, "dil_attention")


def fox_attention(qkv, qaug, kaug):
    s = qkv.shape[1]
    t = min(SOFTMAX_TILE, s)
    wide = N_HEADS_FOX * LANES
    specs = [pl.BlockSpec((None, t, wide), lambda i, j: (i, j, 0)),
             pl.BlockSpec((None, s, wide), lambda i, j: (i, 0, 0))]
    scratch = [pltpu.VMEM((s, wide), BF16), pltpu.VMEM((s, wide), BF16)]
    return _attention_call(_fox_kernel, qkv, N_HEADS_FOX, t, [qaug, kaug], specs, scratch, "fox_attention")


def _outproj_router_kernel(osb_ref, odil_ref, ofox_ref, gmix_ref, wout_ref, x_ref, ga_ref, sc_ref, sh_ref,
                           gffn_ref, wr_ref, br_ref,
                           xo_ref, h2_ref, pos_ref, post_ref, gate_ref, start_ref, len_ref, base_ref, total_ref,
                           carry_ref):
    first = (pl.program_id(0) == 0) & (pl.program_id(1) == 0)

    @pl.when(first)
    def _():
        carry_ref[...] = jnp.zeros_like(carry_ref)

    a = None
    lo = 0
    for o_ref in (osb_ref, odil_ref, ofox_ref):
        w = o_ref.shape[-1]
        on = (_rms_rows(o_ref[...].astype(F32)) * gmix_ref[:, lo:lo + w]).astype(BF16)
        part = jnp.dot(on, wout_ref[lo:lo + w, :], preferred_element_type=F32)
        a = part if a is None else a + part
        lo += w
    xn = x_ref[...] + ga_ref[...] * a
    xo_ref[...] = xn
    h2 = _rms_rows(xn) * gffn_ref[...] * (1.0 + sc_ref[...]) + sh_ref[...]
    h2_ref[...] = h2

    h_hi = h2.astype(BF16)
    h_lo = (h2 - h_hi.astype(F32)).astype(BF16)
    both = jnp.dot(h_hi, wr_ref[...], preferred_element_type=F32)
    logits = (both[:, :LANES] + both[:, LANES:]
              + jnp.dot(h_lo, wr_ref[:, :LANES], preferred_element_type=F32) + br_ref[...])
    tm = logits.shape[0]
    lane = lax.broadcasted_iota(jnp.int32, (tm, LANES), 1)
    vals, idxs = [], []
    rest = logits
    for _ in range(TOP_K):
        m = jnp.max(rest, axis=-1, keepdims=True)
        ik = jnp.min(jnp.where(rest == m, lane, LANES), axis=-1, keepdims=True)
        vals.append(m)
        idxs.append(ik)
        rest = jnp.where(lane == ik, -jnp.inf, rest)
    es = [jnp.exp(v - vals[0]) for v in vals]
    den = es[0] + es[1] + es[2] + es[3]

    hot = [(lane == ik) for ik in idxs]
    multi = jnp.where(hot[0] | hot[1] | hot[2] | hot[3], 1.0, 0.0)
    r = lax.broadcasted_iota(jnp.int32, (tm, tm), 0)
    c = lax.broadcasted_iota(jnp.int32, (tm, tm), 1)
    before = (c < r).astype(BF16)
    prior = jnp.dot(before, multi.astype(BF16), preferred_element_type=F32)
    counts = (prior[tm - 1:tm, :] + multi[tm - 1:tm, :]).astype(jnp.int32)
    group_len = (counts + (GROUP_ALIGN - 1)) // GROUP_ALIGN * GROUP_ALIGN
    er = lax.broadcasted_iota(jnp.int32, (LANES, LANES), 0)
    ec = lax.broadcasted_iota(jnp.int32, (LANES, LANES), 1)
    lens8 = jnp.broadcast_to(group_len.astype(F32), (8, LANES)).astype(BF16)
    group_start = jnp.dot(lens8, (er < ec).astype(BF16), preferred_element_type=F32)[0:1, :]
    where_in_tile = group_start + prior
    pos_out = jnp.zeros((tm, LANES), F32)
    gate_out = jnp.zeros((tm, LANES), F32)
    for k in range(TOP_K):
        pk = jnp.sum(jnp.where(hot[k], where_in_tile, 0.0), axis=-1, keepdims=True)
        pos_out = jnp.where(lane == k, pk, pos_out)
        gate_out = jnp.where(lane == k, es[k] / den, gate_out)
    pos_ref[...] = pos_out.astype(jnp.int32)
    post_ref[...] = pos_out.T[0:8, :].astype(jnp.int32)
    gate_ref[...] = gate_out
    start_ref[...] = group_start.astype(jnp.int32)
    len_ref[...] = group_len
    base_ref[...] = carry_ref[...]
    carry_ref[...] = carry_ref[...] + group_len
    total_ref[...] = carry_ref[...]


def outproj_router(o_sb, o_dil, o_fox, g_mix, w_out, x, mod, g_ffn, w_r, b_r):
    b, s, d = x.shape
    tm = min(ROW_TILE, s)
    n = b * s

    def act(w):
        return pl.BlockSpec((None, tm, w), lambda i, j: (i, j, 0))

    def modspec(col):
        return pl.BlockSpec((None, 1, d), lambda i, j: (i, 0, col))

    def const(shape):
        return pl.BlockSpec(shape, lambda i, j: (0, 0))

    per_b = s // tm
    tok = pl.BlockSpec((tm, LANES), lambda i, j: (i * per_b + j, 0))
    per_tile = pl.BlockSpec((None, 1, LANES), lambda i, j: (i * per_b + j, 0, 0))
    ntiles = b * per_b
    tile_tab = jax.ShapeDtypeStruct((ntiles, 1, LANES), jnp.int32)
    return pl.pallas_call(
        _outproj_router_kernel,
        grid=(b, per_b),
        in_specs=[act(o_sb.shape[-1]), act(o_dil.shape[-1]), act(o_fox.shape[-1]),
                  const((1, d)), const(w_out.shape), act(d),
                  modspec(2), modspec(4), modspec(3),
                  const((1, d)), const((d, 2 * LANES)), const((1, LANES))],
        out_specs=[act(d), pl.BlockSpec((tm, d), lambda i, j: (i * per_b + j, 0)), tok,
                   pl.BlockSpec((None, 8, tm), lambda i, j: (i * per_b + j, 0, 0)), tok,
                   per_tile, per_tile, per_tile, const((1, LANES))],
        out_shape=[jax.ShapeDtypeStruct((b, s, d), F32), jax.ShapeDtypeStruct((n, d), F32),
                   jax.ShapeDtypeStruct((n, LANES), jnp.int32), jax.ShapeDtypeStruct((ntiles, 8, tm), jnp.int32),
                   jax.ShapeDtypeStruct((n, LANES), F32), tile_tab, tile_tab, tile_tab,
                   jax.ShapeDtypeStruct((1, LANES), jnp.int32)],
        scratch_shapes=[pltpu.VMEM((1, LANES), jnp.int32)],
        compiler_params=_params(("arbitrary", "arbitrary")),
        name="outproj_router",
    )(o_sb, o_dil, o_fox, g_mix, w_out, x, mod, mod, mod, g_ffn, w_r, b_r)


GROUP_ALIGN = 8
GROUP_CHUNKS = (512, 256, 128, 64, 32, 16, 8)


def sorted_rows(tm, n_experts):
    return -(-(TOP_K * tm + n_experts * (GROUP_ALIGN - 1)) // 256) * 256


def _for_each_chunk(len_ref, dst_ref, first, count, fn):
    def body(e, _):
        length = len_ref[first + e]
        hbm_row = dst_ref[first + e]
        offset = jnp.int32(0)
        for size in GROUP_CHUNKS:
            hit = (length & size) != 0

            @pl.when(hit)
            def _(offset=offset, hbm_row=hbm_row, size=size):
                fn(size, first + e, pl.multiple_of(offset, GROUP_ALIGN), pl.multiple_of(hbm_row, GROUP_ALIGN))

            step = jnp.where(hit, size, 0)
            offset = offset + step
            hbm_row = hbm_row + step
        return 0

    lax.fori_loop(0, count, body, 0)


def _dispatch_kernel(start_ref, len_ref, dst_ref, tail_len_ref, tail_dst_ref, nused_ref, h_ref, post_ref, xs_hbm,
                     buf_ref, sem, *, ne, te):
    rows, tm = buf_ref.shape[1], h_ref.shape[0]
    nblk = xs_hbm.shape[0] // te
    j = pl.program_id(0)
    slot = j % 2

    @pl.when(j == 0)
    def _():
        zero_rows, zsem = buf_ref.at[1], sem.at[1]
        zero_rows[0:te, :] = jnp.zeros((te, buf_ref.shape[2]), F32)

        def tail(size, e, offset, hbm_row):
            return pltpu.make_async_copy(zero_rows.at[pl.ds(0, size)], xs_hbm.at[pl.ds(hbm_row, size)], zsem)

        def block(i):
            dst = xs_hbm.at[pl.ds(pl.multiple_of(i * te, te), te)]
            return pltpu.make_async_copy(zero_rows.at[pl.ds(0, te)], dst, zsem)

        _for_each_chunk(tail_len_ref, tail_dst_ref, 0, ne, lambda *a: tail(*a).start())
        lax.fori_loop(nused_ref[0], nblk, lambda i, c: (block(i).start(), c)[1], 0)
        _for_each_chunk(tail_len_ref, tail_dst_ref, 0, ne, lambda *a: tail(*a).wait())
        lax.fori_loop(nused_ref[0], nblk, lambda i, c: (block(i).wait(), c)[1], 0)

    r = lax.broadcasted_iota(jnp.int32, (rows, tm), 0)
    pick = r == post_ref[0:1, :]
    for k in range(1, TOP_K):
        pick = pick | (r == post_ref[k:k + 1, :])
    buf_ref[slot] = jnp.dot(jnp.where(pick, 1.0, 0.0).astype(BF16), h_ref[...].astype(BF16),
                            preferred_element_type=F32)

    def copy(which):
        def make(size, g, offset, hbm_row):
            buf_row = pl.multiple_of(start_ref[g] + offset, GROUP_ALIGN)
            return pltpu.make_async_copy(buf_ref.at[which, pl.ds(buf_row, size)], xs_hbm.at[pl.ds(hbm_row, size)],
                                         sem.at[which])
        return make

    @pl.when(j > 0)
    def _():
        _for_each_chunk(len_ref, dst_ref, (j - 1) * ne, ne, lambda *a: copy(1 - slot)(*a).wait())

    _for_each_chunk(len_ref, dst_ref, j * ne, ne, lambda *a: copy(slot)(*a).start())

    @pl.when(j == pl.num_programs(0) - 1)
    def _():
        _for_each_chunk(len_ref, dst_ref, j * ne, ne, lambda *a: copy(slot)(*a).wait())


def dispatch_rows(tabs, tails, nused, h, post, cap, ne, te):
    n, d = h.shape
    tm = post.shape[-1]
    return pl.pallas_call(
        functools.partial(_dispatch_kernel, ne=ne, te=te),
        grid_spec=pltpu.PrefetchScalarGridSpec(
            num_scalar_prefetch=6,
            grid=(n // tm,),
            in_specs=[pl.BlockSpec((tm, d), lambda i, *_: (i, 0)),
                      pl.BlockSpec((None, 8, tm), lambda i, *_: (i, 0, 0))],
            out_specs=pl.BlockSpec(memory_space=pl.ANY),
            scratch_shapes=[pltpu.VMEM((2, sorted_rows(tm, ne), d), F32), pltpu.SemaphoreType.DMA((2,))]),
        out_shape=jax.ShapeDtypeStruct((cap, d), F32),
        compiler_params=pltpu.CompilerParams(dimension_semantics=("arbitrary",), has_side_effects=True,
                                             vmem_limit_bytes=VMEM_LIMIT),
        name="dispatch_rows",
    )(*tabs, *tails, nused, h, post)


def _expert_kernel(blk_e_ref, nused_ref, xs_ref, wgu_ref, bgu_ref, wdn_ref, bdn_ref, o_ref, wgu_bf, wdn_bf):
    i = pl.program_id(0)
    used = i < nused_ref[0]

    @pl.when(jnp.logical_not(used))
    def _():
        o_ref[...] = jnp.zeros_like(o_ref)

    @pl.when(used & ((i == 0) | (blk_e_ref[i] != blk_e_ref[jnp.maximum(i - 1, 0)])))
    def _():
        wgu_bf[...] = wgu_ref[...].astype(BF16)
        wdn_bf[...] = wdn_ref[...].astype(BF16)

    @pl.when(used)
    def _():
        f = wdn_ref.shape[0]
        gu = jnp.dot(xs_ref[...].astype(BF16), wgu_bf[...], preferred_element_type=F32) + bgu_ref[...]
        gate = jnp.minimum(gu[:, :f], SWIGLU_LIMIT)
        up = jnp.clip(gu[:, f:], -SWIGLU_LIMIT, SWIGLU_LIMIT)
        act = (up + 1.0) * (gate * jax.nn.sigmoid(SWIGLU_ALPHA * gate))
        o_ref[...] = jnp.dot(act.astype(BF16), wdn_bf[...], preferred_element_type=F32) + bdn_ref[...]


def expert_ffn(blk_e, nused, xs, layer, w_gu, b_gu, w_dn, b_dn, te):
    cap, d = xs.shape
    f2 = w_gu.shape[-1]
    f = f2 // 2
    nblk = cap // te

    def rows(i, be, nu):
        return (jnp.minimum(i, nu[0] - 1), 0)

    def per_expert(i, be, nu):
        return (layer, be[i], 0, 0)

    return pl.pallas_call(
        _expert_kernel,
        grid_spec=pltpu.PrefetchScalarGridSpec(
            num_scalar_prefetch=2,
            grid=(nblk,),
            in_specs=[pl.BlockSpec((te, d), rows),
                      pl.BlockSpec((None, None, d, f2), per_expert),
                      pl.BlockSpec((None, None, 1, f2), per_expert),
                      pl.BlockSpec((None, None, f, d), per_expert),
                      pl.BlockSpec((None, None, 1, d), per_expert)],
            out_specs=pl.BlockSpec((te, d), lambda i, be, nu: (i, 0)),
            scratch_shapes=[pltpu.VMEM((d, f2), BF16), pltpu.VMEM((f, d), BF16)]),
        out_shape=jax.ShapeDtypeStruct((cap, d), F32),
        compiler_params=_params(("arbitrary",), EXPERT_VMEM_LIMIT),
        name="expert_ffn",
    )(blk_e, nused, xs, w_gu, b_gu, w_dn, b_dn)


def _combine_kernel(start_ref, len_ref, dst_ref, ys_hbm, pos_ref, gate_ref, x_ref, ga_ref, xo_ref, buf_ref, sem,
                    *, ne):
    j = pl.program_id(0)
    slot = j % 2

    def copy(which):
        def make(size, g, offset, hbm_row):
            buf_row = pl.multiple_of(start_ref[g] + offset, GROUP_ALIGN)
            return pltpu.make_async_copy(ys_hbm.at[pl.ds(hbm_row, size)], buf_ref.at[which, pl.ds(buf_row, size)],
                                         sem.at[which])
        return make

    @pl.when(j == 0)
    def _():
        buf_ref[...] = jnp.zeros_like(buf_ref)
        _for_each_chunk(len_ref, dst_ref, 0, ne, lambda *a: copy(0)(*a).start())

    @pl.when(j + 1 < pl.num_programs(0))
    def _():
        _for_each_chunk(len_ref, dst_ref, (j + 1) * ne, ne, lambda *a: copy(1 - slot)(*a).start())

    rows, tm = buf_ref.shape[1], pos_ref.shape[0]
    r = lax.broadcasted_iota(jnp.int32, (tm, rows), 1)
    pos, g = pos_ref[...], gate_ref[...]
    weights = jnp.zeros((tm, rows), F32)
    for k in range(TOP_K):
        weights = jnp.where(r == pos[:, k:k + 1], g[:, k:k + 1], weights)
    _for_each_chunk(len_ref, dst_ref, j * ne, ne, lambda *a: copy(slot)(*a).wait())
    y = jnp.dot(weights.astype(BF16), buf_ref[slot].astype(BF16), preferred_element_type=F32)
    xo_ref[...] = x_ref[...] + ga_ref[...] * y


def combine_rows(tabs, ys, pos, gates, x, mod, ne):
    b, s, d = x.shape
    tm = min(ROW_TILE, s)
    per_b = s // tm
    return pl.pallas_call(
        functools.partial(_combine_kernel, ne=ne),
        grid_spec=pltpu.PrefetchScalarGridSpec(
            num_scalar_prefetch=3,
            grid=(b * per_b,),
            in_specs=[pl.BlockSpec(memory_space=pl.ANY),
                      pl.BlockSpec((tm, LANES), lambda i, *_: (i, 0)),
                      pl.BlockSpec((tm, LANES), lambda i, *_: (i, 0)),
                      pl.BlockSpec((None, tm, d), lambda i, *_: (i // per_b, i % per_b, 0)),
                      pl.BlockSpec((None, 1, d), lambda i, *_: (i // per_b, 0, 5))],
            out_specs=pl.BlockSpec((None, tm, d), lambda i, *_: (i // per_b, i % per_b, 0)),
            scratch_shapes=[pltpu.VMEM((2, sorted_rows(tm, ne), d), F32), pltpu.SemaphoreType.DMA((2,))]),
        out_shape=jax.ShapeDtypeStruct((b, s, d), F32),
        compiler_params=_params(("arbitrary",), EXPERT_VMEM_LIMIT),
        name="combine_rows",
    )(*tabs, ys, pos, gates, x, mod)


def _final_norm_kernel(x_ref, g_ref, o_ref):
    o_ref[...] = _rms_rows(x_ref[...]) * g_ref[...]


def final_norm(x, g):
    b, s, d = x.shape
    tm = min(ROW_TILE, s)
    spec = pl.BlockSpec((None, tm, d), lambda i, j: (i, j, 0))
    return pl.pallas_call(
        _final_norm_kernel,
        grid=(b, s // tm),
        in_specs=[spec, pl.BlockSpec((1, d), lambda i, j: (0, 0))],
        out_specs=spec,
        out_shape=jax.ShapeDtypeStruct((b, s, d), F32),
        compiler_params=_params(("arbitrary", "arbitrary")),
        name="final_norm",
    )(x, g)


def _routing_tables(starts, lens, bases, totals, n_experts, te, nblk):
    totals = totals[0, :n_experts]
    padded = (totals + te - 1) // te * te
    ends = jnp.cumsum(padded)
    first_row = ends - padded
    dst = (first_row[None, :] + bases[:, 0, :n_experts]).reshape(-1).astype(jnp.int32)
    tabs = (starts[:, 0, :n_experts].reshape(-1), lens[:, 0, :n_experts].reshape(-1), dst)
    tails = ((padded - totals).astype(jnp.int32), (first_row + totals).astype(jnp.int32))
    nused = (ends[-1] // te).astype(jnp.int32)
    blk = jnp.arange(nblk, dtype=jnp.int32)
    blk_e = jnp.sum((ends[None, :] <= (blk * te)[:, None]).astype(jnp.int32), axis=1)
    blk_e = jnp.minimum(blk_e, n_experts - 1)
    blk_e = jnp.where(blk < nused, blk_e, blk_e[nused - 1])
    return tabs, tails, blk_e, nused.reshape(1)


def _split_bf16(w):
    hi = w.astype(BF16)
    return jnp.concatenate([hi, (w - hi.astype(F32)).astype(BF16)], axis=1)


def _pad_lanes(a, fill=0.0):
    return jnp.pad(a, [(0, 0)] * (a.ndim - 1) + [(0, LANES - a.shape[-1])], constant_values=fill)


def kernel(x, c, positions, w_mod, b_mod, g_attn, w_in, b_forget, g_mix, w_out, g_ffn, w_router, b_router,
           w_gate_up, b_gate_up, w_down, b_down, g_final):
    b, s, d = x.shape
    depth = w_mod.shape[0]
    n_experts = w_router.shape[-1]
    d_mix = w_out.shape[1]
    n = b * s
    te = min(EXPERT_TILE, n)
    ntiles = n // min(ROW_TILE, s)
    nblk = -(-(n * TOP_K + ntiles * n_experts * (GROUP_ALIGN - 1) + n_experts * (te - 1)) // te)
    cap = nblk * te

    mod_all = modulation(c, w_mod, b_mod)
    cos, sin = rope_tables(positions)
    for layer in range(depth):
        mod = mod_all[layer].reshape(b, 1, N_MOD * d)
        w_qkv = family_major_qkv_weight(w_in[layer], d_mix)
        w_f = _pad_lanes(w_in[layer, :, 3 * d_mix:]).astype(BF16)
        b_f = _pad_lanes(b_forget[layer][None, :])
        qkv_sb, qkv_dil, qkv_fox, logf = in_projection(x, mod, g_attn[layer][None, :], w_qkv, w_f, b_f)
        qaug, kaug = forget_cumsum(logf, N_HEADS_FOX)
        o_sb = sb_attention(qkv_sb)
        o_dil = dil_attention(qkv_dil, cos, sin)
        o_fox = fox_attention(qkv_fox, qaug, kaug)
        x, h2, pos, post, gates, starts, lens, bases, totals = outproj_router(
            o_sb, o_dil, o_fox, g_mix[layer][None, :], w_out[layer].astype(BF16), x, mod, g_ffn[layer][None, :],
            _split_bf16(_pad_lanes(w_router[layer])), _pad_lanes(b_router[layer][None, :], NEG_BIG))
        tabs, tails, blk_e, nused = _routing_tables(starts, lens, bases, totals, n_experts, te, nblk)
        xs = dispatch_rows(tabs, tails, nused, h2, post, cap, n_experts, te)
        ys = expert_ffn(blk_e, nused, xs, layer, w_gate_up, b_gate_up[:, :, None, :], w_down,
                        b_down[:, :, None, :], te)
        x = combine_rows(tabs, ys, pos, gates, x, mod, n_experts)
    return final_norm(x, g_final[None, :])
```

```python
import functools

import numpy as np
import jax
import jax.numpy as jnp
from jax import lax
from jax.experimental import pallas as pl
from jax.experimental.pallas import tpu as pltpu

F32 = jnp.float32
BF16 = jnp.bfloat16
HIGHEST = lax.Precision.HIGHEST

HEAD_DIM = 64
LANES = 128
N_HEADS_SB = 4
N_HEADS_DIL = 6
N_HEADS_FOX = 6
DIL_PATTERNS = ((128, 1), (512, 4), (2048, 16))
ROPE_THETA = 10000.0
TOP_K = 4
SWIGLU_LIMIT = 7.0
SWIGLU_ALPHA = 1.702
N_MOD = 6
EPS = 1e-6
NEG_BIG = -1e30
SB_TILE = 256
SOFTMAX_TILE = 512
ROW_TILE = 512
EXPERT_TILE = 512
VMEM_LIMIT = 48 * 1024 * 1024
EXPERT_VMEM_LIMIT = 60 * 1024 * 1024


def _params(sem, vmem=VMEM_LIMIT):
    return pltpu.CompilerParams(dimension_semantics=sem, vmem_limit_bytes=vmem)


def _rms_rows(x):
    return x * lax.rsqrt(jnp.mean(x * x, axis=-1, keepdims=True) + EPS)


def _softplus(z):
    return jnp.maximum(z, 0.0) + jnp.log(1.0 + jnp.exp(-jnp.abs(z)))


def _mod_kernel(c_ref, w_ref, b_ref, o_ref):
    c = c_ref[...]
    ca = c * jax.nn.sigmoid(c)
    o_ref[...] = jnp.dot(ca, w_ref[...], preferred_element_type=F32, precision=HIGHEST) + b_ref[...]


def modulation(c, w_mod, b_mod):
    depth, d, n6 = w_mod.shape
    b = c.shape[0]
    tn = min(n6, 1536)
    return pl.pallas_call(
        _mod_kernel,
        grid=(depth, n6 // tn),
        in_specs=[pl.BlockSpec((b, d), lambda l, j: (0, 0)),
                  pl.BlockSpec((None, d, tn), lambda l, j: (l, 0, j)),
                  pl.BlockSpec((None, 1, tn), lambda l, j: (l, 0, j))],
        out_specs=pl.BlockSpec((None, b, tn), lambda l, j: (l, 0, j)),
        out_shape=jax.ShapeDtypeStruct((depth, b, n6), F32),
        compiler_params=_params(("arbitrary", "arbitrary")),
        name="modulation",
    )(c, w_mod, b_mod.reshape(depth, 1, n6))


def _rope_table_kernel(pos_ref, invf_ref, cos_ref, sin_ref):
    ang = pos_ref[...].astype(F32) * invf_ref[...]
    lane = lax.broadcasted_iota(jnp.int32, ang.shape, 1)
    first_half = (lane % HEAD_DIM) < (HEAD_DIM // 2)
    s = jnp.sin(ang)
    cos_ref[...] = jnp.cos(ang)
    sin_ref[...] = jnp.where(first_half, -s, s)


def rope_tables(positions):
    b, s = positions.shape
    half = HEAD_DIM // 2
    inv_freq = ROPE_THETA ** (-np.arange(0, HEAD_DIM, 2, dtype=np.float64) / HEAD_DIM)
    invf = jnp.asarray(np.tile(inv_freq, LANES // half)[None, :], F32)
    return pl.pallas_call(
        _rope_table_kernel,
        grid=(b,),
        in_specs=[pl.BlockSpec((None, s, 1), lambda i: (i, 0, 0)),
                  pl.BlockSpec((1, LANES), lambda i: (0, 0))],
        out_specs=[pl.BlockSpec((None, s, LANES), lambda i: (i, 0, 0))] * 2,
        out_shape=[jax.ShapeDtypeStruct((b, s, LANES), F32)] * 2,
        compiler_params=_params(("arbitrary",)),
        name="rope_tables",
    )(positions.reshape(b, s, 1), invf)


def _rope(x, cos, sin_signed):
    lane = lax.broadcasted_iota(jnp.int32, x.shape, 1)
    first_half = (lane % HEAD_DIM) < (HEAD_DIM // 2)
    half = HEAD_DIM // 2
    partner = jnp.where(first_half, pltpu.roll(x, LANES - half, 1), pltpu.roll(x, half, 1))
    return x * cos + partner * sin_signed


FAMILY_HEADS = (N_HEADS_SB, N_HEADS_DIL, N_HEADS_FOX)


def family_major_qkv_weight(w_in_layer, d_mix):
    wq = w_in_layer[:, :d_mix] * (HEAD_DIM ** -0.5)
    wk = w_in_layer[:, d_mix:2 * d_mix]
    wv = w_in_layer[:, 2 * d_mix:3 * d_mix]
    cols, lo = [], 0
    for nh in FAMILY_HEADS:
        hi = lo + nh * HEAD_DIM
        cols += [wq[:, lo:hi], wk[:, lo:hi], wv[:, lo:hi]]
        lo = hi
    return jnp.concatenate(cols, axis=1).astype(BF16)


def _inproj_kernel(x_ref, sh_ref, sc_ref, g_ref, wqkv_ref, wf_ref, bf_ref, sb_ref, dil_ref, fox_ref, logf_ref):
    x = x_ref[...]
    h = _rms_rows(x) * g_ref[...] * (1.0 + sc_ref[...]) + sh_ref[...]
    hb = h.astype(BF16)
    lo = 0
    for o_ref in (sb_ref, dil_ref, fox_ref):
        w = o_ref.shape[-1]
        o_ref[...] = jnp.dot(hb, wqkv_ref[:, lo:lo + w], preferred_element_type=F32).astype(BF16)
        lo += w
    f = jnp.dot(hb, wf_ref[...], preferred_element_type=F32) + bf_ref[...]
    logf_ref[...] = -_softplus(-f)


def in_projection(x, mod, g, w_qkv, w_f, b_f):
    b, s, d = x.shape
    d_mix = w_qkv.shape[1] // 3
    tm = min(ROW_TILE, s)
    widths = [3 * nh * HEAD_DIM for nh in FAMILY_HEADS]
    return pl.pallas_call(
        _inproj_kernel,
        grid=(b, s // tm),
        in_specs=[pl.BlockSpec((None, tm, d), lambda i, j: (i, j, 0)),
                  pl.BlockSpec((None, 1, d), lambda i, j: (i, 0, 0)),
                  pl.BlockSpec((None, 1, d), lambda i, j: (i, 0, 1)),
                  pl.BlockSpec((1, d), lambda i, j: (0, 0)),
                  pl.BlockSpec((d, 3 * d_mix), lambda i, j: (0, 0)),
                  pl.BlockSpec((d, LANES), lambda i, j: (0, 0)),
                  pl.BlockSpec((1, LANES), lambda i, j: (0, 0))],
        out_specs=[pl.BlockSpec((None, tm, w), lambda i, j: (i, j, 0)) for w in widths + [LANES]],
        out_shape=[jax.ShapeDtypeStruct((b, s, w), BF16) for w in widths]
        + [jax.ShapeDtypeStruct((b, s, LANES), F32)],
        compiler_params=_params(("arbitrary", "arbitrary")),
        name="in_projection",
    )(x, mod, mod, g, w_qkv, w_f, b_f)


AUG_TERMS = 3


def _own_half(shape, h):
    lane = lax.broadcasted_iota(jnp.int32, shape, len(shape) - 1)
    return (lane < HEAD_DIM) if h % 2 == 0 else (lane >= HEAD_DIM)


def _spare_lane(shape, h):
    lane = lax.broadcasted_iota(jnp.int32, shape, len(shape) - 1)
    return lane - (HEAD_DIM if h % 2 == 0 else 0)


def _cumsum_kernel(logf_ref, qaug_ref, kaug_ref, *, nh):
    s = logf_ref.shape[0]
    r = lax.broadcasted_iota(jnp.int32, (LANES, LANES), 0)
    c = lax.broadcasted_iota(jnp.int32, (LANES, LANES), 1)
    tri = (c <= r).astype(F32)
    carry = jnp.zeros((1, LANES), F32)
    for i in range(s // LANES):
        rows = slice(i * LANES, (i + 1) * LANES)
        cs = jnp.dot(tri, logf_ref[rows, :], preferred_element_type=F32, precision=HIGHEST) + carry
        carry = cs[LANES - 1:LANES, :]
        for h in range(nh):
            full = jnp.broadcast_to(cs[:, h:h + 1], (LANES, LANES))
            parts, rest = [], full
            for _ in range(AUG_TERMS):
                piece = rest.astype(BF16).astype(F32)
                parts.append(piece)
                rest = rest - piece
            j = _spare_lane((LANES, LANES), h)
            qa = jnp.where((j >= AUG_TERMS) & (j < 2 * AUG_TERMS), 1.0, 0.0)
            ka = jnp.where((j >= 0) & (j < AUG_TERMS), 1.0, 0.0)
            for n, piece in enumerate(parts):
                qa = jnp.where(j == n, piece, qa)
                ka = jnp.where(j == AUG_TERMS + n, -piece, ka)
            qaug_ref[rows, h * LANES:(h + 1) * LANES] = qa.astype(BF16)
            kaug_ref[rows, h * LANES:(h + 1) * LANES] = ka.astype(BF16)


def forget_cumsum(logf, nh):
    b, s, _ = logf.shape
    out = pl.BlockSpec((None, s, nh * LANES), lambda i: (i, 0, 0))
    return pl.pallas_call(
        functools.partial(_cumsum_kernel, nh=nh),
        grid=(b,),
        in_specs=[pl.BlockSpec((None, s, LANES), lambda i: (i, 0, 0))],
        out_specs=[out, out],
        out_shape=[jax.ShapeDtypeStruct((b, s, nh * LANES), BF16)] * 2,
        compiler_params=_params(("arbitrary",)),
        name="forget_cumsum",
    )(logf)


def _tile_iotas(t):
    return lax.broadcasted_iota(jnp.int32, (t, t), 0), lax.broadcasted_iota(jnp.int32, (t, t), 1)


def _pair_tile(ref, rows, h):
    p = h // 2
    return ref[rows, p * LANES:(p + 1) * LANES]


def _keep_own(tile, h, other=0.0):
    return jnp.where(_own_half(tile.shape, h), tile.astype(F32), other).astype(BF16)


def _qk(q, k):
    return lax.dot_general(q, k, (((1,), (1,)), ((), ())), preferred_element_type=F32)


def _merge_pairs(per_head, o_ref):
    tiles = [jnp.where(_own_half(per_head[h].shape, h), per_head[h], per_head[h + 1])
             for h in range(0, len(per_head), 2)]
    o_ref[...] = jnp.concatenate(tiles, axis=-1).astype(o_ref.dtype)


def _sb_kernel(q_ref, k_ref, v_ref, o_ref, *, t, nh):
    qi = pl.program_id(1)
    row, col = _tile_iotas(t)
    strict = col < row
    upper = (row > col).astype(BF16)
    qs = [_keep_own(_pair_tile(q_ref, slice(None), h), h) for h in range(nh)]

    def tile(ki, carries, diag):
        rows = pl.ds(pl.multiple_of(ki * t, t), t)
        heads = range(nh)
        zs = [_qk(qs[h], _pair_tile(k_ref, rows, h)) for h in heads]
        sps = [_softplus(z) for z in zs]
        log_nots = [jnp.where(strict, -sp, 0.0) if diag else -sp for sp in sps]
        betweens = [jnp.dot(ln.astype(BF16), upper, preferred_element_type=F32) for ln in log_nots]
        out = []
        for h in heads:
            suffix, acc = carries[h]
            w = jnp.exp(zs[h] - sps[h] + betweens[h] + suffix)
            if diag:
                w = jnp.where(strict, w, 0.0)
            acc = acc + jnp.dot(w.astype(BF16), _pair_tile(v_ref, rows, h), preferred_element_type=F32)
            out.append((suffix + betweens[h][:, 0:1] + log_nots[h][:, 0:1], acc))
        return tuple(out)

    init = tuple((jnp.zeros((t, 1), F32), jnp.zeros((t, LANES), F32)) for _ in range(nh))
    carries = tile(qi, init, True)
    carries = lax.fori_loop(0, qi, lambda j, cr: tile(qi - 1 - j, cr, False), carries)
    _merge_pairs([cr[1] for cr in carries], o_ref)


def _online_softmax_steps(scores, values, carries):
    m_news = [jnp.maximum(m, jnp.max(s, axis=-1, keepdims=True)) for s, (m, _) in zip(scores, carries)]
    ps = [jnp.exp(s - m_new).astype(BF16) for s, m_new in zip(scores, m_news)]
    return tuple((m_new, jnp.exp(m - m_new) * acc + jnp.dot(p, v, preferred_element_type=F32))
                 for p, v, m_new, (m, acc) in zip(ps, values, m_news, carries))


def _softmax_init(t, nh):
    return tuple((jnp.full((t, 1), NEG_BIG, F32), jnp.zeros((t, LANES), F32)) for _ in range(nh))


def _softmax_finish(carries, o_ref):
    outs = []
    for h, (_, acc) in enumerate(carries):
        denom = jnp.sum(jnp.where(_spare_lane(acc.shape, h) == 0, acc, 0.0), axis=-1, keepdims=True)
        outs.append(acc / denom)
    _merge_pairs(outs, o_ref)


def _pad_values(v_ref, vpad_ref, nh):
    for h in range(nh):
        tile = _pair_tile(v_ref, slice(None), h)
        ones_col = jnp.where(_spare_lane(tile.shape, h) == 0, 1.0, 0.0)
        vpad_ref[:, h * LANES:(h + 1) * LANES] = _keep_own(tile, h, ones_col)


def _fox_kernel(q_ref, k_ref, v_ref, qaug_ref, kaug_ref, o_ref, kpad_ref, vpad_ref, *, t, nh):
    qi = pl.program_id(1)
    row, col = _tile_iotas(t)

    @pl.when(qi == 0)
    def _():
        _pad_values(v_ref, vpad_ref, nh)
        for h in range(nh):
            lanes = slice(h * LANES, (h + 1) * LANES)
            kpad_ref[:, lanes] = _keep_own(_pair_tile(k_ref, slice(None), h), h, kaug_ref[:, lanes].astype(F32))

    qs = [_keep_own(_pair_tile(q_ref, slice(None), h), h, qaug_ref[:, h * LANES:(h + 1) * LANES].astype(F32))
          for h in range(nh)]

    def tile(ki, carries, diag):
        rows = pl.ds(pl.multiple_of(ki * t, t), t)
        lanes = [slice(h * LANES, (h + 1) * LANES) for h in range(nh)]
        scores = [_qk(qs[h], kpad_ref[rows, lanes[h]]) for h in range(nh)]
        if diag:
            scores = [jnp.where(col <= row, s, NEG_BIG) for s in scores]
        return _online_softmax_steps(scores, [vpad_ref[rows, ln] for ln in lanes], carries)

    carries = lax.fori_loop(0, qi, lambda ki, cr: tile(ki, cr, False), _softmax_init(t, nh))
    _softmax_finish(tile(qi, carries, True), o_ref)


DIL_NEAR_TILES = 3


def _dilation_log_count(d):
    cnt = None
    for window, dil in DIL_PATTERNS:
        hit = (d >= 0) & (d <= window) & ((d & (dil - 1)) == 0)
        term = jnp.where(hit, 1.0, 0.0)
        cnt = term if cnt is None else cnt + term
    return jnp.where(cnt > 0.0, jnp.log(jnp.maximum(cnt, 1.0)), NEG_BIG)


def _dil_bias_kernel(o_ref):
    t = o_ref.shape[-1]
    row, col = _tile_iotas(t)
    o_ref[...] = _dilation_log_count(pl.program_id(0) * t + row - col)


def dilation_bias(t):
    return pl.pallas_call(
        _dil_bias_kernel,
        grid=(DIL_NEAR_TILES + 1,),
        out_specs=pl.BlockSpec((None, t, t), lambda i: (i, 0, 0)),
        out_shape=jax.ShapeDtypeStruct((DIL_NEAR_TILES + 1, t, t), F32),
        compiler_params=_params(("arbitrary",)),
        name="dilation_bias",
    )()


def _dil_kernel(q_ref, k_ref, v_ref, cos_ref, sin_ref, bias_ref, o_ref, kr_ref, vpad_ref, *, t, nh):
    qi = pl.program_id(1)

    @pl.when(qi == 0)
    def _():
        _pad_values(v_ref, vpad_ref, nh)
        for p in range(nh // 2):
            lanes = slice(p * LANES, (p + 1) * LANES)
            kr_ref[:, lanes] = _rope(k_ref[:, lanes].astype(F32), cos_ref[...], sin_ref[...]).astype(BF16)

    qrows = pl.ds(pl.multiple_of(qi * t, t), t)
    cos, sin = cos_ref[qrows, :], sin_ref[qrows, :]
    qs = [_keep_own(_rope(_pair_tile(q_ref, slice(None), h).astype(F32), cos, sin), h) for h in range(nh)]

    def tile(ki, carries):
        rows = pl.ds(pl.multiple_of(ki * t, t), t)
        bias = bias_ref[jnp.minimum(qi - ki, DIL_NEAR_TILES)]
        scores = [_qk(qs[h], _pair_tile(kr_ref, rows, h)) + bias for h in range(nh)]
        values = [vpad_ref[rows, h * LANES:(h + 1) * LANES] for h in range(nh)]
        return _online_softmax_steps(scores, values, carries)

    _softmax_finish(lax.fori_loop(0, qi + 1, tile, _softmax_init(t, nh)), o_ref)


def _attention_call(kernel, qkv, n_heads, t, extra_inputs, extra_specs, scratch, name):
    b, s, w3 = qkv.shape
    w = w3 // 3
    return pl.pallas_call(
        functools.partial(kernel, t=t, nh=n_heads),
        grid=(b, s // t),
        in_specs=[pl.BlockSpec((None, t, w), lambda i, j: (i, j, 0)),
                  pl.BlockSpec((None, s, w), lambda i, j: (i, 0, 1)),
                  pl.BlockSpec((None, s, w), lambda i, j: (i, 0, 2))] + extra_specs,
        out_specs=pl.BlockSpec((None, t, w), lambda i, j: (i, j, 0)),
        out_shape=jax.ShapeDtypeStruct((b, s, w), BF16),
        scratch_shapes=scratch,
        compiler_params=_params(("arbitrary", "arbitrary")),
        name=name,
    )(qkv, qkv, qkv, *extra_inputs)


def sb_attention(qkv):
    return _attention_call(_sb_kernel, qkv, N_HEADS_SB, min(SB_TILE, qkv.shape[1]), [], [], [], "sb_attention")


def dil_tile(s):
    t = min(SOFTMAX_TILE, s)
    widest_window, widest_dil = DIL_PATTERNS[-1]
    assert s - 1 <= widest_window and t % widest_dil == 0
    assert all(window < (DIL_NEAR_TILES - 1) * t + 1 for window, _ in DIL_PATTERNS[:-1])
    return t


def dil_attention(qkv, cos, sin, bias):
    s = qkv.shape[1]
    t = bias.shape[-1]
    tab = pl.BlockSpec((None, s, LANES), lambda i, j: (i, 0, 0))
    whole = pl.BlockSpec(bias.shape, lambda i, j: (0, 0, 0))
    scratch = [pltpu.VMEM((s, qkv.shape[-1] // 3), BF16), pltpu.VMEM((s, N_HEADS_DIL * LANES), BF16)]
    return _attention_call(_dil_kernel, qkv, N_HEADS_DIL, t, [cos, sin, bias], [tab, tab, whole], scratch,
                           "dil_attention")


def fox_attention(qkv, qaug, kaug):
    s = qkv.shape[1]
    t = min(SOFTMAX_TILE, s)
    wide = N_HEADS_FOX * LANES
    specs = [pl.BlockSpec((None, t, wide), lambda i, j: (i, j, 0)),
             pl.BlockSpec((None, s, wide), lambda i, j: (i, 0, 0))]
    scratch = [pltpu.VMEM((s, wide), BF16), pltpu.VMEM((s, wide), BF16)]
    return _attention_call(_fox_kernel, qkv, N_HEADS_FOX, t, [qaug, kaug], specs, scratch, "fox_attention")


def _outproj_router_kernel(osb_ref, odil_ref, ofox_ref, gmix_ref, wout_ref, x_ref, ga_ref, sc_ref, sh_ref,
                           gffn_ref, wr_ref, br_ref,
                           xo_ref, h2_ref, pos_ref, post_ref, gate_ref, start_ref, len_ref, base_ref, total_ref,
                           carry_ref):
    first = (pl.program_id(0) == 0) & (pl.program_id(1) == 0)

    @pl.when(first)
    def _():
        carry_ref[...] = jnp.zeros_like(carry_ref)

    a = None
    lo = 0
    for o_ref in (osb_ref, odil_ref, ofox_ref):
        w = o_ref.shape[-1]
        on = (_rms_rows(o_ref[...].astype(F32)) * gmix_ref[:, lo:lo + w]).astype(BF16)
        part = jnp.dot(on, wout_ref[lo:lo + w, :], preferred_element_type=F32)
        a = part if a is None else a + part
        lo += w
    xn = x_ref[...] + ga_ref[...] * a
    xo_ref[...] = xn
    h2 = _rms_rows(xn) * gffn_ref[...] * (1.0 + sc_ref[...]) + sh_ref[...]
    h2_ref[...] = h2

    h_hi = h2.astype(BF16)
    h_lo = (h2 - h_hi.astype(F32)).astype(BF16)
    both = jnp.dot(h_hi, wr_ref[...], preferred_element_type=F32)
    logits = (both[:, :LANES] + both[:, LANES:]
              + jnp.dot(h_lo, wr_ref[:, :LANES], preferred_element_type=F32) + br_ref[...])
    tm = logits.shape[0]
    lane = lax.broadcasted_iota(jnp.int32, (tm, LANES), 1)
    vals, idxs = [], []
    rest = logits
    for _ in range(TOP_K):
        m = jnp.max(rest, axis=-1, keepdims=True)
        ik = jnp.min(jnp.where(rest == m, lane, LANES), axis=-1, keepdims=True)
        vals.append(m)
        idxs.append(ik)
        rest = jnp.where(lane == ik, -jnp.inf, rest)
    es = [jnp.exp(v - vals[0]) for v in vals]
    den = es[0] + es[1] + es[2] + es[3]

    hot = [(lane == ik) for ik in idxs]
    multi = jnp.where(hot[0] | hot[1] | hot[2] | hot[3], 1.0, 0.0)
    r = lax.broadcasted_iota(jnp.int32, (tm, tm), 0)
    c = lax.broadcasted_iota(jnp.int32, (tm, tm), 1)
    before = (c < r).astype(BF16)
    prior = jnp.dot(before, multi.astype(BF16), preferred_element_type=F32)
    counts = (prior[tm - 1:tm, :] + multi[tm - 1:tm, :]).astype(jnp.int32)
    group_len = (counts + (GROUP_ALIGN - 1)) // GROUP_ALIGN * GROUP_ALIGN
    er = lax.broadcasted_iota(jnp.int32, (LANES, LANES), 0)
    ec = lax.broadcasted_iota(jnp.int32, (LANES, LANES), 1)
    lens8 = jnp.broadcast_to(group_len.astype(F32), (8, LANES)).astype(BF16)
    group_start = jnp.dot(lens8, (er < ec).astype(BF16), preferred_element_type=F32)[0:1, :]
    where_in_tile = group_start + prior
    pos_out = jnp.zeros((tm, LANES), F32)
    gate_out = jnp.zeros((tm, LANES), F32)
    for k in range(TOP_K):
        pk = jnp.sum(jnp.where(hot[k], where_in_tile, 0.0), axis=-1, keepdims=True)
        pos_out = jnp.where(lane == k, pk, pos_out)
        gate_out = jnp.where(lane == k, es[k] / den, gate_out)
    pos_ref[...] = pos_out.astype(jnp.int32)
    post_ref[...] = pos_out.T[0:8, :].astype(jnp.int32)
    gate_ref[...] = gate_out
    start_ref[...] = group_start.astype(jnp.int32)
    len_ref[...] = group_len
    base_ref[...] = carry_ref[...]
    carry_ref[...] = carry_ref[...] + group_len
    total_ref[...] = carry_ref[...]


def outproj_router(o_sb, o_dil, o_fox, g_mix, w_out, x, mod, g_ffn, w_r, b_r):
    b, s, d = x.shape
    tm = min(ROW_TILE, s)
    n = b * s

    def act(w):
        return pl.BlockSpec((None, tm, w), lambda i, j: (i, j, 0))

    def modspec(col):
        return pl.BlockSpec((None, 1, d), lambda i, j: (i, 0, col))

    def const(shape):
        return pl.BlockSpec(shape, lambda i, j: (0, 0))

    per_b = s // tm
    tok = pl.BlockSpec((tm, LANES), lambda i, j: (i * per_b + j, 0))
    per_tile = pl.BlockSpec((None, 1, LANES), lambda i, j: (i * per_b + j, 0, 0))
    ntiles = b * per_b
    tile_tab = jax.ShapeDtypeStruct((ntiles, 1, LANES), jnp.int32)
    return pl.pallas_call(
        _outproj_router_kernel,
        grid=(b, per_b),
        in_specs=[act(o_sb.shape[-1]), act(o_dil.shape[-1]), act(o_fox.shape[-1]),
                  const((1, d)), const(w_out.shape), act(d),
                  modspec(2), modspec(4), modspec(3),
                  const((1, d)), const((d, 2 * LANES)), const((1, LANES))],
        out_specs=[act(d), pl.BlockSpec((tm, d), lambda i, j: (i * per_b + j, 0)), tok,
                   pl.BlockSpec((None, 8, tm), lambda i, j: (i * per_b + j, 0, 0)), tok,
                   per_tile, per_tile, per_tile, const((1, LANES))],
        out_shape=[jax.ShapeDtypeStruct((b, s, d), F32), jax.ShapeDtypeStruct((n, d), F32),
                   jax.ShapeDtypeStruct((n, LANES), jnp.int32), jax.ShapeDtypeStruct((ntiles, 8, tm), jnp.int32),
                   jax.ShapeDtypeStruct((n, LANES), F32), tile_tab, tile_tab, tile_tab,
                   jax.ShapeDtypeStruct((1, LANES), jnp.int32)],
        scratch_shapes=[pltpu.VMEM((1, LANES), jnp.int32)],
        compiler_params=_params(("arbitrary", "arbitrary")),
        name="outproj_router",
    )(o_sb, o_dil, o_fox, g_mix, w_out, x, mod, mod, mod, g_ffn, w_r, b_r)


GROUP_ALIGN = 8
GROUP_CHUNKS = (512, 256, 128, 64, 32, 16, 8)


def sorted_rows(tm, n_experts):
    return -(-(TOP_K * tm + n_experts * (GROUP_ALIGN - 1)) // 256) * 256


def _for_each_chunk(len_ref, dst_ref, first, count, fn):
    def body(e, _):
        length = len_ref[first + e]
        hbm_row = dst_ref[first + e]
        offset = jnp.int32(0)
        for size in GROUP_CHUNKS:
            hit = (length & size) != 0

            @pl.when(hit)
            def _(offset=offset, hbm_row=hbm_row, size=size):
                fn(size, first + e, pl.multiple_of(offset, GROUP_ALIGN), pl.multiple_of(hbm_row, GROUP_ALIGN))

            step = jnp.where(hit, size, 0)
            offset = offset + step
            hbm_row = hbm_row + step
        return 0

    lax.fori_loop(0, count, body, 0)


def _dispatch_kernel(start_ref, len_ref, dst_ref, tail_len_ref, tail_dst_ref, nused_ref, h_ref, post_ref, xs_hbm,
                     buf_ref, sem, *, ne, te):
    rows, tm = buf_ref.shape[1], h_ref.shape[0]
    nblk = xs_hbm.shape[0] // te
    j = pl.program_id(0)
    slot = j % 2

    @pl.when(j == 0)
    def _():
        zero_rows, zsem = buf_ref.at[1], sem.at[1]
        zero_rows[0:te, :] = jnp.zeros((te, buf_ref.shape[2]), F32)

        def tail(size, e, offset, hbm_row):
            return pltpu.make_async_copy(zero_rows.at[pl.ds(0, size)], xs_hbm.at[pl.ds(hbm_row, size)], zsem)

        def block(i):
            dst = xs_hbm.at[pl.ds(pl.multiple_of(i * te, te), te)]
            return pltpu.make_async_copy(zero_rows.at[pl.ds(0, te)], dst, zsem)

        _for_each_chunk(tail_len_ref, tail_dst_ref, 0, ne, lambda *a: tail(*a).start())
        lax.fori_loop(nused_ref[0], nblk, lambda i, c: (block(i).start(), c)[1], 0)
        _for_each_chunk(tail_len_ref, tail_dst_ref, 0, ne, lambda *a: tail(*a).wait())
        lax.fori_loop(nused_ref[0], nblk, lambda i, c: (block(i).wait(), c)[1], 0)

    r = lax.broadcasted_iota(jnp.int32, (rows, tm), 0)
    pick = r == post_ref[0:1, :]
    for k in range(1, TOP_K):
        pick = pick | (r == post_ref[k:k + 1, :])
    buf_ref[slot] = jnp.dot(jnp.where(pick, 1.0, 0.0).astype(BF16), h_ref[...].astype(BF16),
                            preferred_element_type=F32)

    def copy(which):
        def make(size, g, offset, hbm_row):
            buf_row = pl.multiple_of(start_ref[g] + offset, GROUP_ALIGN)
            return pltpu.make_async_copy(buf_ref.at[which, pl.ds(buf_row, size)], xs_hbm.at[pl.ds(hbm_row, size)],
                                         sem.at[which])
        return make

    @pl.when(j > 0)
    def _():
        _for_each_chunk(len_ref, dst_ref, (j - 1) * ne, ne, lambda *a: copy(1 - slot)(*a).wait())

    _for_each_chunk(len_ref, dst_ref, j * ne, ne, lambda *a: copy(slot)(*a).start())

    @pl.when(j == pl.num_programs(0) - 1)
    def _():
        _for_each_chunk(len_ref, dst_ref, j * ne, ne, lambda *a: copy(slot)(*a).wait())


def dispatch_rows(tabs, tails, nused, h, post, cap, ne, te):
    n, d = h.shape
    tm = post.shape[-1]
    return pl.pallas_call(
        functools.partial(_dispatch_kernel, ne=ne, te=te),
        grid_spec=pltpu.PrefetchScalarGridSpec(
            num_scalar_prefetch=6,
            grid=(n // tm,),
            in_specs=[pl.BlockSpec((tm, d), lambda i, *_: (i, 0)),
                      pl.BlockSpec((None, 8, tm), lambda i, *_: (i, 0, 0))],
            out_specs=pl.BlockSpec(memory_space=pl.ANY),
            scratch_shapes=[pltpu.VMEM((2, sorted_rows(tm, ne), d), F32), pltpu.SemaphoreType.DMA((2,))]),
        out_shape=jax.ShapeDtypeStruct((cap, d), F32),
        compiler_params=pltpu.CompilerParams(dimension_semantics=("arbitrary",), has_side_effects=True,
                                             vmem_limit_bytes=VMEM_LIMIT),
        name="dispatch_rows",
    )(*tabs, *tails, nused, h, post)


def _expert_kernel(blk_e_ref, nused_ref, xs_ref, wgu_ref, bgu_ref, wdn_ref, bdn_ref, o_ref, wgu_bf, wdn_bf):
    i = pl.program_id(0)
    used = i < nused_ref[0]

    @pl.when(jnp.logical_not(used))
    def _():
        o_ref[...] = jnp.zeros_like(o_ref)

    @pl.when(used & ((i == 0) | (blk_e_ref[i] != blk_e_ref[jnp.maximum(i - 1, 0)])))
    def _():
        wgu_bf[...] = wgu_ref[...].astype(BF16)
        wdn_bf[...] = wdn_ref[...].astype(BF16)

    @pl.when(used)
    def _():
        f = wdn_ref.shape[0]
        gu = jnp.dot(xs_ref[...].astype(BF16), wgu_bf[...], preferred_element_type=F32) + bgu_ref[...]
        gate = jnp.minimum(gu[:, :f], SWIGLU_LIMIT)
        up = jnp.clip(gu[:, f:], -SWIGLU_LIMIT, SWIGLU_LIMIT)
        act = (up + 1.0) * (gate * jax.nn.sigmoid(SWIGLU_ALPHA * gate))
        o_ref[...] = jnp.dot(act.astype(BF16), wdn_bf[...], preferred_element_type=F32) + bdn_ref[...]


def expert_ffn(blk_e, nused, xs, layer, w_gu, b_gu, w_dn, b_dn, te):
    cap, d = xs.shape
    f2 = w_gu.shape[-1]
    f = f2 // 2
    nblk = cap // te

    def rows(i, be, nu):
        return (jnp.minimum(i, nu[0] - 1), 0)

    def per_expert(i, be, nu):
        return (layer, be[i], 0, 0)

    return pl.pallas_call(
        _expert_kernel,
        grid_spec=pltpu.PrefetchScalarGridSpec(
            num_scalar_prefetch=2,
            grid=(nblk,),
            in_specs=[pl.BlockSpec((te, d), rows),
                      pl.BlockSpec((None, None, d, f2), per_expert),
                      pl.BlockSpec((None, None, 1, f2), per_expert),
                      pl.BlockSpec((None, None, f, d), per_expert),
                      pl.BlockSpec((None, None, 1, d), per_expert)],
            out_specs=pl.BlockSpec((te, d), lambda i, be, nu: (i, 0)),
            scratch_shapes=[pltpu.VMEM((d, f2), BF16), pltpu.VMEM((f, d), BF16)]),
        out_shape=jax.ShapeDtypeStruct((cap, d), F32),
        compiler_params=_params(("arbitrary",), EXPERT_VMEM_LIMIT),
        name="expert_ffn",
    )(blk_e, nused, xs, w_gu, b_gu, w_dn, b_dn)


def _combine_kernel(start_ref, len_ref, dst_ref, ys_hbm, pos_ref, gate_ref, x_ref, ga_ref, gfin_ref, xo_ref, buf_ref,
                    sem, *, ne, final):
    j = pl.program_id(0)
    slot = j % 2

    def copy(which):
        def make(size, g, offset, hbm_row):
            buf_row = pl.multiple_of(start_ref[g] + offset, GROUP_ALIGN)
            return pltpu.make_async_copy(ys_hbm.at[pl.ds(hbm_row, size)], buf_ref.at[which, pl.ds(buf_row, size)],
                                         sem.at[which])
        return make

    @pl.when(j == 0)
    def _():
        buf_ref[...] = jnp.zeros_like(buf_ref)
        _for_each_chunk(len_ref, dst_ref, 0, ne, lambda *a: copy(0)(*a).start())

    @pl.when(j + 1 < pl.num_programs(0))
    def _():
        _for_each_chunk(len_ref, dst_ref, (j + 1) * ne, ne, lambda *a: copy(1 - slot)(*a).start())

    rows, tm = buf_ref.shape[1], pos_ref.shape[0]
    r = lax.broadcasted_iota(jnp.int32, (tm, rows), 1)
    pos, g = pos_ref[...], gate_ref[...]
    weights = jnp.zeros((tm, rows), F32)
    for k in range(TOP_K):
        weights = jnp.where(r == pos[:, k:k + 1], g[:, k:k + 1], weights)
    _for_each_chunk(len_ref, dst_ref, j * ne, ne, lambda *a: copy(slot)(*a).wait())
    y = jnp.dot(weights.astype(BF16), buf_ref[slot].astype(BF16), preferred_element_type=F32)
    xo = x_ref[...] + ga_ref[...] * y
    xo_ref[...] = _rms_rows(xo) * gfin_ref[...] if final else xo


def combine_rows(tabs, ys, pos, gates, x, mod, g_final, ne, final):
    b, s, d = x.shape
    tm = min(ROW_TILE, s)
    per_b = s // tm
    return pl.pallas_call(
        functools.partial(_combine_kernel, ne=ne, final=final),
        grid_spec=pltpu.PrefetchScalarGridSpec(
            num_scalar_prefetch=3,
            grid=(b * per_b,),
            in_specs=[pl.BlockSpec(memory_space=pl.ANY),
                      pl.BlockSpec((tm, LANES), lambda i, *_: (i, 0)),
                      pl.BlockSpec((tm, LANES), lambda i, *_: (i, 0)),
                      pl.BlockSpec((None, tm, d), lambda i, *_: (i // per_b, i % per_b, 0)),
                      pl.BlockSpec((None, 1, d), lambda i, *_: (i // per_b, 0, 5)),
                      pl.BlockSpec((1, d), lambda i, *_: (0, 0))],
            out_specs=pl.BlockSpec((None, tm, d), lambda i, *_: (i // per_b, i % per_b, 0)),
            scratch_shapes=[pltpu.VMEM((2, sorted_rows(tm, ne), d), F32), pltpu.SemaphoreType.DMA((2,))]),
        out_shape=jax.ShapeDtypeStruct((b, s, d), F32),
        compiler_params=_params(("arbitrary",), EXPERT_VMEM_LIMIT),
        name="combine_rows",
    )(*tabs, ys, pos, gates, x, mod, g_final)


def _routing_tables(starts, lens, bases, totals, n_experts, te, nblk):
    totals = totals[0, :n_experts]
    padded = (totals + te - 1) // te * te
    ends = jnp.cumsum(padded)
    first_row = ends - padded
    dst = (first_row[None, :] + bases[:, 0, :n_experts]).reshape(-1).astype(jnp.int32)
    tabs = (starts[:, 0, :n_experts].reshape(-1), lens[:, 0, :n_experts].reshape(-1), dst)
    tails = ((padded - totals).astype(jnp.int32), (first_row + totals).astype(jnp.int32))
    nused = (ends[-1] // te).astype(jnp.int32)
    blk = jnp.arange(nblk, dtype=jnp.int32)
    blk_e = jnp.sum((ends[None, :] <= (blk * te)[:, None]).astype(jnp.int32), axis=1)
    blk_e = jnp.minimum(blk_e, n_experts - 1)
    blk_e = jnp.where(blk < nused, blk_e, blk_e[nused - 1])
    return tabs, tails, blk_e, nused.reshape(1)


def _split_bf16(w):
    hi = w.astype(BF16)
    return jnp.concatenate([hi, (w - hi.astype(F32)).astype(BF16)], axis=1)


def _pad_lanes(a, fill=0.0):
    return jnp.pad(a, [(0, 0)] * (a.ndim - 1) + [(0, LANES - a.shape[-1])], constant_values=fill)


def kernel(x, c, positions, w_mod, b_mod, g_attn, w_in, b_forget, g_mix, w_out, g_ffn, w_router, b_router,
           w_gate_up, b_gate_up, w_down, b_down, g_final):
    b, s, d = x.shape
    depth = w_mod.shape[0]
    n_experts = w_router.shape[-1]
    d_mix = w_out.shape[1]
    n = b * s
    te = min(EXPERT_TILE, n)
    ntiles = n // min(ROW_TILE, s)
    nblk = -(-(n * TOP_K + ntiles * n_experts * (GROUP_ALIGN - 1) + n_experts * (te - 1)) // te)
    cap = nblk * te

    mod_all = modulation(c, w_mod, b_mod)
    cos, sin = rope_tables(positions)
    dil_bias = dilation_bias(dil_tile(s))
    for layer in range(depth):
        mod = mod_all[layer].reshape(b, 1, N_MOD * d)
        w_qkv = family_major_qkv_weight(w_in[layer], d_mix)
        w_f = _pad_lanes(w_in[layer, :, 3 * d_mix:]).astype(BF16)
        b_f = _pad_lanes(b_forget[layer][None, :])
        qkv_sb, qkv_dil, qkv_fox, logf = in_projection(x, mod, g_attn[layer][None, :], w_qkv, w_f, b_f)
        qaug, kaug = forget_cumsum(logf, N_HEADS_FOX)
        o_sb = sb_attention(qkv_sb)
        o_dil = dil_attention(qkv_dil, cos, sin, dil_bias)
        o_fox = fox_attention(qkv_fox, qaug, kaug)
        x, h2, pos, post, gates, starts, lens, bases, totals = outproj_router(
            o_sb, o_dil, o_fox, g_mix[layer][None, :], w_out[layer].astype(BF16), x, mod, g_ffn[layer][None, :],
            _split_bf16(_pad_lanes(w_router[layer])), _pad_lanes(b_router[layer][None, :], NEG_BIG))
        tabs, tails, blk_e, nused = _routing_tables(starts, lens, bases, totals, n_experts, te, nblk)
        xs = dispatch_rows(tabs, tails, nused, h2, post, cap, n_experts, te)
        ys = expert_ffn(blk_e, nused, xs, layer, w_gate_up, b_gate_up[:, :, None, :], w_down,
                        b_down[:, :, None, :], te)
        x = combine_rows(tabs, ys, pos, gates, x, mod, g_final[None, :], n_experts, final=layer == depth - 1)
    return x
```

```python
import functools

import numpy as np
import jax
import jax.numpy as jnp
from jax import lax
from jax.experimental import pallas as pl
from jax.experimental.pallas import tpu as pltpu

F32 = jnp.float32
BF16 = jnp.bfloat16
HIGHEST = lax.Precision.HIGHEST

HEAD_DIM = 64
LANES = 128
N_HEADS_SB = 4
N_HEADS_DIL = 6
N_HEADS_FOX = 6
DIL_PATTERNS = ((128, 1), (512, 4), (2048, 16))
ROPE_THETA = 10000.0
TOP_K = 4
SWIGLU_LIMIT = 7.0
SWIGLU_ALPHA = 1.702
N_MOD = 6
EPS = 1e-6
NEG_BIG = -1e30
SB_TILE = 256
SOFTMAX_TILE = 512
ROW_TILE = 512
EXPERT_TILE = 512
VMEM_LIMIT = 48 * 1024 * 1024
EXPERT_VMEM_LIMIT = 60 * 1024 * 1024


def _params(sem, vmem=VMEM_LIMIT):
    return pltpu.CompilerParams(dimension_semantics=sem, vmem_limit_bytes=vmem)


def _rms_rows(x):
    return x * lax.rsqrt(jnp.mean(x * x, axis=-1, keepdims=True) + EPS)


def _softplus(z):
    return jnp.maximum(z, 0.0) + jnp.log(1.0 + jnp.exp(-jnp.abs(z)))


def _mod_kernel(c_ref, w_ref, b_ref, o_ref):
    c = c_ref[...]
    ca = c * jax.nn.sigmoid(c)
    o_ref[...] = jnp.dot(ca, w_ref[...], preferred_element_type=F32, precision=HIGHEST) + b_ref[...]


def modulation(c, w_mod, b_mod):
    depth, d, n6 = w_mod.shape
    b = c.shape[0]
    tn = min(n6, 1536)
    return pl.pallas_call(
        _mod_kernel,
        grid=(depth, n6 // tn),
        in_specs=[pl.BlockSpec((b, d), lambda l, j: (0, 0)),
                  pl.BlockSpec((None, d, tn), lambda l, j: (l, 0, j)),
                  pl.BlockSpec((None, 1, tn), lambda l, j: (l, 0, j))],
        out_specs=pl.BlockSpec((None, b, tn), lambda l, j: (l, 0, j)),
        out_shape=jax.ShapeDtypeStruct((depth, b, n6), F32),
        compiler_params=_params(("arbitrary", "arbitrary")),
        name="modulation",
    )(c, w_mod, b_mod.reshape(depth, 1, n6))


def _rope_table_kernel(pos_ref, invf_ref, cos_ref, sin_ref):
    ang = pos_ref[...].astype(F32) * invf_ref[...]
    lane = lax.broadcasted_iota(jnp.int32, ang.shape, 1)
    first_half = (lane % HEAD_DIM) < (HEAD_DIM // 2)
    s = jnp.sin(ang)
    cos_ref[...] = jnp.cos(ang)
    sin_ref[...] = jnp.where(first_half, -s, s)


def rope_tables(positions):
    b, s = positions.shape
    half = HEAD_DIM // 2
    inv_freq = ROPE_THETA ** (-np.arange(0, HEAD_DIM, 2, dtype=np.float64) / HEAD_DIM)
    invf = jnp.asarray(np.tile(inv_freq, LANES // half)[None, :], F32)
    return pl.pallas_call(
        _rope_table_kernel,
        grid=(b,),
        in_specs=[pl.BlockSpec((None, s, 1), lambda i: (i, 0, 0)),
                  pl.BlockSpec((1, LANES), lambda i: (0, 0))],
        out_specs=[pl.BlockSpec((None, s, LANES), lambda i: (i, 0, 0))] * 2,
        out_shape=[jax.ShapeDtypeStruct((b, s, LANES), F32)] * 2,
        compiler_params=_params(("arbitrary",)),
        name="rope_tables",
    )(positions.reshape(b, s, 1), invf)


def _rope(x, cos, sin_signed):
    lane = lax.broadcasted_iota(jnp.int32, x.shape, 1)
    first_half = (lane % HEAD_DIM) < (HEAD_DIM // 2)
    half = HEAD_DIM // 2
    partner = jnp.where(first_half, pltpu.roll(x, LANES - half, 1), pltpu.roll(x, half, 1))
    return x * cos + partner * sin_signed


FAMILY_HEADS = (N_HEADS_SB, N_HEADS_DIL, N_HEADS_FOX)


def family_major_qkv_weight(w_in_layer, d_mix):
    wq = w_in_layer[:, :d_mix] * (HEAD_DIM ** -0.5)
    wk = w_in_layer[:, d_mix:2 * d_mix]
    wv = w_in_layer[:, 2 * d_mix:3 * d_mix]
    cols, lo = [], 0
    for nh in FAMILY_HEADS:
        hi = lo + nh * HEAD_DIM
        cols += [wq[:, lo:hi], wk[:, lo:hi], wv[:, lo:hi]]
        lo = hi
    return jnp.concatenate(cols, axis=1).astype(BF16)


def _inproj_kernel(x_ref, sh_ref, sc_ref, g_ref, wqkv_ref, wf_ref, bf_ref, sb_ref, dil_ref, fox_ref, logf_ref):
    x = x_ref[...]
    h = _rms_rows(x) * g_ref[...] * (1.0 + sc_ref[...]) + sh_ref[...]
    hb = h.astype(BF16)
    lo = 0
    for o_ref in (sb_ref, dil_ref, fox_ref):
        w = o_ref.shape[-1]
        o_ref[...] = jnp.dot(hb, wqkv_ref[:, lo:lo + w], preferred_element_type=F32).astype(BF16)
        lo += w
    f = jnp.dot(hb, wf_ref[...], preferred_element_type=F32) + bf_ref[...]
    logf_ref[...] = -_softplus(-f)


def in_projection(x, mod, g, w_qkv, w_f, b_f):
    b, s, d = x.shape
    d_mix = w_qkv.shape[1] // 3
    tm = min(ROW_TILE, s)
    widths = [3 * nh * HEAD_DIM for nh in FAMILY_HEADS]
    return pl.pallas_call(
        _inproj_kernel,
        grid=(b, s // tm),
        in_specs=[pl.BlockSpec((None, tm, d), lambda i, j: (i, j, 0)),
                  pl.BlockSpec((None, 1, d), lambda i, j: (i, 0, 0)),
                  pl.BlockSpec((None, 1, d), lambda i, j: (i, 0, 1)),
                  pl.BlockSpec((1, d), lambda i, j: (0, 0)),
                  pl.BlockSpec((d, 3 * d_mix), lambda i, j: (0, 0)),
                  pl.BlockSpec((d, LANES), lambda i, j: (0, 0)),
                  pl.BlockSpec((1, LANES), lambda i, j: (0, 0))],
        out_specs=[pl.BlockSpec((None, tm, w), lambda i, j: (i, j, 0)) for w in widths + [LANES]],
        out_shape=[jax.ShapeDtypeStruct((b, s, w), BF16) for w in widths]
        + [jax.ShapeDtypeStruct((b, s, LANES), F32)],
        compiler_params=_params(("arbitrary", "arbitrary")),
        name="in_projection",
    )(x, mod, mod, g, w_qkv, w_f, b_f)


AUG_TERMS = 3


def _own_half(shape, h):
    lane = lax.broadcasted_iota(jnp.int32, shape, len(shape) - 1)
    return (lane < HEAD_DIM) if h % 2 == 0 else (lane >= HEAD_DIM)


def _spare_lane(shape, h):
    lane = lax.broadcasted_iota(jnp.int32, shape, len(shape) - 1)
    return lane - (HEAD_DIM if h % 2 == 0 else 0)


def _cumsum_kernel(logf_ref, qaug_ref, kaug_ref, *, nh):
    s = logf_ref.shape[0]
    r = lax.broadcasted_iota(jnp.int32, (LANES, LANES), 0)
    c = lax.broadcasted_iota(jnp.int32, (LANES, LANES), 1)
    tri = (c <= r).astype(F32)
    carry = jnp.zeros((1, LANES), F32)
    for i in range(s // LANES):
        rows = slice(i * LANES, (i + 1) * LANES)
        cs = jnp.dot(tri, logf_ref[rows, :], preferred_element_type=F32, precision=HIGHEST) + carry
        carry = cs[LANES - 1:LANES, :]
        for h in range(nh):
            full = jnp.broadcast_to(cs[:, h:h + 1], (LANES, LANES))
            parts, rest = [], full
            for _ in range(AUG_TERMS):
                piece = rest.astype(BF16).astype(F32)
                parts.append(piece)
                rest = rest - piece
            j = _spare_lane((LANES, LANES), h)
            qa = jnp.where((j >= AUG_TERMS) & (j < 2 * AUG_TERMS), 1.0, 0.0)
            ka = jnp.where((j >= 0) & (j < AUG_TERMS), 1.0, 0.0)
            for n, piece in enumerate(parts):
                qa = jnp.where(j == n, piece, qa)
                ka = jnp.where(j == AUG_TERMS + n, -piece, ka)
            qaug_ref[rows, h * LANES:(h + 1) * LANES] = qa.astype(BF16)
            kaug_ref[rows, h * LANES:(h + 1) * LANES] = ka.astype(BF16)


def forget_cumsum(logf, nh):
    b, s, _ = logf.shape
    out = pl.BlockSpec((None, s, nh * LANES), lambda i: (i, 0, 0))
    return pl.pallas_call(
        functools.partial(_cumsum_kernel, nh=nh),
        grid=(b,),
        in_specs=[pl.BlockSpec((None, s, LANES), lambda i: (i, 0, 0))],
        out_specs=[out, out],
        out_shape=[jax.ShapeDtypeStruct((b, s, nh * LANES), BF16)] * 2,
        compiler_params=_params(("arbitrary",)),
        name="forget_cumsum",
    )(logf)


def _tile_iotas(t):
    return lax.broadcasted_iota(jnp.int32, (t, t), 0), lax.broadcasted_iota(jnp.int32, (t, t), 1)


def _pair_tile(ref, rows, h):
    p = h // 2
    return ref[rows, p * LANES:(p + 1) * LANES]


def _keep_own(tile, h, other=0.0):
    return jnp.where(_own_half(tile.shape, h), tile.astype(F32), other).astype(BF16)


def _qk(q, k):
    return lax.dot_general(q, k, (((1,), (1,)), ((), ())), preferred_element_type=F32)


def _merge_pairs(per_head, o_ref):
    tiles = [jnp.where(_own_half(per_head[h].shape, h), per_head[h], per_head[h + 1])
             for h in range(0, len(per_head), 2)]
    o_ref[...] = jnp.concatenate(tiles, axis=-1).astype(o_ref.dtype)


def _sb_kernel(q_ref, k_ref, v_ref, o_ref, *, t, nh):
    qi = pl.program_id(1)
    row, col = _tile_iotas(t)
    strict = col < row
    upper = (row > col).astype(BF16)
    qs = [_keep_own(_pair_tile(q_ref, slice(None), h), h) for h in range(nh)]

    def tile(ki, carries, diag):
        rows = pl.ds(pl.multiple_of(ki * t, t), t)
        heads = range(nh)
        zs = [_qk(qs[h], _pair_tile(k_ref, rows, h)) for h in heads]
        sps = [_softplus(z) for z in zs]
        log_nots = [jnp.where(strict, -sp, 0.0) if diag else -sp for sp in sps]
        betweens = [jnp.dot(ln.astype(BF16), upper, preferred_element_type=F32) for ln in log_nots]
        out = []
        for h in heads:
            suffix, acc = carries[h]
            w = jnp.exp(zs[h] - sps[h] + betweens[h] + suffix)
            if diag:
                w = jnp.where(strict, w, 0.0)
            acc = acc + jnp.dot(w.astype(BF16), _pair_tile(v_ref, rows, h), preferred_element_type=F32)
            out.append((suffix + betweens[h][:, 0:1] + log_nots[h][:, 0:1], acc))
        return tuple(out)

    init = tuple((jnp.zeros((t, 1), F32), jnp.zeros((t, LANES), F32)) for _ in range(nh))
    carries = tile(qi, init, True)
    carries = lax.fori_loop(0, qi, lambda j, cr: tile(qi - 1 - j, cr, False), carries)
    _merge_pairs([cr[1] for cr in carries], o_ref)


def _online_softmax_steps(scores, values, carries):
    m_news = [jnp.maximum(m, jnp.max(s, axis=-1, keepdims=True)) for s, (m, _) in zip(scores, carries)]
    ps = [jnp.exp(s - m_new).astype(BF16) for s, m_new in zip(scores, m_news)]
    return tuple((m_new, jnp.exp(m - m_new) * acc + jnp.dot(p, v, preferred_element_type=F32))
                 for p, v, m_new, (m, acc) in zip(ps, values, m_news, carries))


def _softmax_init(t, nh):
    return tuple((jnp.full((t, 1), NEG_BIG, F32), jnp.zeros((t, LANES), F32)) for _ in range(nh))


def _softmax_finish(carries, o_ref):
    outs = []
    for h, (_, acc) in enumerate(carries):
        denom = jnp.sum(jnp.where(_spare_lane(acc.shape, h) == 0, acc, 0.0), axis=-1, keepdims=True)
        outs.append(acc / denom)
    _merge_pairs(outs, o_ref)


def _pad_values(v_ref, vpad_ref, nh):
    for h in range(nh):
        tile = _pair_tile(v_ref, slice(None), h)
        ones_col = jnp.where(_spare_lane(tile.shape, h) == 0, 1.0, 0.0)
        vpad_ref[:, h * LANES:(h + 1) * LANES] = _keep_own(tile, h, ones_col)


def _fox_kernel(q_ref, k_ref, v_ref, qaug_ref, kaug_ref, o_ref, kpad_ref, vpad_ref, *, t, nh):
    qi = pl.program_id(1)
    row, col = _tile_iotas(t)

    @pl.when(qi == 0)
    def _():
        _pad_values(v_ref, vpad_ref, nh)
        for h in range(nh):
            lanes = slice(h * LANES, (h + 1) * LANES)
            kpad_ref[:, lanes] = _keep_own(_pair_tile(k_ref, slice(None), h), h, kaug_ref[:, lanes].astype(F32))

    qs = [_keep_own(_pair_tile(q_ref, slice(None), h), h, qaug_ref[:, h * LANES:(h + 1) * LANES].astype(F32))
          for h in range(nh)]

    def tile(ki, carries, diag):
        rows = pl.ds(pl.multiple_of(ki * t, t), t)
        lanes = [slice(h * LANES, (h + 1) * LANES) for h in range(nh)]
        scores = [_qk(qs[h], kpad_ref[rows, lanes[h]]) for h in range(nh)]
        if diag:
            scores = [jnp.where(col <= row, s, NEG_BIG) for s in scores]
        return _online_softmax_steps(scores, [vpad_ref[rows, ln] for ln in lanes], carries)

    carries = lax.fori_loop(0, qi, lambda ki, cr: tile(ki, cr, False), _softmax_init(t, nh))
    _softmax_finish(tile(qi, carries, True), o_ref)


DIL_NEAR_TILES = 3


def _dilation_log_count(d):
    cnt = None
    for window, dil in DIL_PATTERNS:
        hit = (d >= 0) & (d <= window) & ((d & (dil - 1)) == 0)
        term = jnp.where(hit, 1.0, 0.0)
        cnt = term if cnt is None else cnt + term
    return jnp.where(cnt > 0.0, jnp.log(jnp.maximum(cnt, 1.0)), NEG_BIG)


def _dil_bias_kernel(o_ref):
    t = o_ref.shape[-1]
    row, col = _tile_iotas(t)
    o_ref[...] = _dilation_log_count(pl.program_id(0) * t + row - col)


def dilation_bias(t):
    return pl.pallas_call(
        _dil_bias_kernel,
        grid=(DIL_NEAR_TILES + 1,),
        out_specs=pl.BlockSpec((None, t, t), lambda i: (i, 0, 0)),
        out_shape=jax.ShapeDtypeStruct((DIL_NEAR_TILES + 1, t, t), F32),
        compiler_params=_params(("arbitrary",)),
        name="dilation_bias",
    )()


def _dil_kernel(q_ref, k_ref, v_ref, cos_ref, sin_ref, bias_ref, o_ref, kr_ref, vpad_ref, *, t, nh):
    qi = pl.program_id(1)

    @pl.when(qi == 0)
    def _():
        _pad_values(v_ref, vpad_ref, nh)
        for p in range(nh // 2):
            lanes = slice(p * LANES, (p + 1) * LANES)
            kr_ref[:, lanes] = _rope(k_ref[:, lanes].astype(F32), cos_ref[...], sin_ref[...]).astype(BF16)

    qrows = pl.ds(pl.multiple_of(qi * t, t), t)
    cos, sin = cos_ref[qrows, :], sin_ref[qrows, :]
    qs = [_keep_own(_rope(_pair_tile(q_ref, slice(None), h).astype(F32), cos, sin), h) for h in range(nh)]

    def tile(ki, carries):
        rows = pl.ds(pl.multiple_of(ki * t, t), t)
        bias = bias_ref[jnp.minimum(qi - ki, DIL_NEAR_TILES)]
        scores = [_qk(qs[h], _pair_tile(kr_ref, rows, h)) + bias for h in range(nh)]
        values = [vpad_ref[rows, h * LANES:(h + 1) * LANES] for h in range(nh)]
        return _online_softmax_steps(scores, values, carries)

    _softmax_finish(lax.fori_loop(0, qi + 1, tile, _softmax_init(t, nh)), o_ref)


def _attention_call(kernel, qkv, n_heads, t, extra_inputs, extra_specs, scratch, name):
    b, s, w3 = qkv.shape
    w = w3 // 3
    return pl.pallas_call(
        functools.partial(kernel, t=t, nh=n_heads),
        grid=(b, s // t),
        in_specs=[pl.BlockSpec((None, t, w), lambda i, j: (i, j, 0)),
                  pl.BlockSpec((None, s, w), lambda i, j: (i, 0, 1)),
                  pl.BlockSpec((None, s, w), lambda i, j: (i, 0, 2))] + extra_specs,
        out_specs=pl.BlockSpec((None, t, w), lambda i, j: (i, j, 0)),
        out_shape=jax.ShapeDtypeStruct((b, s, w), BF16),
        scratch_shapes=scratch,
        compiler_params=_params(("arbitrary", "arbitrary")),
        name=name,
    )(qkv, qkv, qkv, *extra_inputs)


def sb_attention(qkv):
    return _attention_call(_sb_kernel, qkv, N_HEADS_SB, min(SB_TILE, qkv.shape[1]), [], [], [], "sb_attention")


def dil_tile(s):
    t = min(SOFTMAX_TILE, s)
    widest_window, widest_dil = DIL_PATTERNS[-1]
    assert s - 1 <= widest_window and t % widest_dil == 0
    assert all(window < (DIL_NEAR_TILES - 1) * t + 1 for window, _ in DIL_PATTERNS[:-1])
    return t


def dil_attention(qkv, cos, sin, bias):
    s = qkv.shape[1]
    t = bias.shape[-1]
    tab = pl.BlockSpec((None, s, LANES), lambda i, j: (i, 0, 0))
    whole = pl.BlockSpec(bias.shape, lambda i, j: (0, 0, 0))
    scratch = [pltpu.VMEM((s, qkv.shape[-1] // 3), BF16), pltpu.VMEM((s, N_HEADS_DIL * LANES), BF16)]
    return _attention_call(_dil_kernel, qkv, N_HEADS_DIL, t, [cos, sin, bias], [tab, tab, whole], scratch,
                           "dil_attention")


def fox_attention(qkv, qaug, kaug):
    s = qkv.shape[1]
    t = min(SOFTMAX_TILE, s)
    wide = N_HEADS_FOX * LANES
    specs = [pl.BlockSpec((None, t, wide), lambda i, j: (i, j, 0)),
             pl.BlockSpec((None, s, wide), lambda i, j: (i, 0, 0))]
    scratch = [pltpu.VMEM((s, wide), BF16), pltpu.VMEM((s, wide), BF16)]
    return _attention_call(_fox_kernel, qkv, N_HEADS_FOX, t, [qaug, kaug], specs, scratch, "fox_attention")


def _outproj_router_kernel(osb_ref, odil_ref, ofox_ref, gmix_ref, wout_ref, x_ref, ga_ref, sc_ref, sh_ref,
                           gffn_ref, wr_ref, br_ref,
                           xo_ref, h2_ref, pos_ref, post_ref, gate_ref, start_ref, len_ref, base_ref, total_ref,
                           carry_ref):
    first = (pl.program_id(0) == 0) & (pl.program_id(1) == 0)

    @pl.when(first)
    def _():
        carry_ref[...] = jnp.zeros_like(carry_ref)

    a = None
    lo = 0
    for o_ref in (osb_ref, odil_ref, ofox_ref):
        w = o_ref.shape[-1]
        on = (_rms_rows(o_ref[...].astype(F32)) * gmix_ref[:, lo:lo + w]).astype(BF16)
        part = jnp.dot(on, wout_ref[lo:lo + w, :], preferred_element_type=F32)
        a = part if a is None else a + part
        lo += w
    xn = x_ref[...] + ga_ref[...] * a
    xo_ref[...] = xn
    h2 = _rms_rows(xn) * gffn_ref[...] * (1.0 + sc_ref[...]) + sh_ref[...]
    h2_ref[...] = h2

    h_hi = h2.astype(BF16)
    h_lo = (h2 - h_hi.astype(F32)).astype(BF16)
    both = jnp.dot(h_hi, wr_ref[...], preferred_element_type=F32)
    logits = (both[:, :LANES] + both[:, LANES:]
              + jnp.dot(h_lo, wr_ref[:, :LANES], preferred_element_type=F32) + br_ref[...])
    tm = logits.shape[0]
    lane = lax.broadcasted_iota(jnp.int32, (tm, LANES), 1)
    vals, idxs = [], []
    rest = logits
    for _ in range(TOP_K):
        m = jnp.max(rest, axis=-1, keepdims=True)
        ik = jnp.min(jnp.where(rest == m, lane, LANES), axis=-1, keepdims=True)
        vals.append(m)
        idxs.append(ik)
        rest = jnp.where(lane == ik, -jnp.inf, rest)
    es = [jnp.exp(v - vals[0]) for v in vals]
    den = es[0] + es[1] + es[2] + es[3]

    hot = [(lane == ik) for ik in idxs]
    multi = jnp.where(hot[0] | hot[1] | hot[2] | hot[3], 1.0, 0.0)
    r = lax.broadcasted_iota(jnp.int32, (tm, tm), 0)
    c = lax.broadcasted_iota(jnp.int32, (tm, tm), 1)
    before = (c < r).astype(BF16)
    prior = jnp.dot(before, multi.astype(BF16), preferred_element_type=F32)
    counts = (prior[tm - 1:tm, :] + multi[tm - 1:tm, :]).astype(jnp.int32)
    group_len = (counts + (GROUP_ALIGN - 1)) // GROUP_ALIGN * GROUP_ALIGN
    er = lax.broadcasted_iota(jnp.int32, (LANES, LANES), 0)
    ec = lax.broadcasted_iota(jnp.int32, (LANES, LANES), 1)
    lens8 = jnp.broadcast_to(group_len.astype(F32), (8, LANES)).astype(BF16)
    group_start = jnp.dot(lens8, (er < ec).astype(BF16), preferred_element_type=F32)[0:1, :]
    where_in_tile = group_start + prior
    pos_out = jnp.zeros((tm, LANES), F32)
    gate_out = jnp.zeros((tm, LANES), F32)
    for k in range(TOP_K):
        pk = jnp.sum(jnp.where(hot[k], where_in_tile, 0.0), axis=-1, keepdims=True)
        pos_out = jnp.where(lane == k, pk, pos_out)
        gate_out = jnp.where(lane == k, es[k] / den, gate_out)
    pos_ref[...] = pos_out.astype(jnp.int32)
    post_ref[...] = pos_out.T[0:8, :].astype(jnp.int32)
    gate_ref[...] = gate_out
    start_ref[...] = group_start.astype(jnp.int32)
    len_ref[...] = group_len
    base_ref[...] = carry_ref[...]
    carry_ref[...] = carry_ref[...] + group_len
    total_ref[...] = carry_ref[...]


def outproj_router(o_sb, o_dil, o_fox, g_mix, w_out, x, mod, g_ffn, w_r, b_r):
    b, s, d = x.shape
    tm = min(ROW_TILE, s)
    n = b * s

    def act(w):
        return pl.BlockSpec((None, tm, w), lambda i, j: (i, j, 0))

    def modspec(col):
        return pl.BlockSpec((None, 1, d), lambda i, j: (i, 0, col))

    def const(shape):
        return pl.BlockSpec(shape, lambda i, j: (0, 0))

    per_b = s // tm
    tok = pl.BlockSpec((tm, LANES), lambda i, j: (i * per_b + j, 0))
    per_tile = pl.BlockSpec((None, 1, LANES), lambda i, j: (i * per_b + j, 0, 0))
    ntiles = b * per_b
    tile_tab = jax.ShapeDtypeStruct((ntiles, 1, LANES), jnp.int32)
    return pl.pallas_call(
        _outproj_router_kernel,
        grid=(b, per_b),
        in_specs=[act(o_sb.shape[-1]), act(o_dil.shape[-1]), act(o_fox.shape[-1]),
                  const((1, d)), const(w_out.shape), act(d),
                  modspec(2), modspec(4), modspec(3),
                  const((1, d)), const((d, 2 * LANES)), const((1, LANES))],
        out_specs=[act(d), pl.BlockSpec((tm, d), lambda i, j: (i * per_b + j, 0)), tok,
                   pl.BlockSpec((None, 8, tm), lambda i, j: (i * per_b + j, 0, 0)), tok,
                   per_tile, per_tile, per_tile, const((1, LANES))],
        out_shape=[jax.ShapeDtypeStruct((b, s, d), F32), jax.ShapeDtypeStruct((n, d), F32),
                   jax.ShapeDtypeStruct((n, LANES), jnp.int32), jax.ShapeDtypeStruct((ntiles, 8, tm), jnp.int32),
                   jax.ShapeDtypeStruct((n, LANES), F32), tile_tab, tile_tab, tile_tab,
                   jax.ShapeDtypeStruct((1, LANES), jnp.int32)],
        scratch_shapes=[pltpu.VMEM((1, LANES), jnp.int32)],
        compiler_params=_params(("arbitrary", "arbitrary")),
        name="outproj_router",
    )(o_sb, o_dil, o_fox, g_mix, w_out, x, mod, mod, mod, g_ffn, w_r, b_r)


GROUP_ALIGN = 8
GROUP_CHUNKS = (512, 256, 128, 64, 32, 16, 8)


def sorted_rows(tm, n_experts):
    return -(-(TOP_K * tm + n_experts * (GROUP_ALIGN - 1)) // 256) * 256


def _for_each_chunk(len_ref, dst_ref, first, count, fn):
    def body(e, _):
        length = len_ref[first + e]
        hbm_row = dst_ref[first + e]
        offset = jnp.int32(0)
        for size in GROUP_CHUNKS:
            hit = (length & size) != 0

            @pl.when(hit)
            def _(offset=offset, hbm_row=hbm_row, size=size):
                fn(size, first + e, pl.multiple_of(offset, GROUP_ALIGN), pl.multiple_of(hbm_row, GROUP_ALIGN))

            step = jnp.where(hit, size, 0)
            offset = offset + step
            hbm_row = hbm_row + step
        return 0

    lax.fori_loop(0, count, body, 0)


def _dispatch_kernel(start_ref, len_ref, dst_ref, tail_len_ref, tail_dst_ref, nused_ref, h_ref, post_ref, xs_hbm,
                     buf_ref, sem, *, ne, te):
    rows, tm = buf_ref.shape[1], h_ref.shape[0]
    nblk = xs_hbm.shape[0] // te
    j = pl.program_id(0)
    slot = j % 2

    @pl.when(j == 0)
    def _():
        zero_rows, zsem = buf_ref.at[1], sem.at[1]
        zero_rows[0:te, :] = jnp.zeros((te, buf_ref.shape[2]), F32)

        def tail(size, e, offset, hbm_row):
            return pltpu.make_async_copy(zero_rows.at[pl.ds(0, size)], xs_hbm.at[pl.ds(hbm_row, size)], zsem)

        def block(i):
            dst = xs_hbm.at[pl.ds(pl.multiple_of(i * te, te), te)]
            return pltpu.make_async_copy(zero_rows.at[pl.ds(0, te)], dst, zsem)

        _for_each_chunk(tail_len_ref, tail_dst_ref, 0, ne, lambda *a: tail(*a).start())
        lax.fori_loop(nused_ref[0], nblk, lambda i, c: (block(i).start(), c)[1], 0)
        _for_each_chunk(tail_len_ref, tail_dst_ref, 0, ne, lambda *a: tail(*a).wait())
        lax.fori_loop(nused_ref[0], nblk, lambda i, c: (block(i).wait(), c)[1], 0)

    r = lax.broadcasted_iota(jnp.int32, (rows, tm), 0)
    pick = r == post_ref[0:1, :]
    for k in range(1, TOP_K):
        pick = pick | (r == post_ref[k:k + 1, :])
    buf_ref[slot] = jnp.dot(jnp.where(pick, 1.0, 0.0).astype(BF16), h_ref[...].astype(BF16),
                            preferred_element_type=F32)

    def copy(which):
        def make(size, g, offset, hbm_row):
            buf_row = pl.multiple_of(start_ref[g] + offset, GROUP_ALIGN)
            return pltpu.make_async_copy(buf_ref.at[which, pl.ds(buf_row, size)], xs_hbm.at[pl.ds(hbm_row, size)],
                                         sem.at[which])
        return make

    @pl.when(j > 0)
    def _():
        _for_each_chunk(len_ref, dst_ref, (j - 1) * ne, ne, lambda *a: copy(1 - slot)(*a).wait())

    _for_each_chunk(len_ref, dst_ref, j * ne, ne, lambda *a: copy(slot)(*a).start())

    @pl.when(j == pl.num_programs(0) - 1)
    def _():
        _for_each_chunk(len_ref, dst_ref, j * ne, ne, lambda *a: copy(slot)(*a).wait())


def dispatch_rows(tabs, tails, nused, h, post, cap, ne, te):
    n, d = h.shape
    tm = post.shape[-1]
    return pl.pallas_call(
        functools.partial(_dispatch_kernel, ne=ne, te=te),
        grid_spec=pltpu.PrefetchScalarGridSpec(
            num_scalar_prefetch=6,
            grid=(n // tm,),
            in_specs=[pl.BlockSpec((tm, d), lambda i, *_: (i, 0)),
                      pl.BlockSpec((None, 8, tm), lambda i, *_: (i, 0, 0))],
            out_specs=pl.BlockSpec(memory_space=pl.ANY),
            scratch_shapes=[pltpu.VMEM((2, sorted_rows(tm, ne), d), F32), pltpu.SemaphoreType.DMA((2,))]),
        out_shape=jax.ShapeDtypeStruct((cap, d), F32),
        compiler_params=pltpu.CompilerParams(dimension_semantics=("arbitrary",), has_side_effects=True,
                                             vmem_limit_bytes=VMEM_LIMIT),
        name="dispatch_rows",
    )(*tabs, *tails, nused, h, post)


def _expert_kernel(blk_e_ref, nxt_ref, nused_ref, xs_ref, wgu_hbm, bgu_ref, wdn_hbm, bdn_ref, o_ref,
                   wgu_f32, wdn_f32, wgu_bf, wdn_bf, sem, *, layer):
    i = pl.program_id(0)
    used = i < nused_ref[0]

    def fetch(e):
        return (pltpu.make_async_copy(wgu_hbm.at[layer, e], wgu_f32, sem.at[0]),
                pltpu.make_async_copy(wdn_hbm.at[layer, e], wdn_f32, sem.at[1]))

    @pl.when(i == 0)
    def _():
        for cp in fetch(blk_e_ref[0]):
            cp.start()

    @pl.when(jnp.logical_not(used))
    def _():
        o_ref[...] = jnp.zeros_like(o_ref)

    @pl.when(used & ((i == 0) | (blk_e_ref[i] != blk_e_ref[jnp.maximum(i - 1, 0)])))
    def _():
        for cp in fetch(blk_e_ref[i]):
            cp.wait()
        wgu_bf[...] = wgu_f32[...].astype(BF16)
        wdn_bf[...] = wdn_f32[...].astype(BF16)

        @pl.when(nxt_ref[i] >= 0)
        def _():
            for cp in fetch(nxt_ref[i]):
                cp.start()

    @pl.when(used)
    def _():
        f = wdn_bf.shape[0]
        gu = jnp.dot(xs_ref[...].astype(BF16), wgu_bf[...], preferred_element_type=F32) + bgu_ref[...]
        gate = jnp.minimum(gu[:, :f], SWIGLU_LIMIT)
        up = jnp.clip(gu[:, f:], -SWIGLU_LIMIT, SWIGLU_LIMIT)
        act = (up + 1.0) * (gate * jax.nn.sigmoid(SWIGLU_ALPHA * gate))
        o_ref[...] = jnp.dot(act.astype(BF16), wdn_bf[...], preferred_element_type=F32) + bdn_ref[...]


def expert_ffn(blk_e, nxt_e, nused, xs, layer, w_gu, b_gu, w_dn, b_dn, te):
    cap, d = xs.shape
    f2 = w_gu.shape[-1]
    f = f2 // 2
    nblk = cap // te

    def rows(i, be, nx, nu):
        return (jnp.minimum(i, nu[0] - 1), 0)

    def per_expert(i, be, nx, nu):
        return (layer, be[i], 0, 0)

    return pl.pallas_call(
        functools.partial(_expert_kernel, layer=layer),
        grid_spec=pltpu.PrefetchScalarGridSpec(
            num_scalar_prefetch=3,
            grid=(nblk,),
            in_specs=[pl.BlockSpec((te, d), rows),
                      pl.BlockSpec(memory_space=pl.ANY),
                      pl.BlockSpec((None, None, 1, f2), per_expert),
                      pl.BlockSpec(memory_space=pl.ANY),
                      pl.BlockSpec((None, None, 1, d), per_expert)],
            out_specs=pl.BlockSpec((te, d), lambda i, be, nx, nu: (i, 0)),
            scratch_shapes=[pltpu.VMEM((d, f2), F32), pltpu.VMEM((f, d), F32),
                            pltpu.VMEM((d, f2), BF16), pltpu.VMEM((f, d), BF16), pltpu.SemaphoreType.DMA((2,))]),
        out_shape=jax.ShapeDtypeStruct((cap, d), F32),
        compiler_params=_params(("arbitrary",), EXPERT_VMEM_LIMIT),
        name="expert_ffn",
    )(blk_e, nxt_e, nused, xs, w_gu, b_gu, w_dn, b_dn)


def _combine_kernel(start_ref, len_ref, dst_ref, ys_hbm, pos_ref, gate_ref, x_ref, ga_ref, gfin_ref, xo_ref, buf_ref,
                    sem, *, ne, final):
    j = pl.program_id(0)
    slot = j % 2

    def copy(which):
        def make(size, g, offset, hbm_row):
            buf_row = pl.multiple_of(start_ref[g] + offset, GROUP_ALIGN)
            return pltpu.make_async_copy(ys_hbm.at[pl.ds(hbm_row, size)], buf_ref.at[which, pl.ds(buf_row, size)],
                                         sem.at[which])
        return make

    @pl.when(j == 0)
    def _():
        buf_ref[...] = jnp.zeros_like(buf_ref)
        _for_each_chunk(len_ref, dst_ref, 0, ne, lambda *a: copy(0)(*a).start())

    @pl.when(j + 1 < pl.num_programs(0))
    def _():
        _for_each_chunk(len_ref, dst_ref, (j + 1) * ne, ne, lambda *a: copy(1 - slot)(*a).start())

    rows, tm = buf_ref.shape[1], pos_ref.shape[0]
    r = lax.broadcasted_iota(jnp.int32, (tm, rows), 1)
    pos, g = pos_ref[...], gate_ref[...]
    weights = jnp.zeros((tm, rows), F32)
    for k in range(TOP_K):
        weights = jnp.where(r == pos[:, k:k + 1], g[:, k:k + 1], weights)
    _for_each_chunk(len_ref, dst_ref, j * ne, ne, lambda *a: copy(slot)(*a).wait())
    y = jnp.dot(weights.astype(BF16), buf_ref[slot].astype(BF16), preferred_element_type=F32)
    xo = x_ref[...] + ga_ref[...] * y
    xo_ref[...] = _rms_rows(xo) * gfin_ref[...] if final else xo


def combine_rows(tabs, ys, pos, gates, x, mod, g_final, ne, final):
    b, s, d = x.shape
    tm = min(ROW_TILE, s)
    per_b = s // tm
    return pl.pallas_call(
        functools.partial(_combine_kernel, ne=ne, final=final),
        grid_spec=pltpu.PrefetchScalarGridSpec(
            num_scalar_prefetch=3,
            grid=(b * per_b,),
            in_specs=[pl.BlockSpec(memory_space=pl.ANY),
                      pl.BlockSpec((tm, LANES), lambda i, *_: (i, 0)),
                      pl.BlockSpec((tm, LANES), lambda i, *_: (i, 0)),
                      pl.BlockSpec((None, tm, d), lambda i, *_: (i // per_b, i % per_b, 0)),
                      pl.BlockSpec((None, 1, d), lambda i, *_: (i // per_b, 0, 5)),
                      pl.BlockSpec((1, d), lambda i, *_: (0, 0))],
            out_specs=pl.BlockSpec((None, tm, d), lambda i, *_: (i // per_b, i % per_b, 0)),
            scratch_shapes=[pltpu.VMEM((2, sorted_rows(tm, ne), d), F32), pltpu.SemaphoreType.DMA((2,))]),
        out_shape=jax.ShapeDtypeStruct((b, s, d), F32),
        compiler_params=_params(("arbitrary",), EXPERT_VMEM_LIMIT),
        name="combine_rows",
    )(*tabs, ys, pos, gates, x, mod, g_final)


def _routing_tables(starts, lens, bases, totals, n_experts, te, nblk):
    totals = totals[0, :n_experts]
    padded = (totals + te - 1) // te * te
    ends = jnp.cumsum(padded)
    first_row = ends - padded
    dst = (first_row[None, :] + bases[:, 0, :n_experts]).reshape(-1).astype(jnp.int32)
    tabs = (starts[:, 0, :n_experts].reshape(-1), lens[:, 0, :n_experts].reshape(-1), dst)
    tails = ((padded - totals).astype(jnp.int32), (first_row + totals).astype(jnp.int32))
    nused = (ends[-1] // te).astype(jnp.int32)
    blk = jnp.arange(nblk, dtype=jnp.int32)
    blk_e = jnp.sum((ends[None, :] <= (blk * te)[:, None]).astype(jnp.int32), axis=1)
    blk_e = jnp.minimum(blk_e, n_experts - 1)
    blk_e = jnp.where(blk < nused, blk_e, blk_e[nused - 1])
    ids = jnp.arange(n_experts, dtype=jnp.int32)
    later = jnp.where((padded[None, :] > 0) & (ids[None, :] > ids[:, None]), ids[None, :], n_experts)
    nxt = jnp.min(later, axis=1)
    nxt_e = jnp.where(nxt < n_experts, nxt, -1).astype(jnp.int32)[blk_e]
    return tabs, tails, blk_e, nxt_e, nused.reshape(1)


def _split_bf16(w):
    hi = w.astype(BF16)
    return jnp.concatenate([hi, (w - hi.astype(F32)).astype(BF16)], axis=1)


def _pad_lanes(a, fill=0.0):
    return jnp.pad(a, [(0, 0)] * (a.ndim - 1) + [(0, LANES - a.shape[-1])], constant_values=fill)


def kernel(x, c, positions, w_mod, b_mod, g_attn, w_in, b_forget, g_mix, w_out, g_ffn, w_router, b_router,
           w_gate_up, b_gate_up, w_down, b_down, g_final):
    b, s, d = x.shape
    depth = w_mod.shape[0]
    n_experts = w_router.shape[-1]
    d_mix = w_out.shape[1]
    n = b * s
    te = min(EXPERT_TILE, n)
    ntiles = n // min(ROW_TILE, s)
    nblk = -(-(n * TOP_K + ntiles * n_experts * (GROUP_ALIGN - 1) + n_experts * (te - 1)) // te)
    cap = nblk * te

    mod_all = modulation(c, w_mod, b_mod)
    cos, sin = rope_tables(positions)
    dil_bias = dilation_bias(dil_tile(s))
    for layer in range(depth):
        mod = mod_all[layer].reshape(b, 1, N_MOD * d)
        w_qkv = family_major_qkv_weight(w_in[layer], d_mix)
        w_f = _pad_lanes(w_in[layer, :, 3 * d_mix:]).astype(BF16)
        b_f = _pad_lanes(b_forget[layer][None, :])
        qkv_sb, qkv_dil, qkv_fox, logf = in_projection(x, mod, g_attn[layer][None, :], w_qkv, w_f, b_f)
        qaug, kaug = forget_cumsum(logf, N_HEADS_FOX)
        o_sb = sb_attention(qkv_sb)
        o_dil = dil_attention(qkv_dil, cos, sin, dil_bias)
        o_fox = fox_attention(qkv_fox, qaug, kaug)
        x, h2, pos, post, gates, starts, lens, bases, totals = outproj_router(
            o_sb, o_dil, o_fox, g_mix[layer][None, :], w_out[layer].astype(BF16), x, mod, g_ffn[layer][None, :],
            _split_bf16(_pad_lanes(w_router[layer])), _pad_lanes(b_router[layer][None, :], NEG_BIG))
        tabs, tails, blk_e, nxt_e, nused = _routing_tables(starts, lens, bases, totals, n_experts, te, nblk)
        xs = dispatch_rows(tabs, tails, nused, h2, post, cap, n_experts, te)
        ys = expert_ffn(blk_e, nxt_e, nused, xs, layer, w_gate_up, b_gate_up[:, :, None, :], w_down,
                        b_down[:, :, None, :], te)
        x = combine_rows(tabs, ys, pos, gates, x, mod, g_final[None, :], n_experts, final=layer == depth - 1)
    return x
```
